```python
import jax, jax.numpy as jnp
from jax import lax
import numpy as np

D_MODEL = 1024
BATCH = 16
SEQ = 2048
DEPTH = 4
DEC_BATCH = 4
DEC_SEQ = 4096
PAST_LEN = 128

GRID_W = 64
BRANCH_W = 256
N_BRANCH = 4
HEAD_DIM = 64
ATTN_SCALE = HEAD_DIM ** -0.5
NA_HEADS = BRANCH_W // HEAD_DIM
NA_KH = 8
NA_KW = 16
NA_QB = 16
NA_KSPAN = 32
HG_HEADS = 4
HG_DK = BRANCH_W // HG_HEADS
HG_DV = BRANCH_W // HG_HEADS
HG_CHUNK = 64
SG_GROUPS = 4
SG_CHUNK = 128
DIL_HEADS = BRANCH_W // HEAD_DIM
DIL_CONFIGS = ((128, 1), (512, 4), (2048, 16))
D_FF = -(-8 * D_MODEL // (3 * 256)) * 256
IN_SIZES = (3 * BRANCH_W,
            BRANCH_W, BRANCH_W, BRANCH_W, BRANCH_W, BRANCH_W,
            2 * BRANCH_W,
            3 * BRANCH_W)
IN_WIDTH = sum(IN_SIZES)
NEG = -1e30
EPS = 1e-6

kernel_name = "hybrid_gated_encoder_na_hgrn2_sgu_dilated"


def _rms_norm(x, g):
    xf = x.astype(jnp.float32)
    y = xf * lax.rsqrt(jnp.mean(xf * xf, axis=-1, keepdims=True) + EPS)
    return (y * g.astype(jnp.float32)).astype(x.dtype)


def _heads(t, n):
    return t.reshape(t.shape[0], t.shape[1], n, -1)


def _alibi_slopes(n):
    start = 2.0 ** (-8.0 / n)
    return np.array([start ** (h + 1) for h in range(n)], np.float32)


def _neighbourhood_attention(q, k, v, rpb):
    B, L, H, hd = q.shape
    rows = L // GRID_W
    kh = min(NA_KH, rows)
    ncb = GRID_W // NA_QB
    r = np.arange(rows)
    row_idx = np.clip(r - kh // 2, 0, rows - kh)[:, None] + np.arange(kh)[None, :]
    qcol = np.arange(GRID_W).reshape(ncb, NA_QB)
    col0 = np.clip(np.arange(ncb) * NA_QB - NA_KW // 2, 0, GRID_W - NA_KSPAN)
    col_idx = col0[:, None] + np.arange(NA_KSPAN)[None, :]
    win0 = np.clip(qcol - NA_KW // 2, 0, GRID_W - NA_KW)
    kc = col_idx[:, None, :]
    col_valid = (kc >= win0[..., None]) & (kc < win0[..., None] + NA_KW)
    row_off = row_idx - r[:, None] + NA_KH - 1
    col_off = np.clip(kc - qcol[..., None], -(NA_KW - 1), NA_KW - 1) + NA_KW - 1
    bias = rpb[:, row_off[:, None, None, :, None], col_off[None, :, :, None, :]]
    bias = jnp.where(col_valid[None, None, :, :, None, :], bias.astype(jnp.float32), NEG)
    qg = q.reshape(B, rows, ncb, NA_QB, H, hd)
    kg = k.reshape(B, rows, GRID_W, H, hd)[:, row_idx][:, :, :, col_idx]
    vg = v.reshape(B, rows, GRID_W, H, hd)[:, row_idx][:, :, :, col_idx]
    s = jnp.einsum('brnqhd,brknjhd->bhrnqkj', qg, kg, preferred_element_type=jnp.float32) * ATTN_SCALE + bias[None]
    p = jax.nn.softmax(s.reshape(B, H, rows, ncb, NA_QB, kh * NA_KSPAN), axis=-1).reshape(s.shape)
    o = jnp.einsum('bhrnqkj,brknjhd->brnqhd', p.astype(v.dtype), vg)
    return o.reshape(B, L, H, hd)


def _hgrn2_scan(q, k, logf, v):
    B, L, H, dk = q.shape
    dv = v.shape[-1]
    C = HG_CHUNK
    nc = L // C

    def chunks(t):
        return t.reshape(B, nc, C, H, t.shape[-1]).transpose(1, 0, 3, 2, 4)

    qc, kc, gc, vc = chunks(q), chunks(k), chunks(logf), chunks(v)
    A = jnp.cumsum(gc, axis=3)
    lower = np.tril(np.ones((C, C), bool))[:, :, None]

    def step(S, inp):
        qi, ki, ai, vi = inp
        diff = ai[:, :, :, None, :] - ai[:, :, None, :, :]
        decay = jnp.where(lower, jnp.exp(jnp.where(lower, diff, 0.0)), 0.0)
        att = jnp.einsum('bhti,bhtsi,bhsi->bhts', qi, decay, ki)
        o = jnp.einsum('bhts,bhsj->bhtj', att, vi) + jnp.einsum('bhti,bhij->bhtj', qi * jnp.exp(ai), S)
        a_last = ai[:, :, -1, :]
        k_dec = ki * jnp.exp(a_last[:, :, None, :] - ai)
        S = S * jnp.exp(a_last)[..., None] + jnp.einsum('bhsi,bhsj->bhij', k_dec, vi)
        return S, o

    S0 = jnp.zeros((B, H, dk, dv), jnp.float32)
    _, o = lax.scan(step, S0, (qc, kc, A, vc))
    return o.transpose(1, 0, 3, 2, 4).reshape(B, L, H, dv)


def _hgrn2_bidirectional(f_fwd, f_bwd, q_pre, i_pre, g_pre, lb, gnorm):
    B, L, _ = q_pre.shape
    f32 = jnp.float32
    q = jax.nn.silu(_heads(q_pre, HG_HEADS).astype(f32))
    v = _heads(i_pre, HG_HEADS).astype(f32)

    def gates(f_pre, lbd):
        x = _heads(f_pre, HG_HEADS).astype(f32)
        lbd = lbd.reshape(HG_HEADS, HG_DK)
        f = lbd + (1.0 - lbd) * jax.nn.sigmoid(x)
        kk = (1.0 - lbd) * jax.nn.sigmoid(-x)
        return kk, jnp.log(f)

    k_f, lf_f = gates(f_fwd, lb[0])
    k_b, lf_b = gates(f_bwd, lb[1])
    flip = lambda t: jnp.flip(t, axis=1)
    o = _hgrn2_scan(q, k_f, lf_f, v) + flip(_hgrn2_scan(flip(q), flip(k_b), flip(lf_b), flip(v)))
    o = o * lax.rsqrt(jnp.mean(o * o, axis=-1, keepdims=True) + EPS)
    return o.reshape(B, L, BRANCH_W) * gnorm.astype(f32) * jax.nn.silu(g_pre.astype(f32))


def _spatial_gating(uv, ln_g, ln_b, w_s, b_s):
    B, L, _ = uv.shape
    z = jax.nn.gelu(uv.astype(jnp.float32))
    u, v = jnp.split(z, 2, axis=-1)
    mu = jnp.mean(v, axis=-1, keepdims=True)
    var = jnp.mean(jnp.square(v - mu), axis=-1, keepdims=True)
    v = (v - mu) * lax.rsqrt(var + EPS) * ln_g.astype(jnp.float32) + ln_b.astype(jnp.float32)
    v = v.reshape(B, L // SG_CHUNK, SG_CHUNK, SG_GROUPS, BRANCH_W // SG_GROUPS)
    mixed = jnp.einsum('gts,bnsgc->bntgc', w_s.astype(jnp.float32), v) + b_s.astype(jnp.float32).T[:, :, None]
    return u * mixed.reshape(B, L, BRANCH_W)


def _dilated_group(q, k, v, half, dil, slopes):
    B, L, H, hd = q.shape
    n = L // dil
    blk = half // dil
    nb = -(-n // blk)
    n_pad = nb * blk

    def to_classes(t):
        return t.reshape(B, n, dil, H, hd).transpose(0, 2, 3, 1, 4)

    pad_q = ((0, 0), (0, 0), (0, 0), (0, n_pad - n), (0, 0))
    pad_k = ((0, 0), (0, 0), (0, 0), (blk, n_pad - n + blk), (0, 0))
    qc = jnp.pad(to_classes(q), pad_q).reshape(B, dil, H, nb, blk, hd)

    def key_windows(t):
        tb = jnp.pad(to_classes(t), pad_k).reshape(B, dil, H, nb + 2, blk, hd)
        return jnp.concatenate([tb[:, :, :, :-2], tb[:, :, :, 1:-1], tb[:, :, :, 2:]], axis=4)

    kw, vw = key_windows(k), key_windows(v)
    qi = np.arange(n_pad).reshape(nb, blk)
    ki = (np.arange(nb)[:, None] - 1) * blk + np.arange(3 * blk)[None, :]
    step = ki[:, None, :] - qi[:, :, None]
    valid = (np.abs(step) <= blk) & (ki[:, None, :] >= 0) & (ki[:, None, :] < n)
    dist = (np.abs(step) * dil).astype(np.float32)
    bias = jnp.where(valid[None], -slopes[:, None, None, None] * dist[None], NEG)
    s = jnp.einsum('bchnqd,bchnkd->bchnqk', qc, kw, preferred_element_type=jnp.float32) * ATTN_SCALE + bias[None, None]
    m = jnp.max(s, axis=-1)
    e = jnp.exp(s - m[..., None])
    den = jnp.sum(e, axis=-1)
    o = jnp.einsum('bchnqk,bchnkd->bchnqd', e, vw.astype(jnp.float32)) / den[..., None]

    def from_classes(t):
        t = t.reshape(B, dil, H, n_pad, *t.shape[5:])[:, :, :, :n]
        t = jnp.moveaxis(t, 3, 1)
        return t.reshape(B, L, H, *t.shape[4:])

    return from_classes(o), from_classes(m), from_classes(den)


def _dilated_attention(q, k, v):
    slopes = jnp.asarray(_alibi_slopes(DIL_HEADS))
    outs, maxes, dens = [], [], []
    for window, dil in DIL_CONFIGS:
        o, m, d = _dilated_group(q, k, v, window // 2, dil, slopes)
        outs.append(o); maxes.append(m); dens.append(d)
    ms = jnp.stack(maxes)
    wts = jnp.stack(dens) * jnp.exp(ms - jnp.max(ms, axis=0))
    wts = wts / jnp.sum(wts, axis=0)
    return jnp.einsum('gblh,gblhd->blhd', wts, jnp.stack(outs))


def _token_mixer(h, w_in, w_gate, b_gate, w_branch, w_o, na_rpb, lb, hg_gnorm, sg_ln_g, sg_ln_b, sg_w, sg_b):
    B, L, _ = h.shape
    proj = h @ w_in
    cuts = [int(i) for i in np.cumsum(IN_SIZES)[:-1]]
    na_qkv, hg_ff, hg_fb, hg_q, hg_i, hg_g, sg_uv, dl_qkv = jnp.split(proj, cuts, axis=-1)
    nq, nk, nv = jnp.split(na_qkv, 3, axis=-1)
    y_na = _neighbourhood_attention(_heads(nq, NA_HEADS), _heads(nk, NA_HEADS), _heads(nv, NA_HEADS), na_rpb)
    y_hg = _hgrn2_bidirectional(hg_ff, hg_fb, hg_q, hg_i, hg_g, lb, hg_gnorm)
    y_sg = _spatial_gating(sg_uv, sg_ln_g, sg_ln_b, sg_w, sg_b)
    dq, dk, dv = jnp.split(dl_qkv, 3, axis=-1)
    y_dl = _dilated_attention(_heads(dq, DIL_HEADS), _heads(dk, DIL_HEADS), _heads(dv, DIL_HEADS))
    branches = (y_na.reshape(B, L, BRANCH_W), y_hg, y_sg, y_dl.reshape(B, L, BRANCH_W))
    merged = jnp.zeros_like(h)
    for bi in range(N_BRANCH):
        gate = jax.nn.sigmoid(h @ w_gate[bi] + b_gate[bi])
        merged = merged + gate * (branches[bi].astype(h.dtype) @ w_branch[bi])
    return merged @ w_o


def _swiglu(h, wg, wu, wd):
    return (jax.nn.silu(h @ wg) * (h @ wu)) @ wd


def _trunk(x, c, w_ada, b_ada, g_norm_mix, g_norm_ffn, w_in, w_gate, b_gate, w_branch, w_o,
           na_rpb, hg_lb, hg_gnorm, sg_ln_g, sg_ln_b, sg_w, sg_b, w_ffn_gate, w_ffn_up, w_ffn_down, g_final):
    lb_soft = jax.nn.softmax(hg_lb.astype(jnp.float32), axis=1)
    lower_bounds = jnp.cumsum(lb_soft, axis=1) - lb_soft[:, :1]
    cs = jax.nn.silu(c)
    for l in range(DEPTH):
        mod = cs @ w_ada[l] + b_ada[l]
        sh1, sc1, g1, sh2, sc2, g2 = [m[:, None, :] for m in jnp.split(mod, 6, axis=-1)]
        h = _rms_norm(x, g_norm_mix[l]) * (1 + sc1) + sh1
        y = _token_mixer(h, w_in[l], w_gate[l], b_gate[l], w_branch[l], w_o[l], na_rpb[l], lower_bounds[:, l],
                         hg_gnorm[l], sg_ln_g[l], sg_ln_b[l], sg_w[l], sg_b[l])
        x = x + (g1 * y).astype(x.dtype)
        h = _rms_norm(x, g_norm_ffn[l]) * (1 + sc2) + sh2
        x = x + (g2 * _swiglu(h, w_ffn_gate[l], w_ffn_up[l], w_ffn_down[l])).astype(x.dtype)
    return _rms_norm(x, g_final)


def setup_inputs(seed: int = 0) -> dict:
    key = jax.random.key(seed)
    ks = jax.random.split(key, 32)
    nrm = lambda i, shape, s: jax.random.normal(ks[i], shape, jnp.float32) * s
    D = D_MODEL
    return {
        "x_prompt": nrm(0, (BATCH, SEQ, D), 1.0),
        "x_sample": nrm(1, (DEC_BATCH, DEC_SEQ, D), 1.0),
        "c_prompt": nrm(2, (BATCH, D), 1.0),
        "c_sample": nrm(3, (DEC_BATCH, D), 1.0),
        "w_ada": nrm(4, (DEPTH, D, 6 * D), 0.5 * D ** -0.5),
        "b_ada": nrm(5, (DEPTH, 6 * D), 0.01),
        "g_norm_mix": 1.0 + nrm(6, (DEPTH, D), 0.1),
        "g_norm_ffn": 1.0 + nrm(7, (DEPTH, D), 0.1),
        "w_in": nrm(8, (DEPTH, D, IN_WIDTH), D ** -0.5),
        "w_gate": nrm(9, (DEPTH, N_BRANCH, D, D), D ** -0.5),
        "b_gate": nrm(10, (DEPTH, N_BRANCH, D), 0.01),
        "w_branch": nrm(11, (DEPTH, N_BRANCH, BRANCH_W, D), BRANCH_W ** -0.5),
        "w_o": nrm(12, (DEPTH, D, D), D ** -0.5),
        "na_rpb": nrm(13, (DEPTH, NA_HEADS, 2 * NA_KH - 1, 2 * NA_KW - 1), 0.2),
        "hg_lb": nrm(14, (2, DEPTH, HG_HEADS * HG_DK), 1.0),
        "hg_gnorm": 1.0 + nrm(15, (DEPTH, BRANCH_W), 0.1),
        "sg_ln_g": 1.0 + nrm(16, (DEPTH, BRANCH_W), 0.1),
        "sg_ln_b": nrm(17, (DEPTH, BRANCH_W), 0.01),
        "sg_w": nrm(18, (DEPTH, SG_GROUPS, SG_CHUNK, SG_CHUNK), SG_CHUNK ** -0.5),
        "sg_b": 1.0 + nrm(19, (DEPTH, SG_GROUPS, SG_CHUNK), 0.02),
        "w_ffn_gate": nrm(20, (DEPTH, D, D_FF), D ** -0.5),
        "w_ffn_up": nrm(21, (DEPTH, D, D_FF), D ** -0.5),
        "w_ffn_down": nrm(22, (DEPTH, D_FF, D), D_FF ** -0.5),
        "g_final": 1.0 + nrm(23, (D,), 0.1),
    }


def reference(x_prompt, x_sample, c_prompt, c_sample, w_ada, b_ada, g_norm_mix, g_norm_ffn, w_in, w_gate, b_gate,
              w_branch, w_o, na_rpb, hg_lb, hg_gnorm, sg_ln_g, sg_ln_b, sg_w, sg_b, w_ffn_gate, w_ffn_up,
              w_ffn_down, g_final):
    params = (w_ada, b_ada, g_norm_mix, g_norm_ffn, w_in, w_gate, b_gate, w_branch, w_o, na_rpb, hg_lb, hg_gnorm,
              sg_ln_g, sg_ln_b, sg_w, sg_b, w_ffn_gate, w_ffn_up, w_ffn_down, g_final)
    y_prompt = _trunk(x_prompt, c_prompt, *params)
    y_sample = _trunk(x_sample, c_sample, *params)
    return (y_prompt, y_sample)
```

```python
import functools

import jax
import jax.numpy as jnp
import numpy as np
from jax import lax
from jax.experimental import pallas as pl
from jax.experimental.pallas import tpu as pltpu

D_MODEL = 1024
DEPTH = 4
GRID_W = 64
BRANCH_W = 256
N_BRANCH = 4
HEAD_DIM = 64
N_HEADS = BRANCH_W // HEAD_DIM
ATTN_SCALE = HEAD_DIM ** -0.5
NA_KH = 8
NA_KW = 16
HG_CHUNK = 64
HG_SUB = 16
SG_GROUPS = 4
SG_CHUNK = 128
DIL_CONFIGS = ((128, 1), (512, 4), (2048, 16))
DIL_BLK = 64
D_FF = 2816
IN_WIDTH = 3328
NEG = -1e30
EPS = 1e-6

NA_ROWS_PER_STEP = 8
NA_TOK = NA_ROWS_PER_STEP * GRID_W
NA_WIN = NA_KH * GRID_W

VMEM_LIMIT = 56 * 1024 * 1024

BF16 = jnp.bfloat16
F32 = jnp.float32


def _bf(x):
    return x.astype(BF16)


def _dot(a, b):
    return jnp.dot(a, b, preferred_element_type=F32)


def _dot_nt(a, b):
    return lax.dot_general(a, b, (((1,), (1,)), ((), ())), preferred_element_type=F32)


def _dot_tn(a, b):
    return lax.dot_general(a, b, (((0,), (0,)), ((), ())), preferred_element_type=F32)


def _sigmoid(x):
    return 1.0 / (1.0 + jnp.exp(-x))


def _head_masks(dtype):
    lane_head = lax.broadcasted_iota(jnp.int32, (1, BRANCH_W), 1) // HEAD_DIM
    return [(lane_head == h).astype(dtype) for h in range(N_HEADS)]


def _mod_norm(x, g, scale, shift):
    y = x * lax.rsqrt(jnp.mean(x * x, axis=-1, keepdims=True) + EPS)
    return (y * g) * (1.0 + scale) + shift


def _params(*sem):
    return pltpu.CompilerParams(dimension_semantics=sem, vmem_limit_bytes=VMEM_LIMIT)


def _const_spec(shape):
    nd = len(shape)
    return pl.BlockSpec(shape, lambda *_: (0,) * nd, pipeline_mode=pl.Buffered(1))


def _ada_kernel(c_ref, w_ref, b_ref, o_ref):
    c = c_ref[...]
    cs = c * _sigmoid(c)
    o_ref[0] = _dot(_bf(cs), _bf(w_ref[0])) + b_ref[0]


def _ada_call(c_all, w_ada, b_ada):
    nb = c_all.shape[0]
    tn = 1536
    return pl.pallas_call(
        _ada_kernel,
        out_shape=jax.ShapeDtypeStruct((DEPTH, nb, 6 * D_MODEL), F32),
        grid=(DEPTH, 6 * D_MODEL // tn),
        in_specs=[pl.BlockSpec((nb, D_MODEL), lambda l, n: (0, 0)),
                  pl.BlockSpec((1, D_MODEL, tn), lambda l, n: (l, 0, n)),
                  pl.BlockSpec((1, 1, tn), lambda l, n: (l, 0, n))],
        out_specs=pl.BlockSpec((1, nb, tn), lambda l, n: (l, 0, n)),
        compiler_params=_params("arbitrary", "arbitrary"),
        name="ada",
    )(c_all, w_ada, b_ada.reshape(DEPTH, 1, 6 * D_MODEL))


PROJ_TM = 512


def _proj_kernel(x_ref, mod_ref, g_ref, w_ref, na_ref, hg_ref, sg_ref, dl_ref):
    h = _bf(_mod_norm(x_ref[0], g_ref[...], mod_ref[0, 1:2, :], mod_ref[0, 0:1, :]))
    na_ref[0] = _bf(_dot(h, w_ref[:, 0:768]))
    hg_ref[0] = _dot(h, w_ref[:, 768:2048])
    sg_ref[0] = _dot(h, w_ref[:, 2048:2560])
    dl_ref[0] = _bf(_dot(h, w_ref[:, 2560:3328]))


def _proj_call(x, mod, g, w_in):
    B, L, _ = x.shape
    tm = PROJ_TM
    tok = lambda w: pl.BlockSpec((1, tm, w), lambda b, t: (b, t, 0))
    return pl.pallas_call(
        _proj_kernel,
        out_shape=(jax.ShapeDtypeStruct((B, L, 768), BF16),
                   jax.ShapeDtypeStruct((B, L, 1280), F32),
                   jax.ShapeDtypeStruct((B, L, 512), F32),
                   jax.ShapeDtypeStruct((B, L, 768), BF16)),
        grid=(B, L // tm),
        in_specs=[tok(D_MODEL),
                  pl.BlockSpec((1, 6, D_MODEL), lambda b, t: (b, 0, 0)),
                  _const_spec((1, D_MODEL)),
                  _const_spec((D_MODEL, IN_WIDTH))],
        out_specs=(tok(768), tok(1280), tok(512), tok(768)),
        compiler_params=_params("parallel", "arbitrary"),
        name="proj",
    )(x, mod, g, w_in)


def _na_bias_table(rpb):
    delta = np.arange(NA_KH)
    k = np.arange(NA_KH)
    row_off = k[None, :] - delta[:, None] + NA_KH - 1
    c = np.arange(GRID_W)
    kc = np.arange(GRID_W)
    col_off = np.clip(kc[None, :] - c[:, None], -(NA_KW - 1), NA_KW - 1) + NA_KW - 1
    win0 = np.clip(c - NA_KW // 2, 0, GRID_W - NA_KW)
    valid = (kc[None, :] >= win0[:, None]) & (kc[None, :] < win0[:, None] + NA_KW)
    bias = rpb.astype(F32)[:, row_off[:, None, :, None], col_off[None, :, None, :]]
    bias = jnp.where(valid[None, None, :, None, :], bias, NEG)
    return bias.transpose(1, 0, 2, 3, 4).reshape(NA_KH, N_HEADS * GRID_W, NA_WIN)


def _na_kernel(q_ref, kp_ref, kc_ref, kn_ref, vp_ref, vc_ref, vn_ref, bias_ref, o_ref, kbuf, vbuf, *, rows):
    i = pl.program_id(1)
    for n, (kr, vr) in enumerate(((kp_ref, vp_ref), (kc_ref, vc_ref), (kn_ref, vn_ref))):
        kbuf[n * NA_TOK:(n + 1) * NA_TOK, :] = kr[0]
        vbuf[n * NA_TOK:(n + 1) * NA_TOK, :] = vr[0]
    masks_bf = _head_masks(BF16)
    masks = _head_masks(F32)

    def row_body(j, carry):
        r = i * NA_ROWS_PER_STEP + j
        start = jnp.clip(r - NA_KH // 2, 0, rows - NA_KH)
        off = pl.multiple_of((start - (i - 1) * NA_ROWS_PER_STEP) * GRID_W, GRID_W)
        delta = r - start
        qoff = pl.multiple_of(j * GRID_W, GRID_W)
        q = q_ref[0, pl.ds(qoff, GRID_W), :]
        kw = kbuf[pl.ds(off, NA_WIN), :]
        vw = vbuf[pl.ds(off, NA_WIN), :]
        out = jnp.zeros((GRID_W, BRANCH_W), F32)
        for h in range(N_HEADS):
            s = _dot_nt(q * masks_bf[h], kw) * ATTN_SCALE + bias_ref[delta, h * GRID_W:(h + 1) * GRID_W, :]
            m = jnp.max(s, axis=-1, keepdims=True)
            e = jnp.exp(s - m)
            den = jnp.sum(e, axis=-1, keepdims=True)
            out = out + (_dot(_bf(e), vw) * (1.0 / den)) * masks[h]
        o_ref[0, pl.ds(qoff, GRID_W), :] = _bf(out)
        return carry

    lax.fori_loop(0, NA_ROWS_PER_STEP, row_body, 0)


def _na_call(na_qkv, bias_tab):
    B, L, _ = na_qkv.shape
    rows = L // GRID_W
    nblk = rows // NA_ROWS_PER_STEP
    blk = lambda col, shift: pl.BlockSpec(
        (1, NA_TOK, BRANCH_W), lambda b, i: (b, jnp.clip(i + shift, 0, nblk - 1), col))
    return pl.pallas_call(
        functools.partial(_na_kernel, rows=rows),
        out_shape=jax.ShapeDtypeStruct((B, L, BRANCH_W), BF16),
        grid=(B, nblk),
        in_specs=[blk(0, 0), blk(1, -1), blk(1, 0), blk(1, 1), blk(2, -1), blk(2, 0), blk(2, 1),
                  _const_spec((NA_KH, N_HEADS * GRID_W, NA_WIN))],
        out_specs=pl.BlockSpec((1, NA_TOK, BRANCH_W), lambda b, i: (b, i, 0)),
        scratch_shapes=[pltpu.VMEM((3 * NA_TOK, BRANCH_W), BF16), pltpu.VMEM((3 * NA_TOK, BRANCH_W), BF16)],
        compiler_params=_params("parallel", "arbitrary"),
        name="na",
    )(na_qkv, na_qkv, na_qkv, na_qkv, na_qkv, na_qkv, na_qkv, bias_tab)


SG_TS = 512


def _gelu_tanh(x):
    return 0.5 * x * (1.0 + jnp.tanh(np.sqrt(2.0 / np.pi).astype(np.float32) * (x + 0.044715 * (x * x * x))))


def _sg_kernel(uv_ref, lng_ref, lnb_ref, w_ref, b_ref, o_ref):
    lane_group = lax.broadcasted_iota(jnp.int32, (1, BRANCH_W), 1) // (BRANCH_W // SG_GROUPS)
    gmasks = [(lane_group == g).astype(F32) for g in range(SG_GROUPS)]
    w = w_ref[...]
    for n in range(SG_TS // SG_CHUNK):
        rows = slice(n * SG_CHUNK, (n + 1) * SG_CHUNK)
        u = _gelu_tanh(uv_ref[0, rows, 0:BRANCH_W])
        v = _gelu_tanh(uv_ref[0, rows, BRANCH_W:2 * BRANCH_W])
        mu = jnp.mean(v, axis=-1, keepdims=True)
        var = jnp.mean(jnp.square(v - mu), axis=-1, keepdims=True)
        v = (v - mu) * lax.rsqrt(var + EPS) * lng_ref[...] + lnb_ref[...]
        vstack = jnp.concatenate([_bf(v * gmasks[g]) for g in range(SG_GROUPS)], axis=0)
        mixed = _dot(w, vstack) + b_ref[...]
        o_ref[0, rows, :] = _bf(u * mixed)


def _sg_call(sg_uv, ln_g, ln_b, w_cat, b_full):
    B, L, _ = sg_uv.shape
    return pl.pallas_call(
        _sg_kernel,
        out_shape=jax.ShapeDtypeStruct((B, L, BRANCH_W), BF16),
        grid=(B, L // SG_TS),
        in_specs=[pl.BlockSpec((1, SG_TS, 2 * BRANCH_W), lambda b, t: (b, t, 0)),
                  _const_spec((1, BRANCH_W)), _const_spec((1, BRANCH_W)),
                  _const_spec((SG_CHUNK, SG_GROUPS * SG_CHUNK)),
                  _const_spec((SG_CHUNK, BRANCH_W))],
        out_specs=pl.BlockSpec((1, SG_TS, BRANCH_W), lambda b, t: (b, t, 0)),
        compiler_params=_params("parallel", "arbitrary"),
        name="sg",
    )(sg_uv, ln_g, ln_b, w_cat, b_full)


DL_TQ = 128
DL_TQB = 512


def _alibi_slopes(n):
    start = 2.0 ** (-8.0 / n)
    return [float(np.float32(start ** (h + 1))) for h in range(n)]


def _dl_kernel(*refs, dil, n, kw_len, tqb, first, last):
    if first:
        q_ref, k_ref, v_ref = refs[:3]
        outs = refs[3:]
    else:
        q_ref, k_ref, v_ref, accp_ref, mp_ref, lp_ref = refs[:6]
        outs = refs[6:]
    t = pl.program_id(2)
    slopes = _alibi_slopes(N_HEADS)
    masks_bf = _head_masks(BF16)
    masks = _head_masks(F32)

    def tile_body(u, carry):
        base = t * tqb + u * DL_TQ
        ws = pl.multiple_of(jnp.clip(base - DIL_BLK, 0, n - kw_len), DIL_BLK)
        roff = pl.multiple_of(u * DL_TQ, DL_TQ)
        q = q_ref[0, pl.ds(roff, DL_TQ), :]
        kw = k_ref[0, pl.ds(ws, kw_len), :]
        vw = v_ref[0, pl.ds(ws, kw_len), :]
        qi = base + lax.broadcasted_iota(jnp.int32, (DL_TQ, kw_len), 0)
        ki = ws + lax.broadcasted_iota(jnp.int32, (DL_TQ, kw_len), 1)
        step = jnp.abs(ki - qi)
        valid = step <= DIL_BLK
        dist = (step * dil).astype(F32)
        acc = jnp.zeros((DL_TQ, BRANCH_W), F32)
        m_t = jnp.zeros((DL_TQ, BRANCH_W), F32)
        l_t = jnp.zeros((DL_TQ, BRANCH_W), F32)
        for h in range(N_HEADS):
            s = _dot_nt(q * masks_bf[h], kw) * ATTN_SCALE + jnp.where(valid, -slopes[h] * dist, NEG)
            m = jnp.max(s, axis=-1, keepdims=True)
            e = jnp.exp(s - m)
            den = jnp.sum(e, axis=-1, keepdims=True)
            acc = acc + _dot(_bf(e), vw) * masks[h]
            m_t = m_t + m * masks[h]
            l_t = l_t + den * masks[h]
        if not first:
            m_p = mp_ref[0, pl.ds(roff, DL_TQ), :]
            m_n = jnp.maximum(m_p, m_t)
            a_p = jnp.exp(m_p - m_n)
            a_c = jnp.exp(m_t - m_n)
            acc = a_p * accp_ref[0, pl.ds(roff, DL_TQ), :] + a_c * acc
            l_t = a_p * lp_ref[0, pl.ds(roff, DL_TQ), :] + a_c * l_t
            m_t = m_n
        if last:
            outs[0][0, pl.ds(roff, DL_TQ), :] = _bf(acc / l_t)
        else:
            outs[0][0, pl.ds(roff, DL_TQ), :] = acc
            outs[1][0, pl.ds(roff, DL_TQ), :] = m_t
            outs[2][0, pl.ds(roff, DL_TQ), :] = l_t
        return carry

    lax.fori_loop(0, tqb // DL_TQ, tile_body, 0)


def _dl_call(dl_qkv, state, dil, first, last):
    B, L, _ = dl_qkv.shape
    n = L // dil
    kw_len = min(4 * DIL_BLK, n)
    tqb = min(DL_TQB, n)
    qkv_v = dl_qkv.reshape(B, n, dil * 3 * BRANCH_W)
    tok = lambda: pl.BlockSpec((1, tqb, BRANCH_W), lambda b, r, t: (b, t, r))
    in_specs = [pl.BlockSpec((1, tqb, BRANCH_W), lambda b, r, t: (b, t, 3 * r)),
                pl.BlockSpec((1, n, BRANCH_W), lambda b, r, t: (b, 0, 3 * r + 1)),
                pl.BlockSpec((1, n, BRANCH_W), lambda b, r, t: (b, 0, 3 * r + 2))]
    args = [qkv_v, qkv_v, qkv_v]
    if not first:
        in_specs += [tok(), tok(), tok()]
        args += [s.reshape(B, n, dil * BRANCH_W) for s in state]
    if last:
        out_shape = jax.ShapeDtypeStruct((B, n, dil * BRANCH_W), BF16)
        out_specs = tok()
    else:
        out_shape = tuple(jax.ShapeDtypeStruct((B, n, dil * BRANCH_W), F32) for _ in range(3))
        out_specs = (tok(), tok(), tok())
    out = pl.pallas_call(
        functools.partial(_dl_kernel, dil=dil, n=n, kw_len=kw_len, tqb=tqb, first=first, last=last),
        out_shape=out_shape,
        grid=(B, dil, n // tqb),
        in_specs=in_specs,
        out_specs=out_specs,
        compiler_params=_params("parallel", "arbitrary", "arbitrary"),
        name=f"dl{dil}",
    )(*args)
    if last:
        return out.reshape(B, L, BRANCH_W)
    return tuple(o.reshape(B, L, BRANCH_W) for o in out)


def _dilated_attention(dl_qkv):
    state = None
    for gi, (_, dil) in enumerate(DIL_CONFIGS):
        state = _dl_call(dl_qkv, state, dil, gi == 0, gi == len(DIL_CONFIGS) - 1)
    return state


HG_TT = 256
N_SUB = HG_CHUNK // HG_SUB


def _split3(x):
    hi = _bf(x)
    r1 = x - hi.astype(F32)
    mid = _bf(r1)
    lo = _bf(r1 - mid.astype(F32))
    return hi, mid, lo


def _hg_chunk(x, q_pre, v, lb, st_ref, bd_ref, ones_ref, fwd):
    C, S = HG_CHUNK, HG_SUB
    e = jnp.exp(-jnp.abs(x))
    inv = 1.0 / (1.0 + e)
    pos = x >= 0
    sig_p = jnp.where(pos, inv, e * inv)
    sig_n = jnp.where(pos, e * inv, inv)
    logf = jnp.log(lb + (1.0 - lb) * sig_p)
    k = (1.0 - lb) * sig_n
    q = q_pre * _sigmoid(q_pre)

    ti = lax.broadcasted_iota(jnp.int32, (C, C), 0)
    si = lax.broadcasted_iota(jnp.int32, (C, C), 1)
    tri = _bf(jnp.where((si <= ti) if fwd else (si >= ti), 1.0, 0.0))
    cum = sum(_dot(tri, part) for part in _split3(logf))

    edge = cum[C - 1:C, :] if fwd else cum[0:1, :]
    vb = _bf(v)
    masks = _head_masks(F32)

    st = st_ref[...]
    o = _dot_nt(_bf(q * jnp.exp(cum)), _bf(st))

    row = lax.broadcasted_iota(jnp.int32, (C, 1), 0)
    att_blocks = []
    for i in range(N_SUB):
        lo, hi = i * S, (i + 1) * S
        if fwd:
            if i == 0:
                att_blocks.append(None)
                continue
            ref = cum[lo - 1:lo, :]
            seen = row < lo
        else:
            if i == N_SUB - 1:
                att_blocks.append(None)
                continue
            ref = cum[hi:hi + 1, :]
            seen = row >= hi
        q_t = q[lo:hi, :] * jnp.exp(cum[lo:hi, :] - ref)
        k_t = _bf(k * jnp.exp(jnp.where(seen, ref - cum, NEG)))
        q_bd = _bf(jnp.concatenate([q_t * masks[h] for h in range(N_HEADS)], axis=0))
        att = _dot_nt(q_bd, k_t)
        r = _dot(_bf(att), vb)
        att_blocks.append(sum(r[h * S:(h + 1) * S, :] * masks[h] for h in range(N_HEADS)))

    ones_bd = ones_ref[...]
    sub_rows = lax.broadcasted_iota(jnp.int32, (S, 1), 0)
    o_blocks = []
    for i in range(N_SUB):
        lo, hi = i * S, (i + 1) * S
        cq, qq, kk_, vv = cum[lo:hi, :], q[lo:hi, :], k[lo:hi, :], v[lo:hi, :]
        xs = []
        for s in range(S):
            ok = (sub_rows >= s) if fwd else (sub_rows <= s)
            dec = jnp.exp(jnp.where(ok, cq - cq[s:s + 1, :], NEG))
            xs.append(_bf(qq * kk_[s:s + 1, :] * dec))
        z = _dot(jnp.concatenate(xs, axis=0), ones_bd)
        od = z[0:S, :] * vv[0:1, :]
        for s in range(1, S):
            od = od + z[s * S:(s + 1) * S, :] * vv[s:s + 1, :]
        if att_blocks[i] is not None:
            od = od + att_blocks[i]
        o_blocks.append(od)
    o = o + jnp.concatenate(o_blocks, axis=0)

    k_dec = _bf(k * jnp.exp(edge - cum))
    st_ref[...] = st * jnp.exp(edge) + _dot_tn(vb, k_dec) * bd_ref[...]
    return o


def _hg_kernel(*refs, fwd):
    if fwd:
        f_ref, q_ref, i_ref, lb_ref, bd_ref, ones_ref, o_ref, st_ref = refs
    else:
        f_ref, q_ref, i_ref, g_ref, of_ref, lb_ref, gn_ref, bd_ref, ones_ref, o_ref, st_ref = refs

    @pl.when(pl.program_id(1) == 0)
    def _():
        st_ref[...] = jnp.zeros_like(st_ref)

    nchunk = HG_TT // HG_CHUNK

    def chunk_body(ci, carry):
        c = ci if fwd else nchunk - 1 - ci
        rows = pl.ds(pl.multiple_of(c * HG_CHUNK, HG_CHUNK), HG_CHUNK)
        o = _hg_chunk(f_ref[0, rows, :], q_ref[0, rows, :], i_ref[0, rows, :], lb_ref[...],
                      st_ref, bd_ref, ones_ref, fwd)
        if fwd:
            o_ref[0, rows, :] = o
        else:
            o = o + of_ref[0, rows, :]
            sq = o * o
            hi = _bf(sq)
            lo = _bf(sq - hi.astype(F32))
            ms = (_dot(hi, ones_ref[...]) + _dot(lo, ones_ref[...])) * (1.0 / HEAD_DIM)
            g = g_ref[0, rows, :]
            o_ref[0, rows, :] = _bf(o * lax.rsqrt(ms + EPS) * gn_ref[...] * (g * _sigmoid(g)))
        return carry

    lax.fori_loop(0, nchunk, chunk_body, 0)


def _hg_call(hg, o_fwd, lb, gnorm, bd, ones_bd, fwd):
    B, L, _ = hg.shape
    nt = L // HG_TT
    tmap = (lambda t: t) if fwd else (lambda t: nt - 1 - t)
    col = lambda c: pl.BlockSpec((1, HG_TT, BRANCH_W), lambda b, t: (b, tmap(t), c))
    consts = [_const_spec((BRANCH_W, BRANCH_W)), _const_spec((BRANCH_W, BRANCH_W))]
    vec = _const_spec((1, BRANCH_W))
    if fwd:
        in_specs = [col(0), col(2), col(3), vec] + consts
        args = [hg, hg, hg, lb, bd, ones_bd]
        out_dtype = F32
    else:
        in_specs = [col(1), col(2), col(3), col(4), col(0), vec, vec] + consts
        args = [hg, hg, hg, hg, o_fwd, lb, gnorm, bd, ones_bd]
        out_dtype = BF16
    return pl.pallas_call(
        functools.partial(_hg_kernel, fwd=fwd),
        out_shape=jax.ShapeDtypeStruct((B, L, BRANCH_W), out_dtype),
        grid=(B, nt),
        in_specs=in_specs,
        out_specs=col(0),
        scratch_shapes=[pltpu.VMEM((BRANCH_W, BRANCH_W), F32)],
        compiler_params=_params("parallel", "arbitrary"),
        name="hg_fwd" if fwd else "hg_bwd",
    )(*args)


MERGE_TM = 512


def _merge_kernel(x_ref, mod_ref, g_ref, na_ref, hg_ref, sg_ref, dl_ref, wg_ref, bg_ref, wb_ref, wo_ref, o_ref):
    x = x_ref[0]
    h = _bf(_mod_norm(x, g_ref[...], mod_ref[0, 1:2, :], mod_ref[0, 0:1, :]))
    merged = jnp.zeros((MERGE_TM, D_MODEL), F32)
    for bi, br in enumerate((na_ref, hg_ref, sg_ref, dl_ref)):
        gate = _sigmoid(_dot(h, wg_ref[bi]) + bg_ref[bi:bi + 1, :])
        merged = merged + gate * _dot(br[0], wb_ref[bi])
    y = _dot(_bf(merged), wo_ref[...])
    o_ref[0] = x + mod_ref[0, 2:3, :] * y


def _merge_call(x, mod, g, branches, w_gate, b_gate, w_branch, w_o):
    B, L, _ = x.shape
    tm = MERGE_TM
    tok = lambda w: pl.BlockSpec((1, tm, w), lambda b, t: (b, t, 0))
    return pl.pallas_call(
        _merge_kernel,
        out_shape=jax.ShapeDtypeStruct((B, L, D_MODEL), F32),
        grid=(B, L // tm),
        in_specs=[tok(D_MODEL),
                  pl.BlockSpec((1, 6, D_MODEL), lambda b, t: (b, 0, 0)),
                  _const_spec((1, D_MODEL)),
                  tok(BRANCH_W), tok(BRANCH_W), tok(BRANCH_W), tok(BRANCH_W),
                  _const_spec((N_BRANCH, D_MODEL, D_MODEL)),
                  _const_spec((N_BRANCH, D_MODEL)),
                  _const_spec((N_BRANCH, BRANCH_W, D_MODEL)),
                  _const_spec((D_MODEL, D_MODEL))],
        out_specs=tok(D_MODEL),
        compiler_params=_params("parallel", "arbitrary"),
        name="merge",
    )(x, mod, g, *branches, w_gate, b_gate, w_branch, w_o)


FFN_TM = 512
FFN_SPLIT = 2


def _ffn_kernel(x_ref, mod_ref, g_ref, wg_ref, wu_ref, wd_ref, gf_ref, o_ref, *, final):
    x = x_ref[0]
    h = _bf(_mod_norm(x, g_ref[...], mod_ref[0, 4:5, :], mod_ref[0, 3:4, :]))
    y = jnp.zeros((FFN_TM, D_MODEL), F32)
    fc = D_FF // FFN_SPLIT
    for c in range(FFN_SPLIT):
        a = _dot(h, wg_ref[:, c * fc:(c + 1) * fc])
        b = _dot(h, wu_ref[:, c * fc:(c + 1) * fc])
        y = y + _dot(_bf(a * _sigmoid(a) * b), wd_ref[c * fc:(c + 1) * fc, :])
    x = x + mod_ref[0, 5:6, :] * y
    if final:
        x = x * lax.rsqrt(jnp.mean(x * x, axis=-1, keepdims=True) + EPS) * gf_ref[...]
    o_ref[0] = x


def _ffn_call(x, mod, g, wg, wu, wd, g_final, final):
    B, L, _ = x.shape
    tm = FFN_TM
    tok = pl.BlockSpec((1, tm, D_MODEL), lambda b, t: (b, t, 0))
    return pl.pallas_call(
        functools.partial(_ffn_kernel, final=final),
        out_shape=jax.ShapeDtypeStruct((B, L, D_MODEL), F32),
        grid=(B, L // tm),
        in_specs=[tok,
                  pl.BlockSpec((1, 6, D_MODEL), lambda b, t: (b, 0, 0)),
                  _const_spec((1, D_MODEL)),
                  _const_spec((D_MODEL, D_FF)), _const_spec((D_MODEL, D_FF)), _const_spec((D_FF, D_MODEL)),
                  _const_spec((1, D_MODEL))],
        out_specs=tok,
        compiler_params=_params("parallel", "arbitrary"),
        name="ffn",
    )(x, mod, g, wg, wu, wd, g_final)


def _trunk(x, mods, p):
    for l in range(DEPTH):
        mod = mods[l]
        na_qkv, hg, sg_uv, dl_qkv = _proj_call(x, mod, p["g_mix"][l], p["w_in"][l])
        y_na = _na_call(na_qkv, p["na_bias"][l])
        o_f = _hg_call(hg, None, p["lb"][0][l], None, p["bd"], p["ones_bd"], True)
        y_hg = _hg_call(hg, o_f, p["lb"][1][l], p["hg_gnorm"][l], p["bd"], p["ones_bd"], False)
        y_sg = _sg_call(sg_uv, p["sg_ln_g"][l], p["sg_ln_b"][l], p["sg_w"][l], p["sg_b"][l])
        y_dl = _dilated_attention(dl_qkv)
        x = _merge_call(x, mod, p["g_mix"][l], (y_na, y_hg, y_sg, y_dl),
                        p["w_gate"][l], p["b_gate"][l], p["w_branch"][l], p["w_o"][l])
        x = _ffn_call(x, mod, p["g_ffn"][l], p["w_ffn_gate"][l], p["w_ffn_up"][l], p["w_ffn_down"][l],
                      p["g_final"], l == DEPTH - 1)
    return x


def kernel(x_prompt, x_sample, c_prompt, c_sample, w_ada, b_ada, g_norm_mix, g_norm_ffn, w_in, w_gate, b_gate,
           w_branch, w_o, na_rpb, hg_lb, hg_gnorm, sg_ln_g, sg_ln_b, sg_w, sg_b, w_ffn_gate, w_ffn_up,
           w_ffn_down, g_final):
    nbp, nbs = c_prompt.shape[0], c_sample.shape[0]
    nb_pad = -(-(nbp + nbs) // 16) * 16
    c_all = jnp.concatenate([c_prompt, c_sample, jnp.zeros((nb_pad - nbp - nbs, D_MODEL), F32)], axis=0)
    mod_all = _ada_call(c_all, w_ada, b_ada).reshape(DEPTH, nb_pad, 6, D_MODEL)
    mods_p = [mod_all[l, :nbp] for l in range(DEPTH)]
    mods_s = [mod_all[l, nbp:nbp + nbs] for l in range(DEPTH)]

    lb_soft = jax.nn.softmax(hg_lb.astype(F32), axis=1)
    lower = jnp.cumsum(lb_soft, axis=1) - lb_soft[:, :1]
    head = np.arange(BRANCH_W) // HEAD_DIM
    bd = jnp.asarray((head[:, None] == head[None, :]).astype(np.float32))
    vec = lambda a: [a[l].reshape(1, -1).astype(F32) for l in range(DEPTH)]
    p = {
        "g_mix": vec(g_norm_mix), "g_ffn": vec(g_norm_ffn),
        "w_in": [_bf(w_in[l]) for l in range(DEPTH)],
        "w_gate": [_bf(w_gate[l]) for l in range(DEPTH)],
        "b_gate": [b_gate[l] for l in range(DEPTH)],
        "w_branch": [_bf(w_branch[l]) for l in range(DEPTH)],
        "w_o": [_bf(w_o[l]) for l in range(DEPTH)],
        "na_bias": [_na_bias_table(na_rpb[l]) for l in range(DEPTH)],
        "lb": [[lower[d, l].reshape(1, BRANCH_W) for l in range(DEPTH)] for d in range(2)],
        "hg_gnorm": vec(hg_gnorm), "sg_ln_g": vec(sg_ln_g), "sg_ln_b": vec(sg_ln_b),
        "sg_w": [_bf(sg_w[l].transpose(1, 0, 2).reshape(SG_CHUNK, SG_GROUPS * SG_CHUNK)) for l in range(DEPTH)],
        "sg_b": [jnp.repeat(sg_b[l].T.astype(F32), BRANCH_W // SG_GROUPS, axis=1) for l in range(DEPTH)],
        "w_ffn_gate": [_bf(w_ffn_gate[l]) for l in range(DEPTH)],
        "w_ffn_up": [_bf(w_ffn_up[l]) for l in range(DEPTH)],
        "w_ffn_down": [_bf(w_ffn_down[l]) for l in range(DEPTH)],
        "g_final": g_final.reshape(1, D_MODEL).astype(F32),
        "bd": bd, "ones_bd": _bf(bd),
    }
    return (_trunk(x_prompt, mods_p, p), _trunk(x_sample, mods_s, p))
```

```python
import functools

import jax
import jax.numpy as jnp
import numpy as np
from jax import lax
from jax.experimental import pallas as pl
from jax.experimental.pallas import tpu as pltpu

D_MODEL = 1024
DEPTH = 4
GRID_W = 64
BRANCH_W = 256
N_BRANCH = 4
HEAD_DIM = 64
N_HEADS = BRANCH_W // HEAD_DIM
ATTN_SCALE = HEAD_DIM ** -0.5
NA_KH = 8
NA_KW = 16
HG_CHUNK = 64
HG_SUB = 16
SG_GROUPS = 4
SG_CHUNK = 128
DIL_CONFIGS = ((128, 1), (512, 4), (2048, 16))
DIL_BLK = 64
D_FF = 2816
IN_WIDTH = 3328
NEG = -1e30
EPS = 1e-6

NA_ROWS_PER_STEP = 8
NA_TOK = NA_ROWS_PER_STEP * GRID_W
NA_WIN = NA_KH * GRID_W

VMEM_LIMIT = 56 * 1024 * 1024

BF16 = jnp.bfloat16
F32 = jnp.float32


def _bf(x):
    return x.astype(BF16)


def _dot(a, b):
    return jnp.dot(a, b, preferred_element_type=F32)


def _dot_nt(a, b):
    return lax.dot_general(a, b, (((1,), (1,)), ((), ())), preferred_element_type=F32)


def _dot_tn(a, b):
    return lax.dot_general(a, b, (((0,), (0,)), ((), ())), preferred_element_type=F32)


def _sigmoid(x):
    return 1.0 / (1.0 + jnp.exp(-x))


def _head_masks(dtype):
    lane_head = lax.broadcasted_iota(jnp.int32, (1, BRANCH_W), 1) // HEAD_DIM
    return [(lane_head == h).astype(dtype) for h in range(N_HEADS)]


def _pick_heads(blocks):
    lane_head = lax.broadcasted_iota(jnp.int32, (1, BRANCH_W), 1) // HEAD_DIM
    out = blocks[N_HEADS - 1]
    for h in range(N_HEADS - 2, -1, -1):
        out = jnp.where(lane_head <= h, blocks[h], out)
    return out


def _mod_norm(x, g, scale, shift):
    y = x * lax.rsqrt(jnp.mean(x * x, axis=-1, keepdims=True) + EPS)
    return (y * g) * (1.0 + scale) + shift


def _params(*sem):
    return pltpu.CompilerParams(dimension_semantics=sem, vmem_limit_bytes=VMEM_LIMIT)


def _const_spec(shape):
    nd = len(shape)
    return pl.BlockSpec(shape, lambda *_: (0,) * nd, pipeline_mode=pl.Buffered(1))


def _ada_kernel(c_ref, w_ref, b_ref, o_ref):
    c = c_ref[...]
    cs = c * _sigmoid(c)
    o_ref[0] = _dot(_bf(cs), _bf(w_ref[0])) + b_ref[0]


def _ada_call(c_all, w_ada, b_ada):
    nb = c_all.shape[0]
    tn = 1536
    return pl.pallas_call(
        _ada_kernel,
        out_shape=jax.ShapeDtypeStruct((DEPTH, nb, 6 * D_MODEL), F32),
        grid=(DEPTH, 6 * D_MODEL // tn),
        in_specs=[pl.BlockSpec((nb, D_MODEL), lambda l, n: (0, 0)),
                  pl.BlockSpec((1, D_MODEL, tn), lambda l, n: (l, 0, n)),
                  pl.BlockSpec((1, 1, tn), lambda l, n: (l, 0, n))],
        out_specs=pl.BlockSpec((1, nb, tn), lambda l, n: (l, 0, n)),
        compiler_params=_params("arbitrary", "arbitrary"),
        name="ada",
    )(c_all, w_ada, b_ada.reshape(DEPTH, 1, 6 * D_MODEL))


PROJ_TM = 512


def _proj_kernel(x_ref, mod_ref, g_ref, w_ref, na_ref, hg_ref, sg_ref, dl_ref):
    h = _bf(_mod_norm(x_ref[0], g_ref[...], mod_ref[0, 1:2, :], mod_ref[0, 0:1, :]))
    na_ref[0] = _bf(_dot(h, w_ref[:, 0:768]))
    hg_ref[0] = _dot(h, w_ref[:, 768:2048])
    sg_ref[0] = _dot(h, w_ref[:, 2048:2560])
    dl_ref[0] = _bf(_dot(h, w_ref[:, 2560:3328]))


def _proj_call(x, mod, g, w_in):
    B, L, _ = x.shape
    tm = PROJ_TM
    tok = lambda w: pl.BlockSpec((1, tm, w), lambda b, t: (b, t, 0))
    return pl.pallas_call(
        _proj_kernel,
        out_shape=(jax.ShapeDtypeStruct((B, L, 768), BF16),
                   jax.ShapeDtypeStruct((B, L, 1280), F32),
                   jax.ShapeDtypeStruct((B, L, 512), F32),
                   jax.ShapeDtypeStruct((B, L, 768), BF16)),
        grid=(B, L // tm),
        in_specs=[tok(D_MODEL),
                  pl.BlockSpec((1, 6, D_MODEL), lambda b, t: (b, 0, 0)),
                  _const_spec((1, D_MODEL)),
                  _const_spec((D_MODEL, IN_WIDTH))],
        out_specs=(tok(768), tok(1280), tok(512), tok(768)),
        compiler_params=_params("parallel", "arbitrary"),
        name="proj",
    )(x, mod, g, w_in)


def _na_bias_table(rpb):
    c = np.arange(GRID_W)
    kc = np.arange(GRID_W)
    col_off = np.clip(kc[None, :] - c[:, None], -(NA_KW - 1), NA_KW - 1) + NA_KW - 1
    win0 = np.clip(c - NA_KW // 2, 0, GRID_W - NA_KW)
    valid = (kc[None, :] >= win0[:, None]) & (kc[None, :] < win0[:, None] + NA_KW)
    onehot = (col_off[None] == np.arange(2 * NA_KW - 1)[:, None, None]).astype(np.float32)
    cols = jnp.einsum("hrj,jck->hrck", rpb.astype(F32), jnp.asarray(onehot), precision=lax.Precision.HIGHEST)
    cols = jnp.where(valid[None, None], cols, NEG)
    pats = jnp.stack([cols[:, NA_KH - 1 - d:2 * NA_KH - 1 - d] for d in range(NA_KH)])
    return pats.transpose(0, 1, 3, 2, 4).reshape(NA_KH, N_HEADS * GRID_W, NA_WIN)


def _na_kernel(q_ref, kp_ref, kc_ref, kn_ref, vp_ref, vc_ref, vn_ref, bias_ref, o_ref, kbuf, vbuf, *, rows):
    i = pl.program_id(1)
    for n, (kr, vr) in enumerate(((kp_ref, vp_ref), (kc_ref, vc_ref), (kn_ref, vn_ref))):
        kbuf[n * NA_TOK:(n + 1) * NA_TOK, :] = kr[0]
        vbuf[n * NA_TOK:(n + 1) * NA_TOK, :] = vr[0]
    qmasks = [m * ATTN_SCALE for m in _head_masks(BF16)]

    def row_body(j, carry):
        r = i * NA_ROWS_PER_STEP + j
        start = jnp.clip(r - NA_KH // 2, 0, rows - NA_KH)
        off = pl.multiple_of((start - (i - 1) * NA_ROWS_PER_STEP) * GRID_W, GRID_W)
        delta = r - start
        qoff = pl.multiple_of(j * GRID_W, GRID_W)
        q = q_ref[0, pl.ds(qoff, GRID_W), :]
        kw = kbuf[pl.ds(off, NA_WIN), :]
        vw = vbuf[pl.ds(off, NA_WIN), :]
        q_heads = jnp.concatenate([q * qmasks[h] for h in range(N_HEADS)], axis=0)
        s = _dot_nt(q_heads, kw) + bias_ref[delta]
        m = jnp.max(s, axis=-1, keepdims=True)
        e = jnp.exp(s - m)
        den = jnp.sum(e, axis=-1, keepdims=True)
        r_all = _dot(_bf(e), vw) * (1.0 / den)
        out = _pick_heads([r_all[h * GRID_W:(h + 1) * GRID_W, :] for h in range(N_HEADS)])
        o_ref[0, pl.ds(qoff, GRID_W), :] = _bf(out)
        return carry

    lax.fori_loop(0, NA_ROWS_PER_STEP, row_body, 0, unroll=2)


def _na_call(na_qkv, bias_tab):
    B, L, _ = na_qkv.shape
    rows = L // GRID_W
    nblk = rows // NA_ROWS_PER_STEP
    blk = lambda col, shift: pl.BlockSpec(
        (1, NA_TOK, BRANCH_W), lambda b, i: (b, jnp.clip(i + shift, 0, nblk - 1), col))
    return pl.pallas_call(
        functools.partial(_na_kernel, rows=rows),
        out_shape=jax.ShapeDtypeStruct((B, L, BRANCH_W), BF16),
        grid=(B, nblk),
        in_specs=[blk(0, 0), blk(1, -1), blk(1, 0), blk(1, 1), blk(2, -1), blk(2, 0), blk(2, 1),
                  _const_spec((NA_KH, N_HEADS * GRID_W, NA_WIN))],
        out_specs=pl.BlockSpec((1, NA_TOK, BRANCH_W), lambda b, i: (b, i, 0)),
        scratch_shapes=[pltpu.VMEM((3 * NA_TOK, BRANCH_W), BF16), pltpu.VMEM((3 * NA_TOK, BRANCH_W), BF16)],
        compiler_params=_params("parallel", "arbitrary"),
        name="na",
    )(na_qkv, na_qkv, na_qkv, na_qkv, na_qkv, na_qkv, na_qkv, bias_tab)


SG_TS = 512


def _gelu_tanh(x):
    return 0.5 * x * (1.0 + jnp.tanh(np.sqrt(2.0 / np.pi).astype(np.float32) * (x + 0.044715 * (x * x * x))))


def _sg_kernel(uv_ref, lng_ref, lnb_ref, w_ref, b_ref, o_ref):
    lane_group = lax.broadcasted_iota(jnp.int32, (1, BRANCH_W), 1) // (BRANCH_W // SG_GROUPS)
    gmasks = [(lane_group == g).astype(F32) for g in range(SG_GROUPS)]
    w = w_ref[...]
    for n in range(SG_TS // SG_CHUNK):
        rows = slice(n * SG_CHUNK, (n + 1) * SG_CHUNK)
        u = _gelu_tanh(uv_ref[0, rows, 0:BRANCH_W])
        v = _gelu_tanh(uv_ref[0, rows, BRANCH_W:2 * BRANCH_W])
        mu = jnp.mean(v, axis=-1, keepdims=True)
        var = jnp.mean(jnp.square(v - mu), axis=-1, keepdims=True)
        v = (v - mu) * lax.rsqrt(var + EPS) * lng_ref[...] + lnb_ref[...]
        vstack = jnp.concatenate([_bf(v * gmasks[g]) for g in range(SG_GROUPS)], axis=0)
        mixed = _dot(w, vstack) + b_ref[...]
        o_ref[0, rows, :] = _bf(u * mixed)


def _sg_call(sg_uv, ln_g, ln_b, w_cat, b_full):
    B, L, _ = sg_uv.shape
    return pl.pallas_call(
        _sg_kernel,
        out_shape=jax.ShapeDtypeStruct((B, L, BRANCH_W), BF16),
        grid=(B, L // SG_TS),
        in_specs=[pl.BlockSpec((1, SG_TS, 2 * BRANCH_W), lambda b, t: (b, t, 0)),
                  _const_spec((1, BRANCH_W)), _const_spec((1, BRANCH_W)),
                  _const_spec((SG_CHUNK, SG_GROUPS * SG_CHUNK)),
                  _const_spec((SG_CHUNK, BRANCH_W))],
        out_specs=pl.BlockSpec((1, SG_TS, BRANCH_W), lambda b, t: (b, t, 0)),
        compiler_params=_params("parallel", "arbitrary"),
        name="sg",
    )(sg_uv, ln_g, ln_b, w_cat, b_full)


DL_TQ = 128
DL_TQB = 512


def _alibi_slopes(n):
    start = 2.0 ** (-8.0 / n)
    return [float(np.float32(start ** (h + 1))) for h in range(n)]


def _dl_kernel(*refs, dil, n, kw_len, tqb, first, last):
    if first:
        q_ref, k_ref, v_ref, bias_ref = refs[:4]
        outs = refs[4:]
    else:
        q_ref, k_ref, v_ref, bias_ref, op_ref, lsep_ref = refs[:6]
        outs = refs[6:]
    t = pl.program_id(2)
    qmasks = [m * ATTN_SCALE for m in _head_masks(BF16)]
    lead = (0,) * (len(q_ref.shape) - 2)

    def tile_body(u, carry):
        base = t * tqb + u * DL_TQ
        ws = pl.multiple_of(jnp.clip(base - DIL_BLK, 0, n - kw_len), DIL_BLK)
        rows = pl.ds(pl.multiple_of(u * DL_TQ, DL_TQ), DL_TQ)
        q = q_ref[lead + (rows, slice(None))]
        kw = k_ref[lead + (pl.ds(ws, kw_len), slice(None))]
        vw = v_ref[lead + (pl.ds(ws, kw_len), slice(None))]
        q_heads = jnp.concatenate([q * qmasks[h] for h in range(N_HEADS)], axis=0)
        s = _dot_nt(q_heads, kw) + bias_ref[(base - ws) // DIL_BLK]
        m = jnp.max(s, axis=-1, keepdims=True)
        e = jnp.exp(s - m)
        den = jnp.sum(e, axis=-1, keepdims=True)
        r_all = _dot(_bf(e), vw) * (1.0 / den)
        lse_col = m + jnp.log(den)
        o = _pick_heads([r_all[h * DL_TQ:(h + 1) * DL_TQ, :] for h in range(N_HEADS)])
        lse = _pick_heads([lse_col[h * DL_TQ:(h + 1) * DL_TQ, :] for h in range(N_HEADS)])
        if not first:
            lse_p = lsep_ref[lead + (rows, slice(None))]
            lse_m = jnp.maximum(lse_p, lse)
            w_p = jnp.exp(lse_p - lse_m)
            w_c = jnp.exp(lse - lse_m)
            tot = w_p + w_c
            o = (w_p * op_ref[lead + (rows, slice(None))] + w_c * o) / tot
            lse = lse_m + jnp.log(tot)
        if last:
            outs[0][lead + (rows, slice(None))] = _bf(o)
        else:
            outs[0][lead + (rows, slice(None))] = o
            outs[1][lead + (rows, slice(None))] = lse
        return carry

    lax.fori_loop(0, tqb // DL_TQ, tile_body, 0, unroll=2)


def _dl_bias_table(dil, kw_len):
    slopes = np.asarray(_alibi_slopes(N_HEADS), np.float32)
    qi = np.arange(DL_TQ)[None, :, None]
    ki = np.arange(kw_len)[None, None, :] - DIL_BLK * np.arange(3)[:, None, None]
    step = np.abs(ki - qi)
    dist = (step * dil).astype(np.float32)
    bias = np.where((step <= DIL_BLK)[:, None], -slopes[None, :, None, None] * dist[:, None], np.float32(NEG))
    return jnp.asarray(bias.reshape(3, N_HEADS * DL_TQ, kw_len), F32)


def _dl_call(qkv, state, dil, first, last):
    B = qkv.shape[0]
    n = qkv.shape[-2]
    kw_len = min(4 * DIL_BLK, n)
    tqb = min(DL_TQB, n)
    if dil == 1:
        tok = lambda c: pl.BlockSpec((1, tqb, BRANCH_W), lambda b, r, t: (b, t, c))
        full = lambda c: pl.BlockSpec((1, n, BRANCH_W), lambda b, r, t: (b, 0, c))
        shape = (B, n, BRANCH_W)
    else:
        tok = lambda c: pl.BlockSpec((1, 1, tqb, BRANCH_W), lambda b, r, t: (b, r, t, c))
        full = lambda c: pl.BlockSpec((1, 1, n, BRANCH_W), lambda b, r, t: (b, r, 0, c))
        shape = (B, dil, n, BRANCH_W)
    in_specs = [tok(0), full(1), full(2), _const_spec((3, N_HEADS * DL_TQ, kw_len))]
    args = [qkv, qkv, qkv, _dl_bias_table(dil, kw_len)]
    if not first:
        in_specs += [tok(0), tok(0)]
        args += list(state)
    if last:
        out_shape = jax.ShapeDtypeStruct(shape, BF16)
        out_specs = tok(0)
    else:
        out_shape = (jax.ShapeDtypeStruct(shape, F32), jax.ShapeDtypeStruct(shape, F32))
        out_specs = (tok(0), tok(0))
    return pl.pallas_call(
        functools.partial(_dl_kernel, dil=dil, n=n, kw_len=kw_len, tqb=tqb, first=first, last=last),
        out_shape=out_shape,
        grid=(B, dil, n // tqb),
        in_specs=in_specs,
        out_specs=out_specs,
        compiler_params=_params("parallel", "arbitrary", "arbitrary"),
        name=f"dl{dil}",
    )(*args)


def _to_classes(t, dil):
    B, L, C = t.shape
    return t.reshape(B, L // dil, dil, C).transpose(0, 2, 1, 3)


def _dilated_attention(dl_qkv):
    B, L, _ = dl_qkv.shape
    (_, d1), (_, d4), (_, d16) = DIL_CONFIGS
    sub = d16 // d4
    state = _dl_call(dl_qkv, None, d1, True, False)
    state = [_to_classes(s, d4) for s in state]
    state = _dl_call(_to_classes(dl_qkv, d4), state, d4, False, False)
    state = [s.reshape(B, d4, L // d16, sub, BRANCH_W).transpose(0, 3, 1, 2, 4).reshape(B, d16, L // d16, BRANCH_W)
             for s in state]
    y = _dl_call(_to_classes(dl_qkv, d16), state, d16, False, True)
    return y.transpose(0, 2, 1, 3).reshape(B, L, BRANCH_W)


HG_TT = 512
LOG2E = float(np.log2(np.e))
N_SUB = HG_CHUNK // HG_SUB


def _split3(x):
    hi = _bf(x)
    r1 = x - hi.astype(F32)
    mid = _bf(r1)
    lo = _bf(r1 - mid.astype(F32))
    return hi, mid, lo


def _hg_chunk(x, q_pre, v, lb, st_ref, tri_ref, bd_ref, ones_ref, fwd):
    C, S = HG_CHUNK, HG_SUB
    e = jnp.exp(-jnp.abs(x))
    inv = 1.0 / (1.0 + e)
    pos = x >= 0
    sig_p = jnp.where(pos, inv, e * inv)
    sig_n = jnp.where(pos, e * inv, inv)
    logf = jnp.log(lb + (1.0 - lb) * sig_p)
    k = (1.0 - lb) * sig_n
    q = q_pre * _sigmoid(q_pre)

    tri = tri_ref[...]
    cum = sum(_dot(tri, part) for part in _split3(logf))

    edge = cum[C - 1:C, :] if fwd else cum[0:1, :]
    vb = _bf(v)
    masks = _head_masks(F32)

    st = st_ref[...]
    o = _dot_nt(_bf(q * jnp.exp(cum)), _bf(st))

    row = lax.broadcasted_iota(jnp.int32, (C, 1), 0)
    att_blocks = []
    for i in range(N_SUB):
        lo, hi = i * S, (i + 1) * S
        if fwd:
            if i == 0:
                att_blocks.append(None)
                continue
            ref = cum[lo - 1:lo, :]
            seen = row < lo
        else:
            if i == N_SUB - 1:
                att_blocks.append(None)
                continue
            ref = cum[hi:hi + 1, :]
            seen = row >= hi
        q_t = q[lo:hi, :] * jnp.exp(cum[lo:hi, :] - ref)
        k_t = _bf(k * jnp.exp(jnp.where(seen, ref - cum, NEG)))
        q_bd = _bf(jnp.concatenate([q_t * masks[h] for h in range(N_HEADS)], axis=0))
        att = _dot_nt(q_bd, k_t)
        r = _dot(_bf(att), vb)
        att_blocks.append(_pick_heads([r[h * S:(h + 1) * S, :] for h in range(N_HEADS)]))

    ones_bd = ones_ref[...]
    half = S // 2
    half_rows = lax.broadcasted_iota(jnp.int32, (half, 1), 0)
    cum2 = cum * LOG2E
    o_blocks = []
    for i in range(N_SUB):
        lo, hi = i * S, (i + 1) * S
        cq, qq, kk_, vv = cum2[lo:hi, :], q[lo:hi, :], k[lo:hi, :], v[lo:hi, :]
        xs = []
        for s in range(S):
            parts = []
            for r0 in (0, half):
                if fwd:
                    everything, nothing = s <= r0, s > r0 + half - 1
                else:
                    everything, nothing = s >= r0 + half - 1, s < r0
                if nothing:
                    parts.append(jnp.zeros((half, BRANCH_W), F32))
                    continue
                d = cq[r0:r0 + half, :] - cq[s:s + 1, :]
                if not everything:
                    ok = (half_rows >= s - r0) if fwd else (half_rows <= s - r0)
                    d = jnp.where(ok, d, NEG)
                parts.append(qq[r0:r0 + half, :] * kk_[s:s + 1, :] * jnp.exp2(d))
            xs.append(_bf(jnp.concatenate(parts, axis=0)))
        z = _dot(jnp.concatenate(xs, axis=0), ones_bd)
        od = z[0:S, :] * vv[0:1, :]
        for s in range(1, S):
            od = od + z[s * S:(s + 1) * S, :] * vv[s:s + 1, :]
        if att_blocks[i] is not None:
            od = od + att_blocks[i]
        o_blocks.append(od)
    o = o + jnp.concatenate(o_blocks, axis=0)

    k_dec = _bf(k * jnp.exp(edge - cum))
    st_ref[...] = st * jnp.exp(edge) + _dot_tn(vb, k_dec) * bd_ref[...]
    return o


def _hg_kernel(*refs, fwd):
    if fwd:
        f_ref, q_ref, i_ref, lb_ref, tri_ref, bd_ref, ones_ref, o_ref, st_ref = refs
    else:
        f_ref, q_ref, i_ref, g_ref, of_ref, lb_ref, gn_ref, tri_ref, bd_ref, ones_ref, o_ref, st_ref = refs

    @pl.when(pl.program_id(1) == 0)
    def _():
        st_ref[...] = jnp.zeros_like(st_ref)

    nchunk = HG_TT // HG_CHUNK

    def chunk_body(ci, carry):
        c = ci if fwd else nchunk - 1 - ci
        rows = pl.ds(pl.multiple_of(c * HG_CHUNK, HG_CHUNK), HG_CHUNK)
        o = _hg_chunk(f_ref[0, rows, :], q_ref[0, rows, :], i_ref[0, rows, :], lb_ref[...],
                      st_ref, tri_ref, bd_ref, ones_ref, fwd)
        if fwd:
            o_ref[0, rows, :] = o
        else:
            o = o + of_ref[0, rows, :]
            sq = o * o
            hi = _bf(sq)
            lo = _bf(sq - hi.astype(F32))
            ms = (_dot(hi, ones_ref[...]) + _dot(lo, ones_ref[...])) * (1.0 / HEAD_DIM)
            g = g_ref[0, rows, :]
            o_ref[0, rows, :] = _bf(o * lax.rsqrt(ms + EPS) * gn_ref[...] * (g * _sigmoid(g)))
        return carry

    lax.fori_loop(0, nchunk, chunk_body, 0, unroll=2)


def _hg_call(hg, o_fwd, lb, gnorm, bd, ones_bd, fwd):
    B, L, _ = hg.shape
    nt = L // HG_TT
    tmap = (lambda t: t) if fwd else (lambda t: nt - 1 - t)
    col = lambda c: pl.BlockSpec((1, HG_TT, BRANCH_W), lambda b, t: (b, tmap(t), c))
    idx = np.arange(HG_CHUNK)
    tri = jnp.asarray((idx[None, :] <= idx[:, None]) if fwd else (idx[None, :] >= idx[:, None]), BF16)
    consts = [_const_spec((HG_CHUNK, HG_CHUNK)), _const_spec((BRANCH_W, BRANCH_W)), _const_spec((BRANCH_W, BRANCH_W))]
    vec = _const_spec((1, BRANCH_W))
    if fwd:
        in_specs = [col(0), col(2), col(3), vec] + consts
        args = [hg, hg, hg, lb, tri, bd, ones_bd]
        out_dtype = F32
    else:
        in_specs = [col(1), col(2), col(3), col(4), col(0), vec, vec] + consts
        args = [hg, hg, hg, hg, o_fwd, lb, gnorm, tri, bd, ones_bd]
        out_dtype = BF16
    return pl.pallas_call(
        functools.partial(_hg_kernel, fwd=fwd),
        out_shape=jax.ShapeDtypeStruct((B, L, BRANCH_W), out_dtype),
        grid=(B, nt),
        in_specs=in_specs,
        out_specs=col(0),
        scratch_shapes=[pltpu.VMEM((BRANCH_W, BRANCH_W), F32)],
        compiler_params=_params("parallel", "arbitrary"),
        name="hg_fwd" if fwd else "hg_bwd",
    )(*args)


MERGE_TM = 512


def _merge_kernel(x_ref, mod_ref, g_ref, na_ref, hg_ref, sg_ref, dl_ref, wg_ref, bg_ref, wb_ref, wo_ref, o_ref):
    x = x_ref[0]
    h = _bf(_mod_norm(x, g_ref[...], mod_ref[0, 1:2, :], mod_ref[0, 0:1, :]))
    merged = jnp.zeros((MERGE_TM, D_MODEL), F32)
    for bi, br in enumerate((na_ref, hg_ref, sg_ref, dl_ref)):
        gate = _sigmoid(_dot(h, wg_ref[bi]) + bg_ref[bi:bi + 1, :])
        merged = merged + gate * _dot(br[0], wb_ref[bi])
    y = _dot(_bf(merged), wo_ref[...])
    o_ref[0] = x + mod_ref[0, 2:3, :] * y


def _merge_call(x, mod, g, branches, w_gate, b_gate, w_branch, w_o):
    B, L, _ = x.shape
    tm = MERGE_TM
    tok = lambda w: pl.BlockSpec((1, tm, w), lambda b, t: (b, t, 0))
    return pl.pallas_call(
        _merge_kernel,
        out_shape=jax.ShapeDtypeStruct((B, L, D_MODEL), F32),
        grid=(B, L // tm),
        in_specs=[tok(D_MODEL),
                  pl.BlockSpec((1, 6, D_MODEL), lambda b, t: (b, 0, 0)),
                  _const_spec((1, D_MODEL)),
                  tok(BRANCH_W), tok(BRANCH_W), tok(BRANCH_W), tok(BRANCH_W),
                  _const_spec((N_BRANCH, D_MODEL, D_MODEL)),
                  _const_spec((N_BRANCH, D_MODEL)),
                  _const_spec((N_BRANCH, BRANCH_W, D_MODEL)),
                  _const_spec((D_MODEL, D_MODEL))],
        out_specs=tok(D_MODEL),
        compiler_params=_params("parallel", "arbitrary"),
        name="merge",
    )(x, mod, g, *branches, w_gate, b_gate, w_branch, w_o)


FFN_TM = 512
FFN_SPLIT = 2


def _ffn_kernel(x_ref, mod_ref, g_ref, wg_ref, wu_ref, wd_ref, gf_ref, o_ref, *, final):
    x = x_ref[0]
    h = _bf(_mod_norm(x, g_ref[...], mod_ref[0, 4:5, :], mod_ref[0, 3:4, :]))
    y = jnp.zeros((FFN_TM, D_MODEL), F32)
    fc = D_FF // FFN_SPLIT
    for c in range(FFN_SPLIT):
        a = _dot(h, wg_ref[:, c * fc:(c + 1) * fc])
        b = _dot(h, wu_ref[:, c * fc:(c + 1) * fc])
        y = y + _dot(_bf(a * _sigmoid(a) * b), wd_ref[c * fc:(c + 1) * fc, :])
    x = x + mod_ref[0, 5:6, :] * y
    if final:
        x = x * lax.rsqrt(jnp.mean(x * x, axis=-1, keepdims=True) + EPS) * gf_ref[...]
    o_ref[0] = x


def _ffn_call(x, mod, g, wg, wu, wd, g_final, final):
    B, L, _ = x.shape
    tm = FFN_TM
    tok = pl.BlockSpec((1, tm, D_MODEL), lambda b, t: (b, t, 0))
    return pl.pallas_call(
        functools.partial(_ffn_kernel, final=final),
        out_shape=jax.ShapeDtypeStruct((B, L, D_MODEL), F32),
        grid=(B, L // tm),
        in_specs=[tok,
                  pl.BlockSpec((1, 6, D_MODEL), lambda b, t: (b, 0, 0)),
                  _const_spec((1, D_MODEL)),
                  _const_spec((D_MODEL, D_FF)), _const_spec((D_MODEL, D_FF)), _const_spec((D_FF, D_MODEL)),
                  _const_spec((1, D_MODEL))],
        out_specs=tok,
        compiler_params=_params("parallel", "arbitrary"),
        name="ffn",
    )(x, mod, g, wg, wu, wd, g_final)


def _trunk(x, mods, p):
    for l in range(DEPTH):
        mod = mods[l]
        na_qkv, hg, sg_uv, dl_qkv = _proj_call(x, mod, p["g_mix"][l], p["w_in"][l])
        y_na = _na_call(na_qkv, p["na_bias"][l])
        o_f = _hg_call(hg, None, p["lb"][0][l], None, p["bd"], p["ones_bd"], True)
        y_hg = _hg_call(hg, o_f, p["lb"][1][l], p["hg_gnorm"][l], p["bd"], p["ones_bd"], False)
        y_sg = _sg_call(sg_uv, p["sg_ln_g"][l], p["sg_ln_b"][l], p["sg_w"][l], p["sg_b"][l])
        y_dl = _dilated_attention(dl_qkv)
        x = _merge_call(x, mod, p["g_mix"][l], (y_na, y_hg, y_sg, y_dl),
                        p["w_gate"][l], p["b_gate"][l], p["w_branch"][l], p["w_o"][l])
        x = _ffn_call(x, mod, p["g_ffn"][l], p["w_ffn_gate"][l], p["w_ffn_up"][l], p["w_ffn_down"][l],
                      p["g_final"], l == DEPTH - 1)
    return x


def kernel(x_prompt, x_sample, c_prompt, c_sample, w_ada, b_ada, g_norm_mix, g_norm_ffn, w_in, w_gate, b_gate,
           w_branch, w_o, na_rpb, hg_lb, hg_gnorm, sg_ln_g, sg_ln_b, sg_w, sg_b, w_ffn_gate, w_ffn_up,
           w_ffn_down, g_final):
    nbp, nbs = c_prompt.shape[0], c_sample.shape[0]
    nb_pad = -(-(nbp + nbs) // 16) * 16
    c_all = jnp.concatenate([c_prompt, c_sample, jnp.zeros((nb_pad - nbp - nbs, D_MODEL), F32)], axis=0)
    mod_all = _ada_call(c_all, w_ada, b_ada).reshape(DEPTH, nb_pad, 6, D_MODEL)
    mods_p = [mod_all[l, :nbp] for l in range(DEPTH)]
    mods_s = [mod_all[l, nbp:nbp + nbs] for l in range(DEPTH)]

    lb_soft = jax.nn.softmax(hg_lb.astype(F32), axis=1)
    lower = jnp.cumsum(lb_soft, axis=1) - lb_soft[:, :1]
    head = np.arange(BRANCH_W) // HEAD_DIM
    bd = jnp.asarray((head[:, None] == head[None, :]).astype(np.float32))
    vec = lambda a: [a[l].reshape(1, -1).astype(F32) for l in range(DEPTH)]
    p = {
        "g_mix": vec(g_norm_mix), "g_ffn": vec(g_norm_ffn),
        "w_in": [_bf(w_in[l]) for l in range(DEPTH)],
        "w_gate": [_bf(w_gate[l]) for l in range(DEPTH)],
        "b_gate": [b_gate[l] for l in range(DEPTH)],
        "w_branch": [_bf(w_branch[l]) for l in range(DEPTH)],
        "w_o": [_bf(w_o[l]) for l in range(DEPTH)],
        "na_bias": [_na_bias_table(na_rpb[l]) for l in range(DEPTH)],
        "lb": [[lower[d, l].reshape(1, BRANCH_W) for l in range(DEPTH)] for d in range(2)],
        "hg_gnorm": vec(hg_gnorm), "sg_ln_g": vec(sg_ln_g), "sg_ln_b": vec(sg_ln_b),
        "sg_w": [_bf(sg_w[l].transpose(1, 0, 2).reshape(SG_CHUNK, SG_GROUPS * SG_CHUNK)) for l in range(DEPTH)],
        "sg_b": [jnp.repeat(sg_b[l].T.astype(F32), BRANCH_W // SG_GROUPS, axis=1) for l in range(DEPTH)],
        "w_ffn_gate": [_bf(w_ffn_gate[l]) for l in range(DEPTH)],
        "w_ffn_up": [_bf(w_ffn_up[l]) for l in range(DEPTH)],
        "w_ffn_down": [_bf(w_ffn_down[l]) for l in range(DEPTH)],
        "g_final": g_final.reshape(1, D_MODEL).astype(F32),
        "bd": bd, "ones_bd": _bf(bd),
    }
    return (_trunk(x_prompt, mods_p, p), _trunk(x_sample, mods_s, p))
```

```python
import functools

import jax
import jax.numpy as jnp
import numpy as np
from jax import lax
from jax.experimental import pallas as pl
from jax.experimental.pallas import tpu as pltpu

D_MODEL = 1024
DEPTH = 4
GRID_W = 64
BRANCH_W = 256
N_BRANCH = 4
HEAD_DIM = 64
N_HEADS = BRANCH_W // HEAD_DIM
ATTN_SCALE = HEAD_DIM ** -0.5
NA_KH = 8
NA_KW = 16
HG_CHUNK = 64
HG_SUB = 16
SG_GROUPS = 4
SG_CHUNK = 128
DIL_CONFIGS = ((128, 1), (512, 4), (2048, 16))
DIL_BLK = 64
D_FF = 2816
IN_WIDTH = 3328
NEG = -1e30
EPS = 1e-6

NA_ROWS_PER_STEP = 8
NA_TOK = NA_ROWS_PER_STEP * GRID_W
NA_WIN = NA_KH * GRID_W

VMEM_LIMIT = 56 * 1024 * 1024

BF16 = jnp.bfloat16
F32 = jnp.float32


def _bf(x):
    return x.astype(BF16)


def _dot(a, b):
    return jnp.dot(a, b, preferred_element_type=F32)


def _dot_nt(a, b):
    return lax.dot_general(a, b, (((1,), (1,)), ((), ())), preferred_element_type=F32)


def _dot_tn(a, b):
    return lax.dot_general(a, b, (((0,), (0,)), ((), ())), preferred_element_type=F32)


def _sigmoid(x):
    return 1.0 / (1.0 + jnp.exp(-x))


def _head_masks(dtype):
    lane_head = lax.broadcasted_iota(jnp.int32, (1, BRANCH_W), 1) // HEAD_DIM
    return [(lane_head == h).astype(dtype) for h in range(N_HEADS)]


def _pick_heads(blocks):
    lane_head = lax.broadcasted_iota(jnp.int32, (1, BRANCH_W), 1) // HEAD_DIM
    out = blocks[N_HEADS - 1]
    for h in range(N_HEADS - 2, -1, -1):
        out = jnp.where(lane_head <= h, blocks[h], out)
    return out


def _mod_norm(x, g, scale, shift):
    y = x * lax.rsqrt(jnp.mean(x * x, axis=-1, keepdims=True) + EPS)
    return (y * g) * (1.0 + scale) + shift


def _params(*sem):
    return pltpu.CompilerParams(dimension_semantics=sem, vmem_limit_bytes=VMEM_LIMIT)


def _const_spec(shape):
    nd = len(shape)
    return pl.BlockSpec(shape, lambda *_: (0,) * nd, pipeline_mode=pl.Buffered(1))


def _ada_kernel(c_ref, w_ref, b_ref, o_ref):
    c = c_ref[...]
    cs = c * _sigmoid(c)
    o_ref[0] = _dot(_bf(cs), _bf(w_ref[0])) + b_ref[0]


def _ada_call(c_all, w_ada, b_ada):
    nb = c_all.shape[0]
    tn = 1536
    return pl.pallas_call(
        _ada_kernel,
        out_shape=jax.ShapeDtypeStruct((DEPTH, nb, 6 * D_MODEL), F32),
        grid=(DEPTH, 6 * D_MODEL // tn),
        in_specs=[pl.BlockSpec((nb, D_MODEL), lambda l, n: (0, 0)),
                  pl.BlockSpec((1, D_MODEL, tn), lambda l, n: (l, 0, n)),
                  pl.BlockSpec((1, 1, tn), lambda l, n: (l, 0, n))],
        out_specs=pl.BlockSpec((1, nb, tn), lambda l, n: (l, 0, n)),
        compiler_params=_params("arbitrary", "arbitrary"),
        name="ada",
    )(c_all, w_ada, b_ada.reshape(DEPTH, 1, 6 * D_MODEL))


PROJ_TM = 512


LANES = 128


def _store_lane_chunks(scr_ref, val):
    for c in range(val.shape[1] // LANES):
        scr_ref[c] = val[:, c * LANES:(c + 1) * LANES]


def _load_class(scr_ref, r, n, dil):
    return jnp.concatenate([scr_ref[c, pl.ds(r, n, stride=dil), :] for c in range(scr_ref.shape[0])], axis=1)


def _gelu_tanh(x):
    return 0.5 * x * (1.0 + jnp.tanh(np.sqrt(2.0 / np.pi).astype(np.float32) * (x + 0.044715 * (x * x * x))))


def _spatial_gating(uv, ln_g, ln_b, w, b):
    lane_group = lax.broadcasted_iota(jnp.int32, (1, BRANCH_W), 1) // (BRANCH_W // SG_GROUPS)
    u = _gelu_tanh(uv[:, 0:BRANCH_W])
    v = _gelu_tanh(uv[:, BRANCH_W:2 * BRANCH_W])
    mu = jnp.mean(v, axis=-1, keepdims=True)
    var = jnp.mean(jnp.square(v - mu), axis=-1, keepdims=True)
    v = (v - mu) * lax.rsqrt(var + EPS) * ln_g + ln_b
    vstack = jnp.concatenate([_bf(jnp.where(lane_group == g, v, 0.0)) for g in range(SG_GROUPS)], axis=0)
    return u * (_dot(w, vstack) + b)


def _proj_kernel(x_ref, mod_ref, g_ref, w_ref, lng_ref, lnb_ref, sgw_ref, sgb_ref,
                 na_ref, hg_ref, sg_ref, dl1_ref, dl4_ref, dl16_ref, dl_scr):
    h = _bf(_mod_norm(x_ref[0], g_ref[...], mod_ref[0, 1:2, :], mod_ref[0, 0:1, :]))
    na_ref[0] = _bf(_dot(h, w_ref[:, 0:768]))
    hg_ref[0] = _dot(h, w_ref[:, 768:2048])
    uv = _dot(h, w_ref[:, 2048:2560])
    for n in range(PROJ_TM // SG_CHUNK):
        rows = slice(n * SG_CHUNK, (n + 1) * SG_CHUNK)
        sg_ref[0, rows, :] = _bf(_spatial_gating(uv[rows, :], lng_ref[...], lnb_ref[...], sgw_ref[...], sgb_ref[...]))
    dl = _dot(h, w_ref[:, 2560:3328])
    dl1_ref[0] = _bf(dl)
    _store_lane_chunks(dl_scr, dl)
    for dil, ref in ((DIL_CONFIGS[1][1], dl4_ref), (DIL_CONFIGS[2][1], dl16_ref)):
        for r in range(dil):
            ref[0, r] = _bf(_load_class(dl_scr, r, PROJ_TM // dil, dil))


def _proj_call(x, mod, g, w_in, sg_ln_g, sg_ln_b, sg_w, sg_b):
    B, L, _ = x.shape
    tm = PROJ_TM
    d4, d16 = DIL_CONFIGS[1][1], DIL_CONFIGS[2][1]
    tok = lambda w: pl.BlockSpec((1, tm, w), lambda b, t: (b, t, 0))
    cls = lambda d: pl.BlockSpec((1, d, tm // d, 768), lambda b, t: (b, 0, t, 0))
    return pl.pallas_call(
        _proj_kernel,
        out_shape=(jax.ShapeDtypeStruct((B, L, 768), BF16),
                   jax.ShapeDtypeStruct((B, L, 1280), F32),
                   jax.ShapeDtypeStruct((B, L, BRANCH_W), BF16),
                   jax.ShapeDtypeStruct((B, L, 768), BF16),
                   jax.ShapeDtypeStruct((B, d4, L // d4, 768), BF16),
                   jax.ShapeDtypeStruct((B, d16, L // d16, 768), BF16)),
        grid=(B, L // tm),
        in_specs=[tok(D_MODEL),
                  pl.BlockSpec((1, 6, D_MODEL), lambda b, t: (b, 0, 0)),
                  _const_spec((1, D_MODEL)),
                  _const_spec((D_MODEL, IN_WIDTH)),
                  _const_spec((1, BRANCH_W)), _const_spec((1, BRANCH_W)),
                  _const_spec((SG_CHUNK, SG_GROUPS * SG_CHUNK)),
                  _const_spec((SG_CHUNK, BRANCH_W))],
        out_specs=(tok(768), tok(1280), tok(BRANCH_W), tok(768), cls(d4), cls(d16)),
        scratch_shapes=[pltpu.VMEM((768 // LANES, tm, LANES), F32)],
        compiler_params=_params("parallel", "arbitrary"),
        name="proj",
    )(x, mod, g, w_in, sg_ln_g, sg_ln_b, sg_w, sg_b)


def _na_bias_table(rpb):
    c = np.arange(GRID_W)
    kc = np.arange(GRID_W)
    col_off = np.clip(kc[None, :] - c[:, None], -(NA_KW - 1), NA_KW - 1) + NA_KW - 1
    win0 = np.clip(c - NA_KW // 2, 0, GRID_W - NA_KW)
    valid = (kc[None, :] >= win0[:, None]) & (kc[None, :] < win0[:, None] + NA_KW)
    onehot = (col_off[None] == np.arange(2 * NA_KW - 1)[:, None, None]).astype(np.float32)
    cols = jnp.einsum("hrj,jck->hrck", rpb.astype(F32), jnp.asarray(onehot), precision=lax.Precision.HIGHEST)
    cols = jnp.where(valid[None, None], cols, NEG)
    pats = jnp.stack([cols[:, NA_KH - 1 - d:2 * NA_KH - 1 - d] for d in range(NA_KH)])
    return pats.transpose(0, 1, 3, 2, 4).reshape(NA_KH, N_HEADS * GRID_W, NA_WIN)


def _na_kernel(q_ref, kp_ref, kc_ref, kn_ref, vp_ref, vc_ref, vn_ref, bias_ref, o_ref, kbuf, vbuf, *, rows):
    i = pl.program_id(1)
    for n, (kr, vr) in enumerate(((kp_ref, vp_ref), (kc_ref, vc_ref), (kn_ref, vn_ref))):
        kbuf[n * NA_TOK:(n + 1) * NA_TOK, :] = kr[0]
        vbuf[n * NA_TOK:(n + 1) * NA_TOK, :] = vr[0]
    qmasks = [m * ATTN_SCALE for m in _head_masks(BF16)]

    def row_body(j, carry):
        r = i * NA_ROWS_PER_STEP + j
        start = jnp.clip(r - NA_KH // 2, 0, rows - NA_KH)
        off = pl.multiple_of((start - (i - 1) * NA_ROWS_PER_STEP) * GRID_W, GRID_W)
        delta = r - start
        qoff = pl.multiple_of(j * GRID_W, GRID_W)
        q = q_ref[0, pl.ds(qoff, GRID_W), :]
        kw = kbuf[pl.ds(off, NA_WIN), :]
        vw = vbuf[pl.ds(off, NA_WIN), :]
        q_heads = jnp.concatenate([q * qmasks[h] for h in range(N_HEADS)], axis=0)
        s = _dot_nt(q_heads, kw) + bias_ref[delta]
        m = jnp.max(s, axis=-1, keepdims=True)
        e = jnp.exp(s - m)
        den = jnp.sum(e, axis=-1, keepdims=True)
        r_all = _dot(_bf(e), vw) * (1.0 / den)
        out = _pick_heads([r_all[h * GRID_W:(h + 1) * GRID_W, :] for h in range(N_HEADS)])
        o_ref[0, pl.ds(qoff, GRID_W), :] = _bf(out)
        return carry

    lax.fori_loop(0, NA_ROWS_PER_STEP, row_body, 0, unroll=2)


def _na_call(na_qkv, bias_tab):
    B, L, _ = na_qkv.shape
    rows = L // GRID_W
    nblk = rows // NA_ROWS_PER_STEP
    blk = lambda col, shift: pl.BlockSpec(
        (1, NA_TOK, BRANCH_W), lambda b, i: (b, jnp.clip(i + shift, 0, nblk - 1), col))
    return pl.pallas_call(
        functools.partial(_na_kernel, rows=rows),
        out_shape=jax.ShapeDtypeStruct((B, L, BRANCH_W), BF16),
        grid=(B, nblk),
        in_specs=[blk(0, 0), blk(1, -1), blk(1, 0), blk(1, 1), blk(2, -1), blk(2, 0), blk(2, 1),
                  _const_spec((NA_KH, N_HEADS * GRID_W, NA_WIN))],
        out_specs=pl.BlockSpec((1, NA_TOK, BRANCH_W), lambda b, i: (b, i, 0)),
        scratch_shapes=[pltpu.VMEM((3 * NA_TOK, BRANCH_W), BF16), pltpu.VMEM((3 * NA_TOK, BRANCH_W), BF16)],
        compiler_params=_params("parallel", "arbitrary"),
        name="na",
    )(na_qkv, na_qkv, na_qkv, na_qkv, na_qkv, na_qkv, na_qkv, bias_tab)


DL_TQ = 128
DL_TQB = 512


def _alibi_slopes(n):
    start = 2.0 ** (-8.0 / n)
    return [float(np.float32(start ** (h + 1))) for h in range(n)]


def _dl_kernel(*refs, n, kw_len, tqb, cg, first, last, split):
    if first:
        q_ref, k_ref, v_ref, bias_ref = refs[:4]
        rest = refs[4:]
    else:
        q_ref, k_ref, v_ref, bias_ref, op_ref, lsep_ref = refs[:6]
        rest = refs[6:]
    t = pl.program_id(2)
    qmasks = [m * ATTN_SCALE for m in _head_masks(BF16)]
    nlane = BRANCH_W // LANES

    for c in range(cg):
        def tile_body(u, carry, c=c):
            base = t * tqb + u * DL_TQ
            ws = pl.multiple_of(jnp.clip(base - DIL_BLK, 0, n - kw_len), DIL_BLK)
            rows = pl.ds(pl.multiple_of(u * DL_TQ, DL_TQ), DL_TQ)
            q = q_ref[0, c, rows, :]
            kw = k_ref[0, c, pl.ds(ws, kw_len), :]
            vw = v_ref[0, c, pl.ds(ws, kw_len), :]
            q_heads = jnp.concatenate([q * qmasks[h] for h in range(N_HEADS)], axis=0)
            s = _dot_nt(q_heads, kw) + bias_ref[(base - ws) // DIL_BLK]
            m = jnp.max(s, axis=-1, keepdims=True)
            e = jnp.exp(s - m)
            den = jnp.sum(e, axis=-1, keepdims=True)
            r_all = _dot(_bf(e), vw) * (1.0 / den)
            lse_col = m + jnp.log(den)
            o = _pick_heads([r_all[h * DL_TQ:(h + 1) * DL_TQ, :] for h in range(N_HEADS)])
            lse = _pick_heads([lse_col[h * DL_TQ:(h + 1) * DL_TQ, :] for h in range(N_HEADS)])
            if not first:
                lse_p = lsep_ref[0, c, rows, :]
                lse_m = jnp.maximum(lse_p, lse)
                w_p = jnp.exp(lse_p - lse_m)
                w_c = jnp.exp(lse - lse_m)
                tot = w_p + w_c
                o = (w_p * op_ref[0, c, rows, :] + w_c * o) / tot
                lse = lse_m + jnp.log(tot)
            if last:
                rest[0][0, c, rows, :] = _bf(o)
            else:
                srows = pl.ds(pl.multiple_of(c * tqb + u * DL_TQ, DL_TQ), DL_TQ)
                for val, scr in ((o, rest[2]), (lse, rest[3])):
                    for lc in range(nlane):
                        scr[lc, srows, :] = val[:, lc * LANES:(lc + 1) * LANES]
            return carry

        lax.fori_loop(0, tqb // DL_TQ, tile_body, 0, unroll=2)

    if not last:
        for out_ref, scr in ((rest[0], rest[2]), (rest[1], rest[3])):
            for c in range(cg):
                for a in range(split):
                    out_ref[0, a, c] = jnp.concatenate(
                        [scr[lc, pl.ds(c * tqb + a, tqb // split, stride=split), :] for lc in range(nlane)], axis=1)


def _dl_bias_table(dil, kw_len):
    slopes = np.asarray(_alibi_slopes(N_HEADS), np.float32)
    qi = np.arange(DL_TQ)[None, :, None]
    ki = np.arange(kw_len)[None, None, :] - DIL_BLK * np.arange(3)[:, None, None]
    step = np.abs(ki - qi)
    dist = (step * dil).astype(np.float32)
    bias = np.where((step <= DIL_BLK)[:, None], -slopes[None, :, None, None] * dist[:, None], np.float32(NEG))
    return jnp.asarray(bias.reshape(3, N_HEADS * DL_TQ, kw_len), F32)


def _dl_call(qkv, state, dil, first, last, split):
    B, _, n, _ = qkv.shape
    kw_len = min(4 * DIL_BLK, n)
    tqb = min(DL_TQB, n)
    cg = min(dil, max(1, DL_TQB // tqb))
    tok = lambda c: pl.BlockSpec((1, cg, tqb, BRANCH_W), lambda b, r, t: (b, r, t, c))
    full = lambda c: pl.BlockSpec((1, cg, n, BRANCH_W), lambda b, r, t: (b, r, 0, c))
    in_specs = [tok(0), full(1), full(2), _const_spec((3, N_HEADS * DL_TQ, kw_len))]
    args = [qkv, qkv, qkv, _dl_bias_table(dil, kw_len)]
    if not first:
        in_specs += [tok(0), tok(0)]
        args += list(state)
    scratch = []
    if last:
        out_shape = jax.ShapeDtypeStruct((B, dil, n, BRANCH_W), BF16)
        out_specs = tok(0)
    else:
        shape = (B, split, dil, n // split, BRANCH_W)
        spec = lambda: pl.BlockSpec((1, split, cg, tqb // split, BRANCH_W), lambda b, r, t: (b, 0, r, t, 0))
        out_shape = (jax.ShapeDtypeStruct(shape, F32), jax.ShapeDtypeStruct(shape, F32))
        out_specs = (spec(), spec())
        scratch = [pltpu.VMEM((BRANCH_W // LANES, cg * tqb, LANES), F32) for _ in range(2)]
    out = pl.pallas_call(
        functools.partial(_dl_kernel, n=n, kw_len=kw_len, tqb=tqb, cg=cg, first=first, last=last, split=split),
        out_shape=out_shape,
        grid=(B, dil // cg, n // tqb),
        in_specs=in_specs,
        out_specs=out_specs,
        scratch_shapes=scratch,
        compiler_params=_params("parallel", "arbitrary", "arbitrary"),
        name=f"dl{dil}",
    )(*args)
    if last:
        return out
    return [o.reshape(B, split * dil, n // split, BRANCH_W) for o in out]


def _dilated_attention(dl1, dl4, dl16):
    (_, d1), (_, d4), (_, d16) = DIL_CONFIGS
    state = _dl_call(dl1[:, None], None, d1, True, False, d4 // d1)
    state = _dl_call(dl4, state, d4, False, False, d16 // d4)
    return _dl_call(dl16, state, d16, False, True, 1)


HG_TT = 512
LOG2E = float(np.log2(np.e))
N_SUB = HG_CHUNK // HG_SUB


def _split3(x):
    hi = _bf(x)
    r1 = x - hi.astype(F32)
    mid = _bf(r1)
    lo = _bf(r1 - mid.astype(F32))
    return hi, mid, lo


def _hg_chunk(x, q_pre, v, lb, st_ref, tri_ref, bd_ref, ones_ref, fwd):
    C, S = HG_CHUNK, HG_SUB
    e = jnp.exp(-jnp.abs(x))
    inv = 1.0 / (1.0 + e)
    pos = x >= 0
    sig_p = jnp.where(pos, inv, e * inv)
    sig_n = jnp.where(pos, e * inv, inv)
    logf = jnp.log(lb + (1.0 - lb) * sig_p)
    k = (1.0 - lb) * sig_n
    q = q_pre * _sigmoid(q_pre)

    tri = tri_ref[...]
    cum = sum(_dot(tri, part) for part in _split3(logf))

    edge = cum[C - 1:C, :] if fwd else cum[0:1, :]
    vb = _bf(v)
    masks = _head_masks(F32)

    st = st_ref[...]
    o = _dot_nt(_bf(q * jnp.exp(cum)), _bf(st))

    att_blocks = []
    for i in range(N_SUB):
        lo, hi = i * S, (i + 1) * S
        if fwd:
            if i == 0:
                att_blocks.append(None)
                continue
            ref = cum[lo - 1:lo, :]
            k_t = jnp.concatenate([k[0:lo, :] * jnp.exp(ref - cum[0:lo, :]), jnp.zeros((C - lo, BRANCH_W), F32)], axis=0)
        else:
            if i == N_SUB - 1:
                att_blocks.append(None)
                continue
            ref = cum[hi:hi + 1, :]
            k_t = jnp.concatenate([jnp.zeros((hi, BRANCH_W), F32), k[hi:C, :] * jnp.exp(ref - cum[hi:C, :])], axis=0)
        k_t = _bf(k_t)
        q_t = q[lo:hi, :] * jnp.exp(cum[lo:hi, :] - ref)
        q_bd = _bf(jnp.concatenate([q_t * masks[h] for h in range(N_HEADS)], axis=0))
        att = _dot_nt(q_bd, k_t)
        r = _dot(_bf(att), vb)
        att_blocks.append(_pick_heads([r[h * S:(h + 1) * S, :] for h in range(N_HEADS)]))

    ones_bd = ones_ref[...]
    half = S // 2
    half_rows = lax.broadcasted_iota(jnp.int32, (half, 1), 0)
    cum2 = cum * LOG2E
    o_blocks = []
    for i in range(N_SUB):
        lo, hi = i * S, (i + 1) * S
        cq, qq, kk_, vv = cum2[lo:hi, :], q[lo:hi, :], k[lo:hi, :], v[lo:hi, :]
        xs = []
        for s in range(S):
            parts = []
            for r0 in (0, half):
                if fwd:
                    everything, nothing = s <= r0, s > r0 + half - 1
                else:
                    everything, nothing = s >= r0 + half - 1, s < r0
                if nothing:
                    parts.append(jnp.zeros((half, BRANCH_W), F32))
                    continue
                d = cq[r0:r0 + half, :] - cq[s:s + 1, :]
                if not everything:
                    ok = (half_rows >= s - r0) if fwd else (half_rows <= s - r0)
                    d = jnp.where(ok, d, NEG)
                parts.append(qq[r0:r0 + half, :] * kk_[s:s + 1, :] * jnp.exp2(d))
            xs.append(_bf(jnp.concatenate(parts, axis=0)))
        z = _dot(jnp.concatenate(xs, axis=0), ones_bd)
        od_halves = [None, None]
        for s in range(S):
            for hidx, r0 in enumerate((0, half)):
                if (s > r0 + half - 1) if fwd else (s < r0):
                    continue
                term = z[s * S + r0:s * S + r0 + half, :] * vv[s:s + 1, :]
                od_halves[hidx] = term if od_halves[hidx] is None else od_halves[hidx] + term
        od = jnp.concatenate(od_halves, axis=0)
        if att_blocks[i] is not None:
            od = od + att_blocks[i]
        o_blocks.append(od)
    o = o + jnp.concatenate(o_blocks, axis=0)

    k_dec = _bf(k * jnp.exp(edge - cum))
    st_ref[...] = st * jnp.exp(edge) + _dot_tn(vb, k_dec) * bd_ref[...]
    return o


def _hg_kernel(*refs, fwd):
    if fwd:
        f_ref, q_ref, i_ref, lb_ref, tri_ref, bd_ref, ones_ref, o_ref, st_ref = refs
    else:
        f_ref, q_ref, i_ref, g_ref, of_ref, lb_ref, gn_ref, tri_ref, bd_ref, ones_ref, o_ref, st_ref = refs

    @pl.when(pl.program_id(1) == 0)
    def _():
        st_ref[...] = jnp.zeros_like(st_ref)

    nchunk = HG_TT // HG_CHUNK

    def chunk_body(ci, carry):
        c = ci if fwd else nchunk - 1 - ci
        rows = pl.ds(pl.multiple_of(c * HG_CHUNK, HG_CHUNK), HG_CHUNK)
        o = _hg_chunk(f_ref[0, rows, :], q_ref[0, rows, :], i_ref[0, rows, :], lb_ref[...],
                      st_ref, tri_ref, bd_ref, ones_ref, fwd)
        if fwd:
            o_ref[0, rows, :] = o
        else:
            o = o + of_ref[0, rows, :]
            sq = o * o
            hi = _bf(sq)
            lo = _bf(sq - hi.astype(F32))
            ms = (_dot(hi, ones_ref[...]) + _dot(lo, ones_ref[...])) * (1.0 / HEAD_DIM)
            g = g_ref[0, rows, :]
            o_ref[0, rows, :] = _bf(o * lax.rsqrt(ms + EPS) * gn_ref[...] * (g * _sigmoid(g)))
        return carry

    lax.fori_loop(0, nchunk, chunk_body, 0, unroll=2)


def _hg_call(hg, o_fwd, lb, gnorm, bd, ones_bd, fwd):
    B, L, _ = hg.shape
    nt = L // HG_TT
    tmap = (lambda t: t) if fwd else (lambda t: nt - 1 - t)
    col = lambda c: pl.BlockSpec((1, HG_TT, BRANCH_W), lambda b, t: (b, tmap(t), c))
    idx = np.arange(HG_CHUNK)
    tri = jnp.asarray((idx[None, :] <= idx[:, None]) if fwd else (idx[None, :] >= idx[:, None]), BF16)
    consts = [_const_spec((HG_CHUNK, HG_CHUNK)), _const_spec((BRANCH_W, BRANCH_W)), _const_spec((BRANCH_W, BRANCH_W))]
    vec = _const_spec((1, BRANCH_W))
    if fwd:
        in_specs = [col(0), col(2), col(3), vec] + consts
        args = [hg, hg, hg, lb, tri, bd, ones_bd]
        out_dtype = F32
    else:
        in_specs = [col(1), col(2), col(3), col(4), col(0), vec, vec] + consts
        args = [hg, hg, hg, hg, o_fwd, lb, gnorm, tri, bd, ones_bd]
        out_dtype = BF16
    return pl.pallas_call(
        functools.partial(_hg_kernel, fwd=fwd),
        out_shape=jax.ShapeDtypeStruct((B, L, BRANCH_W), out_dtype),
        grid=(B, nt),
        in_specs=in_specs,
        out_specs=col(0),
        scratch_shapes=[pltpu.VMEM((BRANCH_W, BRANCH_W), F32)],
        compiler_params=_params("parallel", "arbitrary"),
        name="hg_fwd" if fwd else "hg_bwd",
    )(*args)


MERGE_TM = 512


def _merge_kernel(x_ref, mod_ref, g_ref, na_ref, hg_ref, sg_ref, dl_ref, perm_ref, wg_ref, bg_ref, wb_ref, wo_ref, o_ref):
    x = x_ref[0]
    h = _bf(_mod_norm(x, g_ref[...], mod_ref[0, 1:2, :], mod_ref[0, 0:1, :]))
    dl_classes = jnp.concatenate([dl_ref[0, r] for r in range(dl_ref.shape[1])], axis=0)
    dl = _bf(_dot(perm_ref[...], dl_classes))
    merged = jnp.zeros((MERGE_TM, D_MODEL), F32)
    for bi, br in enumerate((na_ref[0], hg_ref[0], sg_ref[0], dl)):
        gate = _sigmoid(_dot(h, wg_ref[bi]) + bg_ref[bi:bi + 1, :])
        merged = merged + gate * _dot(br, wb_ref[bi])
    y = _dot(_bf(merged), wo_ref[...])
    o_ref[0] = x + mod_ref[0, 2:3, :] * y


def _merge_call(x, mod, g, branches, w_gate, b_gate, w_branch, w_o):
    B, L, _ = x.shape
    tm = MERGE_TM
    d16 = DIL_CONFIGS[2][1]
    tok_idx = np.arange(tm)
    perm = np.zeros((tm, tm), np.float32)
    perm[tok_idx, (tok_idx % d16) * (tm // d16) + tok_idx // d16] = 1.0
    tok = lambda w: pl.BlockSpec((1, tm, w), lambda b, t: (b, t, 0))
    return pl.pallas_call(
        _merge_kernel,
        out_shape=jax.ShapeDtypeStruct((B, L, D_MODEL), F32),
        grid=(B, L // tm),
        in_specs=[tok(D_MODEL),
                  pl.BlockSpec((1, 6, D_MODEL), lambda b, t: (b, 0, 0)),
                  _const_spec((1, D_MODEL)),
                  tok(BRANCH_W), tok(BRANCH_W), tok(BRANCH_W),
                  pl.BlockSpec((1, d16, tm // d16, BRANCH_W), lambda b, t: (b, 0, t, 0)),
                  _const_spec((tm, tm)),
                  _const_spec((N_BRANCH, D_MODEL, D_MODEL)),
                  _const_spec((N_BRANCH, D_MODEL)),
                  _const_spec((N_BRANCH, BRANCH_W, D_MODEL)),
                  _const_spec((D_MODEL, D_MODEL))],
        out_specs=tok(D_MODEL),
        compiler_params=_params("parallel", "arbitrary"),
        name="merge",
    )(x, mod, g, *branches, jnp.asarray(perm, BF16), w_gate, b_gate, w_branch, w_o)


FFN_TM = 512
FFN_SPLIT = 2


def _ffn_kernel(x_ref, mod_ref, g_ref, wg_ref, wu_ref, wd_ref, gf_ref, o_ref, *, final):
    x = x_ref[0]
    h = _bf(_mod_norm(x, g_ref[...], mod_ref[0, 4:5, :], mod_ref[0, 3:4, :]))
    y = jnp.zeros((FFN_TM, D_MODEL), F32)
    fc = D_FF // FFN_SPLIT
    for c in range(FFN_SPLIT):
        a = _dot(h, wg_ref[:, c * fc:(c + 1) * fc])
        b = _dot(h, wu_ref[:, c * fc:(c + 1) * fc])
        y = y + _dot(_bf(a * _sigmoid(a) * b), wd_ref[c * fc:(c + 1) * fc, :])
    x = x + mod_ref[0, 5:6, :] * y
    if final:
        x = x * lax.rsqrt(jnp.mean(x * x, axis=-1, keepdims=True) + EPS) * gf_ref[...]
    o_ref[0] = x


def _ffn_call(x, mod, g, wg, wu, wd, g_final, final):
    B, L, _ = x.shape
    tm = FFN_TM
    tok = pl.BlockSpec((1, tm, D_MODEL), lambda b, t: (b, t, 0))
    return pl.pallas_call(
        functools.partial(_ffn_kernel, final=final),
        out_shape=jax.ShapeDtypeStruct((B, L, D_MODEL), F32),
        grid=(B, L // tm),
        in_specs=[tok,
                  pl.BlockSpec((1, 6, D_MODEL), lambda b, t: (b, 0, 0)),
                  _const_spec((1, D_MODEL)),
                  _const_spec((D_MODEL, D_FF)), _const_spec((D_MODEL, D_FF)), _const_spec((D_FF, D_MODEL)),
                  _const_spec((1, D_MODEL))],
        out_specs=tok,
        compiler_params=_params("parallel", "arbitrary"),
        name="ffn",
    )(x, mod, g, wg, wu, wd, g_final)


def _trunk(x, mods, p):
    for l in range(DEPTH):
        mod = mods[l]
        na_qkv, hg, y_sg, dl1, dl4, dl16 = _proj_call(x, mod, p["g_mix"][l], p["w_in"][l], p["sg_ln_g"][l],
                                                      p["sg_ln_b"][l], p["sg_w"][l], p["sg_b"][l])
        y_na = _na_call(na_qkv, p["na_bias"][l])
        o_f = _hg_call(hg, None, p["lb"][0][l], None, p["bd"], p["ones_bd"], True)
        y_hg = _hg_call(hg, o_f, p["lb"][1][l], p["hg_gnorm"][l], p["bd"], p["ones_bd"], False)
        y_dl = _dilated_attention(dl1, dl4, dl16)
        x = _merge_call(x, mod, p["g_mix"][l], (y_na, y_hg, y_sg, y_dl),
                        p["w_gate"][l], p["b_gate"][l], p["w_branch"][l], p["w_o"][l])
        x = _ffn_call(x, mod, p["g_ffn"][l], p["w_ffn_gate"][l], p["w_ffn_up"][l], p["w_ffn_down"][l],
                      p["g_final"], l == DEPTH - 1)
    return x


def kernel(x_prompt, x_sample, c_prompt, c_sample, w_ada, b_ada, g_norm_mix, g_norm_ffn, w_in, w_gate, b_gate,
           w_branch, w_o, na_rpb, hg_lb, hg_gnorm, sg_ln_g, sg_ln_b, sg_w, sg_b, w_ffn_gate, w_ffn_up,
           w_ffn_down, g_final):
    nbp, nbs = c_prompt.shape[0], c_sample.shape[0]
    nb_pad = -(-(nbp + nbs) // 16) * 16
    c_all = jnp.concatenate([c_prompt, c_sample, jnp.zeros((nb_pad - nbp - nbs, D_MODEL), F32)], axis=0)
    mod_all = _ada_call(c_all, w_ada, b_ada).reshape(DEPTH, nb_pad, 6, D_MODEL)
    mods_p = [mod_all[l, :nbp] for l in range(DEPTH)]
    mods_s = [mod_all[l, nbp:nbp + nbs] for l in range(DEPTH)]

    lb_soft = jax.nn.softmax(hg_lb.astype(F32), axis=1)
    lower = jnp.cumsum(lb_soft, axis=1) - lb_soft[:, :1]
    head = np.arange(BRANCH_W) // HEAD_DIM
    bd = jnp.asarray((head[:, None] == head[None, :]).astype(np.float32))
    vec = lambda a: [a[l].reshape(1, -1).astype(F32) for l in range(DEPTH)]
    p = {
        "g_mix": vec(g_norm_mix), "g_ffn": vec(g_norm_ffn),
        "w_in": [_bf(w_in[l]) for l in range(DEPTH)],
        "w_gate": [_bf(w_gate[l]) for l in range(DEPTH)],
        "b_gate": [b_gate[l] for l in range(DEPTH)],
        "w_branch": [_bf(w_branch[l]) for l in range(DEPTH)],
        "w_o": [_bf(w_o[l]) for l in range(DEPTH)],
        "na_bias": [_na_bias_table(na_rpb[l]) for l in range(DEPTH)],
        "lb": [[lower[d, l].reshape(1, BRANCH_W) for l in range(DEPTH)] for d in range(2)],
        "hg_gnorm": vec(hg_gnorm), "sg_ln_g": vec(sg_ln_g), "sg_ln_b": vec(sg_ln_b),
        "sg_w": [_bf(sg_w[l].transpose(1, 0, 2).reshape(SG_CHUNK, SG_GROUPS * SG_CHUNK)) for l in range(DEPTH)],
        "sg_b": [jnp.repeat(sg_b[l].T.astype(F32), BRANCH_W // SG_GROUPS, axis=1) for l in range(DEPTH)],
        "w_ffn_gate": [_bf(w_ffn_gate[l]) for l in range(DEPTH)],
        "w_ffn_up": [_bf(w_ffn_up[l]) for l in range(DEPTH)],
        "w_ffn_down": [_bf(w_ffn_down[l]) for l in range(DEPTH)],
        "g_final": g_final.reshape(1, D_MODEL).astype(F32),
        "bd": bd, "ones_bd": _bf(bd),
    }
    return (_trunk(x_prompt, mods_p, p), _trunk(x_sample, mods_s, p))
```

```python
import functools

import jax
import jax.numpy as jnp
import numpy as np
from jax import lax
from jax.experimental import pallas as pl
from jax.experimental.pallas import tpu as pltpu

D_MODEL = 1024
DEPTH = 4
GRID_W = 64
BRANCH_W = 256
N_BRANCH = 4
HEAD_DIM = 64
N_HEADS = BRANCH_W // HEAD_DIM
ATTN_SCALE = HEAD_DIM ** -0.5
NA_KH = 8
NA_KW = 16
HG_CHUNK = 64
HG_SUB = 16
SG_GROUPS = 4
SG_CHUNK = 128
DIL_CONFIGS = ((128, 1), (512, 4), (2048, 16))
DIL_BLK = 64
D_FF = 2816
IN_WIDTH = 3328
NEG = -1e30
EPS = 1e-6

NA_ROWS_PER_STEP = 8
NA_TOK = NA_ROWS_PER_STEP * GRID_W
NA_WIN = NA_KH * GRID_W

VMEM_LIMIT = 56 * 1024 * 1024

BF16 = jnp.bfloat16
F32 = jnp.float32


def _bf(x):
    return x.astype(BF16)


def _dot(a, b):
    return jnp.dot(a, b, preferred_element_type=F32)


def _dot_nt(a, b):
    return lax.dot_general(a, b, (((1,), (1,)), ((), ())), preferred_element_type=F32)


def _dot_tn(a, b):
    return lax.dot_general(a, b, (((0,), (0,)), ((), ())), preferred_element_type=F32)


def _sigmoid(x):
    return 1.0 / (1.0 + jnp.exp(-x))


def _head_masks(dtype):
    lane_head = lax.broadcasted_iota(jnp.int32, (1, BRANCH_W), 1) // HEAD_DIM
    return [(lane_head == h).astype(dtype) for h in range(N_HEADS)]


def _pick_heads(blocks):
    lane_head = lax.broadcasted_iota(jnp.int32, (1, BRANCH_W), 1) // HEAD_DIM
    out = blocks[N_HEADS - 1]
    for h in range(N_HEADS - 2, -1, -1):
        out = jnp.where(lane_head <= h, blocks[h], out)
    return out


def _mod_norm(x, g, scale, shift):
    y = x * lax.rsqrt(jnp.mean(x * x, axis=-1, keepdims=True) + EPS)
    return (y * g) * (1.0 + scale) + shift


def _params(*sem):
    return pltpu.CompilerParams(dimension_semantics=sem, vmem_limit_bytes=VMEM_LIMIT)


def _const_spec(shape):
    nd = len(shape)
    return pl.BlockSpec(shape, lambda *_: (0,) * nd, pipeline_mode=pl.Buffered(1))


def _ada_kernel(c_ref, w_ref, b_ref, o_ref):
    c = c_ref[...]
    cs = c * _sigmoid(c)
    o_ref[0] = _dot(_bf(cs), _bf(w_ref[0])) + b_ref[0]


def _ada_call(c_all, w_ada, b_ada):
    nb = c_all.shape[0]
    tn = 1536
    return pl.pallas_call(
        _ada_kernel,
        out_shape=jax.ShapeDtypeStruct((DEPTH, nb, 6 * D_MODEL), F32),
        grid=(DEPTH, 6 * D_MODEL // tn),
        in_specs=[pl.BlockSpec((nb, D_MODEL), lambda l, n: (0, 0)),
                  pl.BlockSpec((1, D_MODEL, tn), lambda l, n: (l, 0, n)),
                  pl.BlockSpec((1, 1, tn), lambda l, n: (l, 0, n))],
        out_specs=pl.BlockSpec((1, nb, tn), lambda l, n: (l, 0, n)),
        compiler_params=_params("arbitrary", "arbitrary"),
        name="ada",
    )(c_all, w_ada, b_ada.reshape(DEPTH, 1, 6 * D_MODEL))


PROJ_TM = 512


LANES = 128


def _store_lane_chunks(scr_ref, val):
    for c in range(val.shape[1] // LANES):
        scr_ref[c] = val[:, c * LANES:(c + 1) * LANES]


def _load_class(scr_ref, r, n, dil):
    return jnp.concatenate([scr_ref[c, pl.ds(r, n, stride=dil), :] for c in range(scr_ref.shape[0])], axis=1)


def _gelu_tanh(x):
    return 0.5 * x * (1.0 + jnp.tanh(np.sqrt(2.0 / np.pi).astype(np.float32) * (x + 0.044715 * (x * x * x))))


def _spatial_gating(uv, ln_g, ln_b, w, b):
    lane_group = lax.broadcasted_iota(jnp.int32, (1, BRANCH_W), 1) // (BRANCH_W // SG_GROUPS)
    u = _gelu_tanh(uv[:, 0:BRANCH_W])
    v = _gelu_tanh(uv[:, BRANCH_W:2 * BRANCH_W])
    mu = jnp.mean(v, axis=-1, keepdims=True)
    var = jnp.mean(jnp.square(v - mu), axis=-1, keepdims=True)
    v = (v - mu) * lax.rsqrt(var + EPS) * ln_g + ln_b
    vstack = jnp.concatenate([_bf(jnp.where(lane_group == g, v, 0.0)) for g in range(SG_GROUPS)], axis=0)
    return u * (_dot(w, vstack) + b)


def _proj_kernel(x_ref, mod_ref, g_ref, w_ref, lng_ref, lnb_ref, sgw_ref, sgb_ref,
                 na_ref, hg_ref, sg_ref, dl1_ref, dl4_ref, dl16_ref, dl_scr):
    h = _bf(_mod_norm(x_ref[0], g_ref[...], mod_ref[0, 1:2, :], mod_ref[0, 0:1, :]))
    uv = _dot(h, w_ref[:, 2048:2560])
    dl = _dot(h, w_ref[:, 2560:3328])
    for n in range(PROJ_TM // SG_CHUNK):
        rows = slice(n * SG_CHUNK, (n + 1) * SG_CHUNK)
        sg_ref[0, rows, :] = _bf(_spatial_gating(uv[rows, :], lng_ref[...], lnb_ref[...], sgw_ref[...], sgb_ref[...]))
    dl1_ref[0] = _bf(dl)
    _store_lane_chunks(dl_scr, dl)
    for dil, ref in ((DIL_CONFIGS[1][1], dl4_ref), (DIL_CONFIGS[2][1], dl16_ref)):
        for r in range(dil):
            ref[0, r] = _bf(_load_class(dl_scr, r, PROJ_TM // dil, dil))
    na_ref[0] = _bf(_dot(h, w_ref[:, 0:768]))
    hg_ref[0] = _dot(h, w_ref[:, 768:2048])


def _proj_call(x, mod, g, w_in, sg_ln_g, sg_ln_b, sg_w, sg_b):
    B, L, _ = x.shape
    tm = PROJ_TM
    d4, d16 = DIL_CONFIGS[1][1], DIL_CONFIGS[2][1]
    tok = lambda w: pl.BlockSpec((1, tm, w), lambda b, t: (b, t, 0))
    cls = lambda d: pl.BlockSpec((1, d, tm // d, 768), lambda b, t: (b, 0, t, 0))
    return pl.pallas_call(
        _proj_kernel,
        out_shape=(jax.ShapeDtypeStruct((B, L, 768), BF16),
                   jax.ShapeDtypeStruct((B, L, 1280), F32),
                   jax.ShapeDtypeStruct((B, L, BRANCH_W), BF16),
                   jax.ShapeDtypeStruct((B, L, 768), BF16),
                   jax.ShapeDtypeStruct((B, d4, L // d4, 768), BF16),
                   jax.ShapeDtypeStruct((B, d16, L // d16, 768), BF16)),
        grid=(B, L // tm),
        in_specs=[tok(D_MODEL),
                  pl.BlockSpec((1, 6, D_MODEL), lambda b, t: (b, 0, 0)),
                  _const_spec((1, D_MODEL)),
                  _const_spec((D_MODEL, IN_WIDTH)),
                  _const_spec((1, BRANCH_W)), _const_spec((1, BRANCH_W)),
                  _const_spec((SG_CHUNK, SG_GROUPS * SG_CHUNK)),
                  _const_spec((SG_CHUNK, BRANCH_W))],
        out_specs=(tok(768), tok(1280), tok(BRANCH_W), tok(768), cls(d4), cls(d16)),
        scratch_shapes=[pltpu.VMEM((768 // LANES, tm, LANES), F32)],
        compiler_params=_params("parallel", "arbitrary"),
        name="proj",
    )(x, mod, g, w_in, sg_ln_g, sg_ln_b, sg_w, sg_b)


def _na_bias_table(rpb):
    c = np.arange(GRID_W)
    kc = np.arange(GRID_W)
    col_off = np.clip(kc[None, :] - c[:, None], -(NA_KW - 1), NA_KW - 1) + NA_KW - 1
    win0 = np.clip(c - NA_KW // 2, 0, GRID_W - NA_KW)
    valid = (kc[None, :] >= win0[:, None]) & (kc[None, :] < win0[:, None] + NA_KW)
    onehot = (col_off[None] == np.arange(2 * NA_KW - 1)[:, None, None]).astype(np.float32)
    cols = jnp.einsum("hrj,jck->hrck", rpb.astype(F32), jnp.asarray(onehot), precision=lax.Precision.HIGHEST)
    cols = jnp.where(valid[None, None], cols, NEG)
    pats = jnp.stack([cols[:, NA_KH - 1 - d:2 * NA_KH - 1 - d] for d in range(NA_KH)])
    return pats.transpose(0, 1, 3, 2, 4).reshape(NA_KH, N_HEADS * GRID_W, NA_WIN)


def _na_kernel(q_ref, k_ref, v_ref, bias_ref, o_ref, *, rows):
    i = pl.program_id(1)
    qmasks = [m * ATTN_SCALE for m in _head_masks(BF16)]

    def row_body(j, carry):
        r = i * NA_ROWS_PER_STEP + j
        start = jnp.clip(r - NA_KH // 2, 0, rows - NA_KH)
        off = pl.multiple_of(start * GRID_W, GRID_W)
        delta = r - start
        qoff = pl.multiple_of(j * GRID_W, GRID_W)
        q = q_ref[0, pl.ds(qoff, GRID_W), :]
        kw = k_ref[0, pl.ds(off, NA_WIN), :]
        vw = v_ref[0, pl.ds(off, NA_WIN), :]
        q_heads = jnp.concatenate([q * qmasks[h] for h in range(N_HEADS)], axis=0)
        s = _dot_nt(q_heads, kw) + bias_ref[delta]
        m = jnp.max(s, axis=-1, keepdims=True)
        e = jnp.exp(s - m)
        den = jnp.sum(e, axis=-1, keepdims=True)
        r_all = _dot(_bf(e), vw) * (1.0 / den)
        out = _pick_heads([r_all[h * GRID_W:(h + 1) * GRID_W, :] for h in range(N_HEADS)])
        o_ref[0, pl.ds(qoff, GRID_W), :] = _bf(out)
        return carry

    lax.fori_loop(0, NA_ROWS_PER_STEP, row_body, 0, unroll=True)


def _na_call(na_qkv, bias_tab):
    B, L, _ = na_qkv.shape
    rows = L // GRID_W
    seq = lambda col: pl.BlockSpec((1, L, BRANCH_W), lambda b, i: (b, 0, col))
    return pl.pallas_call(
        functools.partial(_na_kernel, rows=rows),
        out_shape=jax.ShapeDtypeStruct((B, L, BRANCH_W), BF16),
        grid=(B, rows // NA_ROWS_PER_STEP),
        in_specs=[pl.BlockSpec((1, NA_TOK, BRANCH_W), lambda b, i: (b, i, 0)), seq(1), seq(2),
                  _const_spec((NA_KH, N_HEADS * GRID_W, NA_WIN))],
        out_specs=pl.BlockSpec((1, NA_TOK, BRANCH_W), lambda b, i: (b, i, 0)),
        compiler_params=_params("parallel", "arbitrary"),
        name="na",
    )(na_qkv, na_qkv, na_qkv, bias_tab)


DL_TQ = 128
DL_TQB = 512


def _alibi_slopes(n):
    start = 2.0 ** (-8.0 / n)
    return [float(np.float32(start ** (h + 1))) for h in range(n)]


def _dl_kernel(*refs, n, kw_len, tqb, cg, first, last, split):
    if first:
        q_ref, k_ref, v_ref, bias_ref = refs[:4]
        rest = refs[4:]
    else:
        q_ref, k_ref, v_ref, bias_ref, op_ref, lsep_ref = refs[:6]
        rest = refs[6:]
    t = pl.program_id(2)
    qmasks = [m * ATTN_SCALE for m in _head_masks(BF16)]
    nlane = BRANCH_W // LANES

    for c in range(cg):
        def tile_body(u, carry, c=c):
            base = t * tqb + u * DL_TQ
            ws = pl.multiple_of(jnp.clip(base - DIL_BLK, 0, n - kw_len), DIL_BLK)
            rows = pl.ds(pl.multiple_of(u * DL_TQ, DL_TQ), DL_TQ)
            q = q_ref[0, c, rows, :]
            kw = k_ref[0, c, pl.ds(ws, kw_len), :]
            vw = v_ref[0, c, pl.ds(ws, kw_len), :]
            q_heads = jnp.concatenate([q * qmasks[h] for h in range(N_HEADS)], axis=0)
            s = _dot_nt(q_heads, kw) + bias_ref[(base - ws) // DIL_BLK]
            m = jnp.max(s, axis=-1, keepdims=True)
            e = jnp.exp(s - m)
            den = jnp.sum(e, axis=-1, keepdims=True)
            r_all = _dot(_bf(e), vw) * (1.0 / den)
            lse_col = m + jnp.log(den)
            o = _pick_heads([r_all[h * DL_TQ:(h + 1) * DL_TQ, :] for h in range(N_HEADS)])
            lse = _pick_heads([lse_col[h * DL_TQ:(h + 1) * DL_TQ, :] for h in range(N_HEADS)])
            if not first:
                lse_p = lsep_ref[0, c, rows, :]
                lse_m = jnp.maximum(lse_p, lse)
                w_p = jnp.exp(lse_p - lse_m)
                w_c = jnp.exp(lse - lse_m)
                tot = w_p + w_c
                o = (w_p * op_ref[0, c, rows, :] + w_c * o) / tot
                lse = lse_m + jnp.log(tot)
            if last:
                rest[0][0, c, rows, :] = _bf(o)
            else:
                srows = pl.ds(pl.multiple_of(c * tqb + u * DL_TQ, DL_TQ), DL_TQ)
                for val, scr in ((o, rest[2]), (lse, rest[3])):
                    for lc in range(nlane):
                        scr[lc, srows, :] = val[:, lc * LANES:(lc + 1) * LANES]
            return carry

        lax.fori_loop(0, tqb // DL_TQ, tile_body, 0, unroll=4)

    if not last:
        for out_ref, scr in ((rest[0], rest[2]), (rest[1], rest[3])):
            for c in range(cg):
                for a in range(split):
                    out_ref[0, a, c] = jnp.concatenate(
                        [scr[lc, pl.ds(c * tqb + a, tqb // split, stride=split), :] for lc in range(nlane)], axis=1)


def _dl_bias_table(dil, kw_len):
    slopes = np.asarray(_alibi_slopes(N_HEADS), np.float32)
    qi = np.arange(DL_TQ)[None, :, None]
    ki = np.arange(kw_len)[None, None, :] - DIL_BLK * np.arange(3)[:, None, None]
    step = np.abs(ki - qi)
    dist = (step * dil).astype(np.float32)
    bias = np.where((step <= DIL_BLK)[:, None], -slopes[None, :, None, None] * dist[:, None], np.float32(NEG))
    return jnp.asarray(bias.reshape(3, N_HEADS * DL_TQ, kw_len), F32)


def _dl_call(qkv, state, dil, first, last, split):
    B, _, n, _ = qkv.shape
    kw_len = min(4 * DIL_BLK, n)
    tqb = min(DL_TQB, n)
    cg = min(dil, max(1, DL_TQB // tqb))
    tok = lambda c: pl.BlockSpec((1, cg, tqb, BRANCH_W), lambda b, r, t: (b, r, t, c))
    full = lambda c: pl.BlockSpec((1, cg, n, BRANCH_W), lambda b, r, t: (b, r, 0, c))
    in_specs = [tok(0), full(1), full(2), _const_spec((3, N_HEADS * DL_TQ, kw_len))]
    args = [qkv, qkv, qkv, _dl_bias_table(dil, kw_len)]
    if not first:
        in_specs += [tok(0), tok(0)]
        args += list(state)
    scratch = []
    if last:
        out_shape = jax.ShapeDtypeStruct((B, dil, n, BRANCH_W), BF16)
        out_specs = tok(0)
    else:
        shape = (B, split, dil, n // split, BRANCH_W)
        spec = lambda: pl.BlockSpec((1, split, cg, tqb // split, BRANCH_W), lambda b, r, t: (b, 0, r, t, 0))
        out_shape = (jax.ShapeDtypeStruct(shape, F32), jax.ShapeDtypeStruct(shape, F32))
        out_specs = (spec(), spec())
        scratch = [pltpu.VMEM((BRANCH_W // LANES, cg * tqb, LANES), F32) for _ in range(2)]
    out = pl.pallas_call(
        functools.partial(_dl_kernel, n=n, kw_len=kw_len, tqb=tqb, cg=cg, first=first, last=last, split=split),
        out_shape=out_shape,
        grid=(B, dil // cg, n // tqb),
        in_specs=in_specs,
        out_specs=out_specs,
        scratch_shapes=scratch,
        compiler_params=_params("parallel", "arbitrary", "arbitrary"),
        name=f"dl{dil}",
    )(*args)
    if last:
        return out
    return [o.reshape(B, split * dil, n // split, BRANCH_W) for o in out]


def _dilated_attention(dl1, dl4, dl16):
    (_, d1), (_, d4), (_, d16) = DIL_CONFIGS
    state = _dl_call(dl1[:, None], None, d1, True, False, d4 // d1)
    state = _dl_call(dl4, state, d4, False, False, d16 // d4)
    return _dl_call(dl16, state, d16, False, True, 1)


HG_TT = 512
LOG2E = float(np.log2(np.e))
N_SUB = HG_CHUNK // HG_SUB


def _split3(x):
    hi = _bf(x)
    r1 = x - hi.astype(F32)
    mid = _bf(r1)
    lo = _bf(r1 - mid.astype(F32))
    return hi, mid, lo


def _hg_chunk(q, k, cum, v, st_ref, bd_ref, ones_ref, fwd):
    C, S = HG_CHUNK, HG_SUB
    edge = cum[C - 1:C, :] if fwd else cum[0:1, :]
    vb = _bf(v)
    masks = _head_masks(F32)

    st = st_ref[...]
    o = _dot_nt(_bf(q * jnp.exp(cum)), _bf(st))

    att_blocks = []
    for i in range(N_SUB):
        lo, hi = i * S, (i + 1) * S
        if fwd:
            if i == 0:
                att_blocks.append(None)
                continue
            ref = cum[lo - 1:lo, :]
            k_t = jnp.concatenate([k[0:lo, :] * jnp.exp(ref - cum[0:lo, :]), jnp.zeros((C - lo, BRANCH_W), F32)], axis=0)
        else:
            if i == N_SUB - 1:
                att_blocks.append(None)
                continue
            ref = cum[hi:hi + 1, :]
            k_t = jnp.concatenate([jnp.zeros((hi, BRANCH_W), F32), k[hi:C, :] * jnp.exp(ref - cum[hi:C, :])], axis=0)
        k_t = _bf(k_t)
        q_t = q[lo:hi, :] * jnp.exp(cum[lo:hi, :] - ref)
        q_bd = _bf(jnp.concatenate([q_t * masks[h] for h in range(N_HEADS)], axis=0))
        att = _dot_nt(q_bd, k_t)
        r = _dot(_bf(att), vb)
        att_blocks.append(_pick_heads([r[h * S:(h + 1) * S, :] for h in range(N_HEADS)]))

    ones_bd = ones_ref[...]
    half = S // 2
    half_rows = lax.broadcasted_iota(jnp.int32, (half, 1), 0)
    cum2 = cum * LOG2E
    o_blocks = []
    for i in range(N_SUB):
        lo, hi = i * S, (i + 1) * S
        cq, qq, kk_, vv = cum2[lo:hi, :], q[lo:hi, :], k[lo:hi, :], v[lo:hi, :]
        xs = []
        for s in range(S):
            parts = []
            for r0 in (0, half):
                if fwd:
                    everything, nothing = s <= r0, s > r0 + half - 1
                else:
                    everything, nothing = s >= r0 + half - 1, s < r0
                if nothing:
                    parts.append(jnp.zeros((half, BRANCH_W), F32))
                    continue
                d = cq[r0:r0 + half, :] - cq[s:s + 1, :]
                if not everything:
                    ok = (half_rows >= s - r0) if fwd else (half_rows <= s - r0)
                    d = jnp.where(ok, d, NEG)
                parts.append(qq[r0:r0 + half, :] * kk_[s:s + 1, :] * jnp.exp2(d))
            xs.append(_bf(jnp.concatenate(parts, axis=0)))
        z = _dot(jnp.concatenate(xs, axis=0), ones_bd)
        od_halves = [None, None]
        for s in range(S):
            for hidx, r0 in enumerate((0, half)):
                if (s > r0 + half - 1) if fwd else (s < r0):
                    continue
                term = z[s * S + r0:s * S + r0 + half, :] * vv[s:s + 1, :]
                od_halves[hidx] = term if od_halves[hidx] is None else od_halves[hidx] + term
        od = jnp.concatenate(od_halves, axis=0)
        if att_blocks[i] is not None:
            od = od + att_blocks[i]
        o_blocks.append(od)
    o = o + jnp.concatenate(o_blocks, axis=0)

    k_dec = _bf(k * jnp.exp(edge - cum))
    st_ref[...] = st * jnp.exp(edge) + _dot_tn(vb, k_dec) * bd_ref[...]
    return o


def _hg_kernel(*refs, fwd):
    if fwd:
        f_ref, q_ref, i_ref, lb_ref, tri_ref, bd_ref, ones_ref, o_ref, st_ref, q_scr, k_scr, cum_scr = refs
    else:
        (f_ref, q_ref, i_ref, g_ref, of_ref, lb_ref, gn_ref, tri_ref, bd_ref, ones_ref, o_ref,
         st_ref, q_scr, k_scr, cum_scr, o_scr) = refs

    @pl.when(pl.program_id(1) == 0)
    def _():
        st_ref[...] = jnp.zeros_like(st_ref)

    nchunk = HG_TT // HG_CHUNK

    x = f_ref[0]
    lb = lb_ref[...]
    e = jnp.exp(-jnp.abs(x))
    inv = 1.0 / (1.0 + e)
    pos = x >= 0
    sig_p = jnp.where(pos, inv, e * inv)
    sig_n = jnp.where(pos, e * inv, inv)
    k_scr[...] = (1.0 - lb) * sig_n
    q_pre = q_ref[0]
    q_scr[...] = q_pre * _sigmoid(q_pre)
    parts = _split3(jnp.log(lb + (1.0 - lb) * sig_p))
    tri = tri_ref[...]
    for c in range(nchunk):
        rows = slice(c * HG_CHUNK, (c + 1) * HG_CHUNK)
        cum_scr[rows, :] = sum(_dot(tri, part[rows, :]) for part in parts)

    def chunk_body(ci, carry):
        c = ci if fwd else nchunk - 1 - ci
        rows = pl.ds(pl.multiple_of(c * HG_CHUNK, HG_CHUNK), HG_CHUNK)
        o = _hg_chunk(q_scr[rows, :], k_scr[rows, :], cum_scr[rows, :], i_ref[0, rows, :],
                      st_ref, bd_ref, ones_ref, fwd)
        if fwd:
            o_ref[0, rows, :] = o
        else:
            o_scr[rows, :] = o
        return carry

    lax.fori_loop(0, nchunk, chunk_body, 0, unroll=4)

    if not fwd:
        o = o_scr[...] + of_ref[0]
        sq = o * o
        hi = _bf(sq)
        lo = _bf(sq - hi.astype(F32))
        ms = (_dot(hi, ones_ref[...]) + _dot(lo, ones_ref[...])) * (1.0 / HEAD_DIM)
        g = g_ref[0]
        o_ref[0] = _bf(o * lax.rsqrt(ms + EPS) * gn_ref[...] * (g * _sigmoid(g)))


def _hg_call(hg, o_fwd, lb, gnorm, bd, ones_bd, fwd):
    B, L, _ = hg.shape
    nt = L // HG_TT
    tmap = (lambda t: t) if fwd else (lambda t: nt - 1 - t)
    col = lambda c: pl.BlockSpec((1, HG_TT, BRANCH_W), lambda b, t: (b, tmap(t), c))
    idx = np.arange(HG_CHUNK)
    tri = jnp.asarray((idx[None, :] <= idx[:, None]) if fwd else (idx[None, :] >= idx[:, None]), BF16)
    consts = [_const_spec((HG_CHUNK, HG_CHUNK)), _const_spec((BRANCH_W, BRANCH_W)), _const_spec((BRANCH_W, BRANCH_W))]
    vec = _const_spec((1, BRANCH_W))
    if fwd:
        in_specs = [col(0), col(2), col(3), vec] + consts
        args = [hg, hg, hg, lb, tri, bd, ones_bd]
        out_dtype = F32
    else:
        in_specs = [col(1), col(2), col(3), col(4), col(0), vec, vec] + consts
        args = [hg, hg, hg, hg, o_fwd, lb, gnorm, tri, bd, ones_bd]
        out_dtype = BF16
    return pl.pallas_call(
        functools.partial(_hg_kernel, fwd=fwd),
        out_shape=jax.ShapeDtypeStruct((B, L, BRANCH_W), out_dtype),
        grid=(B, nt),
        in_specs=in_specs,
        out_specs=col(0),
        scratch_shapes=[pltpu.VMEM((BRANCH_W, BRANCH_W), F32)]
        + [pltpu.VMEM((HG_TT, BRANCH_W), F32) for _ in range(3 if fwd else 4)],
        compiler_params=_params("parallel", "arbitrary"),
        name="hg_fwd" if fwd else "hg_bwd",
    )(*args)


MERGE_TM = 512


def _merge_kernel(x_ref, mod_ref, g_ref, na_ref, hg_ref, sg_ref, dl_ref, perm_ref, wg_ref, bg_ref, wb_ref, wo_ref, o_ref):
    x = x_ref[0]
    h = _bf(_mod_norm(x, g_ref[...], mod_ref[0, 1:2, :], mod_ref[0, 0:1, :]))
    dl_classes = jnp.concatenate([dl_ref[0, r] for r in range(dl_ref.shape[1])], axis=0)
    dl = _bf(_dot(perm_ref[...], dl_classes))
    merged = jnp.zeros((MERGE_TM, D_MODEL), F32)
    for bi, br in enumerate((na_ref[0], hg_ref[0], sg_ref[0], dl)):
        gate = _sigmoid(_dot(h, wg_ref[bi]) + bg_ref[bi:bi + 1, :])
        merged = merged + gate * _dot(br, wb_ref[bi])
    y = _dot(_bf(merged), wo_ref[...])
    o_ref[0] = x + mod_ref[0, 2:3, :] * y


def _merge_call(x, mod, g, branches, w_gate, b_gate, w_branch, w_o):
    B, L, _ = x.shape
    tm = MERGE_TM
    d16 = DIL_CONFIGS[2][1]
    tok_idx = np.arange(tm)
    perm = np.zeros((tm, tm), np.float32)
    perm[tok_idx, (tok_idx % d16) * (tm // d16) + tok_idx // d16] = 1.0
    tok = lambda w: pl.BlockSpec((1, tm, w), lambda b, t: (b, t, 0))
    return pl.pallas_call(
        _merge_kernel,
        out_shape=jax.ShapeDtypeStruct((B, L, D_MODEL), F32),
        grid=(B, L // tm),
        in_specs=[tok(D_MODEL),
                  pl.BlockSpec((1, 6, D_MODEL), lambda b, t: (b, 0, 0)),
                  _const_spec((1, D_MODEL)),
                  tok(BRANCH_W), tok(BRANCH_W), tok(BRANCH_W),
                  pl.BlockSpec((1, d16, tm // d16, BRANCH_W), lambda b, t: (b, 0, t, 0)),
                  _const_spec((tm, tm)),
                  _const_spec((N_BRANCH, D_MODEL, D_MODEL)),
                  _const_spec((N_BRANCH, D_MODEL)),
                  _const_spec((N_BRANCH, BRANCH_W, D_MODEL)),
                  _const_spec((D_MODEL, D_MODEL))],
        out_specs=tok(D_MODEL),
        compiler_params=_params("parallel", "arbitrary"),
        name="merge",
    )(x, mod, g, *branches, jnp.asarray(perm, BF16), w_gate, b_gate, w_branch, w_o)


FFN_TM = 512
FFN_SPLIT = 2


def _ffn_kernel(x_ref, mod_ref, g_ref, wg_ref, wu_ref, wd_ref, gf_ref, o_ref, *, final):
    x = x_ref[0]
    h = _bf(_mod_norm(x, g_ref[...], mod_ref[0, 4:5, :], mod_ref[0, 3:4, :]))
    y = jnp.zeros((FFN_TM, D_MODEL), F32)
    fc = D_FF // FFN_SPLIT
    for c in range(FFN_SPLIT):
        a = _dot(h, wg_ref[:, c * fc:(c + 1) * fc])
        b = _dot(h, wu_ref[:, c * fc:(c + 1) * fc])
        y = y + _dot(_bf(a * _sigmoid(a) * b), wd_ref[c * fc:(c + 1) * fc, :])
    x = x + mod_ref[0, 5:6, :] * y
    if final:
        x = x * lax.rsqrt(jnp.mean(x * x, axis=-1, keepdims=True) + EPS) * gf_ref[...]
    o_ref[0] = x


def _ffn_call(x, mod, g, wg, wu, wd, g_final, final):
    B, L, _ = x.shape
    tm = FFN_TM
    tok = pl.BlockSpec((1, tm, D_MODEL), lambda b, t: (b, t, 0))
    return pl.pallas_call(
        functools.partial(_ffn_kernel, final=final),
        out_shape=jax.ShapeDtypeStruct((B, L, D_MODEL), F32),
        grid=(B, L // tm),
        in_specs=[tok,
                  pl.BlockSpec((1, 6, D_MODEL), lambda b, t: (b, 0, 0)),
                  _const_spec((1, D_MODEL)),
                  _const_spec((D_MODEL, D_FF)), _const_spec((D_MODEL, D_FF)), _const_spec((D_FF, D_MODEL)),
                  _const_spec((1, D_MODEL))],
        out_specs=tok,
        compiler_params=_params("parallel", "arbitrary"),
        name="ffn",
    )(x, mod, g, wg, wu, wd, g_final)


def _trunk(x, mods, p):
    for l in range(DEPTH):
        mod = mods[l]
        na_qkv, hg, y_sg, dl1, dl4, dl16 = _proj_call(x, mod, p["g_mix"][l], p["w_in"][l], p["sg_ln_g"][l],
                                                      p["sg_ln_b"][l], p["sg_w"][l], p["sg_b"][l])
        y_na = _na_call(na_qkv, p["na_bias"][l])
        o_f = _hg_call(hg, None, p["lb"][0][l], None, p["bd"], p["ones_bd"], True)
        y_hg = _hg_call(hg, o_f, p["lb"][1][l], p["hg_gnorm"][l], p["bd"], p["ones_bd"], False)
        y_dl = _dilated_attention(dl1, dl4, dl16)
        x = _merge_call(x, mod, p["g_mix"][l], (y_na, y_hg, y_sg, y_dl),
                        p["w_gate"][l], p["b_gate"][l], p["w_branch"][l], p["w_o"][l])
        x = _ffn_call(x, mod, p["g_ffn"][l], p["w_ffn_gate"][l], p["w_ffn_up"][l], p["w_ffn_down"][l],
                      p["g_final"], l == DEPTH - 1)
    return x


def kernel(x_prompt, x_sample, c_prompt, c_sample, w_ada, b_ada, g_norm_mix, g_norm_ffn, w_in, w_gate, b_gate,
           w_branch, w_o, na_rpb, hg_lb, hg_gnorm, sg_ln_g, sg_ln_b, sg_w, sg_b, w_ffn_gate, w_ffn_up,
           w_ffn_down, g_final):
    nbp, nbs = c_prompt.shape[0], c_sample.shape[0]
    nb_pad = -(-(nbp + nbs) // 16) * 16
    c_all = jnp.concatenate([c_prompt, c_sample, jnp.zeros((nb_pad - nbp - nbs, D_MODEL), F32)], axis=0)
    mod_all = _ada_call(c_all, w_ada, b_ada).reshape(DEPTH, nb_pad, 6, D_MODEL)
    mods_p = [mod_all[l, :nbp] for l in range(DEPTH)]
    mods_s = [mod_all[l, nbp:nbp + nbs] for l in range(DEPTH)]

    lb_soft = jax.nn.softmax(hg_lb.astype(F32), axis=1)
    lower = jnp.cumsum(lb_soft, axis=1) - lb_soft[:, :1]
    head = np.arange(BRANCH_W) // HEAD_DIM
    bd = jnp.asarray((head[:, None] == head[None, :]).astype(np.float32))
    vec = lambda a: [a[l].reshape(1, -1).astype(F32) for l in range(DEPTH)]
    p = {
        "g_mix": vec(g_norm_mix), "g_ffn": vec(g_norm_ffn),
        "w_in": [_bf(w_in[l]) for l in range(DEPTH)],
        "w_gate": [_bf(w_gate[l]) for l in range(DEPTH)],
        "b_gate": [b_gate[l] for l in range(DEPTH)],
        "w_branch": [_bf(w_branch[l]) for l in range(DEPTH)],
        "w_o": [_bf(w_o[l]) for l in range(DEPTH)],
        "na_bias": [_na_bias_table(na_rpb[l]) for l in range(DEPTH)],
        "lb": [[lower[d, l].reshape(1, BRANCH_W) for l in range(DEPTH)] for d in range(2)],
        "hg_gnorm": vec(hg_gnorm), "sg_ln_g": vec(sg_ln_g), "sg_ln_b": vec(sg_ln_b),
        "sg_w": [_bf(sg_w[l].transpose(1, 0, 2).reshape(SG_CHUNK, SG_GROUPS * SG_CHUNK)) for l in range(DEPTH)],
        "sg_b": [jnp.repeat(sg_b[l].T.astype(F32), BRANCH_W // SG_GROUPS, axis=1) for l in range(DEPTH)],
        "w_ffn_gate": [_bf(w_ffn_gate[l]) for l in range(DEPTH)],
        "w_ffn_up": [_bf(w_ffn_up[l]) for l in range(DEPTH)],
        "w_ffn_down": [_bf(w_ffn_down[l]) for l in range(DEPTH)],
        "g_final": g_final.reshape(1, D_MODEL).astype(F32),
        "bd": bd, "ones_bd": _bf(bd),
    }
    return (_trunk(x_prompt, mods_p, p), _trunk(x_sample, mods_s, p))
```

```python
import functools

import jax
import jax.numpy as jnp
import numpy as np
from jax import lax
from jax.experimental import pallas as pl
from jax.experimental.pallas import tpu as pltpu

D_MODEL = 1024
DEPTH = 4
GRID_W = 64
BRANCH_W = 256
N_BRANCH = 4
HEAD_DIM = 64
N_HEADS = BRANCH_W // HEAD_DIM
ATTN_SCALE = HEAD_DIM ** -0.5
NA_KH = 8
NA_KW = 16
HG_CHUNK = 64
HG_SUB = 16
SG_GROUPS = 4
SG_CHUNK = 128
DIL_CONFIGS = ((128, 1), (512, 4), (2048, 16))
DIL_BLK = 64
D_FF = 2816
IN_WIDTH = 3328
NEG = -1e30
EPS = 1e-6

NA_ROWS_PER_STEP = 8
NA_TOK = NA_ROWS_PER_STEP * GRID_W
NA_WIN = NA_KH * GRID_W

VMEM_LIMIT = 56 * 1024 * 1024

BF16 = jnp.bfloat16
F32 = jnp.float32


def _bf(x):
    return x.astype(BF16)


def _dot(a, b):
    return jnp.dot(a, b, preferred_element_type=F32)


def _dot_nt(a, b):
    return lax.dot_general(a, b, (((1,), (1,)), ((), ())), preferred_element_type=F32)


def _dot_tn(a, b):
    return lax.dot_general(a, b, (((0,), (0,)), ((), ())), preferred_element_type=F32)


def _sigmoid(x):
    return 1.0 / (1.0 + jnp.exp(-x))


def _head_masks(dtype):
    lane_head = lax.broadcasted_iota(jnp.int32, (1, BRANCH_W), 1) // HEAD_DIM
    return [(lane_head == h).astype(dtype) for h in range(N_HEADS)]


def _pick_heads(blocks):
    lane_head = lax.broadcasted_iota(jnp.int32, (1, BRANCH_W), 1) // HEAD_DIM
    out = blocks[N_HEADS - 1]
    for h in range(N_HEADS - 2, -1, -1):
        out = jnp.where(lane_head <= h, blocks[h], out)
    return out


def _mod_norm(x, g, scale, shift):
    y = x * lax.rsqrt(jnp.mean(x * x, axis=-1, keepdims=True) + EPS)
    return (y * g) * (1.0 + scale) + shift


def _params(*sem):
    return pltpu.CompilerParams(dimension_semantics=sem, vmem_limit_bytes=VMEM_LIMIT)


def _const_spec(shape):
    nd = len(shape)
    return pl.BlockSpec(shape, lambda *_: (0,) * nd, pipeline_mode=pl.Buffered(1))


def _ada_kernel(c_ref, w_ref, b_ref, o_ref):
    c = c_ref[...]
    cs = c * _sigmoid(c)
    o_ref[0] = _dot(_bf(cs), _bf(w_ref[0])) + b_ref[0]


def _ada_call(c_all, w_ada, b_ada):
    nb = c_all.shape[0]
    tn = 1536
    return pl.pallas_call(
        _ada_kernel,
        out_shape=jax.ShapeDtypeStruct((DEPTH, nb, 6 * D_MODEL), F32),
        grid=(DEPTH, 6 * D_MODEL // tn),
        in_specs=[pl.BlockSpec((nb, D_MODEL), lambda l, n: (0, 0)),
                  pl.BlockSpec((1, D_MODEL, tn), lambda l, n: (l, 0, n)),
                  pl.BlockSpec((1, 1, tn), lambda l, n: (l, 0, n))],
        out_specs=pl.BlockSpec((1, nb, tn), lambda l, n: (l, 0, n)),
        compiler_params=_params("arbitrary", "arbitrary"),
        name="ada",
    )(c_all, w_ada, b_ada.reshape(DEPTH, 1, 6 * D_MODEL))


PROJ_TM = 512


LANES = 128


def _store_lane_chunks(scr_ref, val):
    for c in range(val.shape[1] // LANES):
        scr_ref[c] = val[:, c * LANES:(c + 1) * LANES]


def _load_class(scr_ref, r, n, dil):
    return jnp.concatenate([scr_ref[c, pl.ds(r, n, stride=dil), :] for c in range(scr_ref.shape[0])], axis=1)


def _gelu_tanh(x):
    return 0.5 * x * (1.0 + jnp.tanh(np.sqrt(2.0 / np.pi).astype(np.float32) * (x + 0.044715 * (x * x * x))))


def _spatial_gating(uv, ln_g, ln_b, w, b):
    lane_group = lax.broadcasted_iota(jnp.int32, (1, BRANCH_W), 1) // (BRANCH_W // SG_GROUPS)
    u = _gelu_tanh(uv[:, 0:BRANCH_W])
    v = _gelu_tanh(uv[:, BRANCH_W:2 * BRANCH_W])
    mu = jnp.mean(v, axis=-1, keepdims=True)
    var = jnp.mean(jnp.square(v - mu), axis=-1, keepdims=True)
    v = (v - mu) * lax.rsqrt(var + EPS) * ln_g + ln_b
    vstack = jnp.concatenate([_bf(jnp.where(lane_group == g, v, 0.0)) for g in range(SG_GROUPS)], axis=0)
    return u * (_dot(w, vstack) + b)


def _proj_kernel(x_ref, mod_ref, g_ref, w_ref, lng_ref, lnb_ref, sgw_ref, sgb_ref,
                 na_ref, hg_ref, sg_ref, dl1_ref, dl4_ref, dl16_ref, dl_scr):
    h = _bf(_mod_norm(x_ref[0], g_ref[...], mod_ref[0, 1:2, :], mod_ref[0, 0:1, :]))
    uv = _dot(h, w_ref[:, 2048:2560])
    dl = _dot(h, w_ref[:, 2560:3328])
    for n in range(PROJ_TM // SG_CHUNK):
        rows = slice(n * SG_CHUNK, (n + 1) * SG_CHUNK)
        sg_ref[0, rows, :] = _bf(_spatial_gating(uv[rows, :], lng_ref[...], lnb_ref[...], sgw_ref[...], sgb_ref[...]))
    dl1_ref[0] = _bf(dl)
    _store_lane_chunks(dl_scr, dl)
    for dil, ref in ((DIL_CONFIGS[1][1], dl4_ref), (DIL_CONFIGS[2][1], dl16_ref)):
        for r in range(dil):
            ref[0, r] = _bf(_load_class(dl_scr, r, PROJ_TM // dil, dil))
    na_ref[0] = _bf(_dot(h, w_ref[:, 0:768]))
    hg_ref[0] = _dot(h, w_ref[:, 768:2048])


def _proj_call(x, mod, g, w_in, sg_ln_g, sg_ln_b, sg_w, sg_b):
    B, L, _ = x.shape
    tm = PROJ_TM
    d4, d16 = DIL_CONFIGS[1][1], DIL_CONFIGS[2][1]
    tok = lambda w: pl.BlockSpec((1, tm, w), lambda b, t: (b, t, 0))
    cls = lambda d: pl.BlockSpec((1, d, tm // d, 768), lambda b, t: (b, 0, t, 0))
    return pl.pallas_call(
        _proj_kernel,
        out_shape=(jax.ShapeDtypeStruct((B, L, 768), BF16),
                   jax.ShapeDtypeStruct((B, L, 1280), F32),
                   jax.ShapeDtypeStruct((B, L, BRANCH_W), BF16),
                   jax.ShapeDtypeStruct((B, L, 768), BF16),
                   jax.ShapeDtypeStruct((B, d4, L // d4, 768), BF16),
                   jax.ShapeDtypeStruct((B, d16, L // d16, 768), BF16)),
        grid=(B, L // tm),
        in_specs=[tok(D_MODEL),
                  pl.BlockSpec((1, 6, D_MODEL), lambda b, t: (b, 0, 0)),
                  _const_spec((1, D_MODEL)),
                  _const_spec((D_MODEL, IN_WIDTH)),
                  _const_spec((1, BRANCH_W)), _const_spec((1, BRANCH_W)),
                  _const_spec((SG_CHUNK, SG_GROUPS * SG_CHUNK)),
                  _const_spec((SG_CHUNK, BRANCH_W))],
        out_specs=(tok(768), tok(1280), tok(BRANCH_W), tok(768), cls(d4), cls(d16)),
        scratch_shapes=[pltpu.VMEM((768 // LANES, tm, LANES), F32)],
        compiler_params=_params("parallel", "arbitrary"),
        name="proj",
    )(x, mod, g, w_in, sg_ln_g, sg_ln_b, sg_w, sg_b)


def _na_bias_table(rpb):
    c = np.arange(GRID_W)
    kc = np.arange(GRID_W)
    col_off = np.clip(kc[None, :] - c[:, None], -(NA_KW - 1), NA_KW - 1) + NA_KW - 1
    win0 = np.clip(c - NA_KW // 2, 0, GRID_W - NA_KW)
    valid = (kc[None, :] >= win0[:, None]) & (kc[None, :] < win0[:, None] + NA_KW)
    onehot = (col_off[None] == np.arange(2 * NA_KW - 1)[:, None, None]).astype(np.float32)
    cols = jnp.einsum("hrj,jck->hrck", rpb.astype(F32), jnp.asarray(onehot), precision=lax.Precision.HIGHEST)
    cols = jnp.where(valid[None, None], cols, NEG)
    pats = jnp.stack([cols[:, NA_KH - 1 - d:2 * NA_KH - 1 - d] for d in range(NA_KH)])
    return pats.transpose(0, 1, 3, 2, 4).reshape(NA_KH, N_HEADS * GRID_W, NA_WIN)


def _na_kernel(q_ref, k_ref, v_ref, bias_ref, o_ref, *, rows):
    i = pl.program_id(1)
    qmasks = [m * ATTN_SCALE for m in _head_masks(BF16)]

    def row_body(j, carry):
        r = i * NA_ROWS_PER_STEP + j
        start = jnp.clip(r - NA_KH // 2, 0, rows - NA_KH)
        off = pl.multiple_of(start * GRID_W, GRID_W)
        delta = r - start
        qoff = pl.multiple_of(j * GRID_W, GRID_W)
        q = q_ref[0, pl.ds(qoff, GRID_W), :]
        kw = k_ref[0, pl.ds(off, NA_WIN), :]
        vw = v_ref[0, pl.ds(off, NA_WIN), :]
        q_heads = jnp.concatenate([q * qmasks[h] for h in range(N_HEADS)], axis=0)
        s = _dot_nt(q_heads, kw) + bias_ref[delta]
        m = jnp.max(s, axis=-1, keepdims=True)
        e = jnp.exp(s - m)
        den = jnp.sum(e, axis=-1, keepdims=True)
        r_all = _dot(_bf(e), vw) * (1.0 / den)
        out = _pick_heads([r_all[h * GRID_W:(h + 1) * GRID_W, :] for h in range(N_HEADS)])
        o_ref[0, pl.ds(qoff, GRID_W), :] = _bf(out)
        return carry

    lax.fori_loop(0, NA_ROWS_PER_STEP, row_body, 0, unroll=True)


def _na_call(na_qkv, bias_tab):
    B, L, _ = na_qkv.shape
    rows = L // GRID_W
    seq = lambda col: pl.BlockSpec((1, L, BRANCH_W), lambda b, i: (b, 0, col))
    return pl.pallas_call(
        functools.partial(_na_kernel, rows=rows),
        out_shape=jax.ShapeDtypeStruct((B, L, BRANCH_W), BF16),
        grid=(B, rows // NA_ROWS_PER_STEP),
        in_specs=[pl.BlockSpec((1, NA_TOK, BRANCH_W), lambda b, i: (b, i, 0)), seq(1), seq(2),
                  _const_spec((NA_KH, N_HEADS * GRID_W, NA_WIN))],
        out_specs=pl.BlockSpec((1, NA_TOK, BRANCH_W), lambda b, i: (b, i, 0)),
        compiler_params=_params("parallel", "arbitrary"),
        name="na",
    )(na_qkv, na_qkv, na_qkv, bias_tab)


DL_TQ = 128
DL_TQB = 512


def _alibi_slopes(n):
    start = 2.0 ** (-8.0 / n)
    return [float(np.float32(start ** (h + 1))) for h in range(n)]


def _dl_kernel(*refs, n, kw_len, tqb, cg, first, last, split):
    if first:
        q_ref, k_ref, v_ref, bias_ref = refs[:4]
        rest = refs[4:]
    else:
        q_ref, k_ref, v_ref, bias_ref, op_ref, lsep_ref = refs[:6]
        rest = refs[6:]
    t = pl.program_id(2)
    qmasks = [m * ATTN_SCALE for m in _head_masks(BF16)]
    nlane = BRANCH_W // LANES

    for c in range(cg):
        def tile_body(u, carry, c=c):
            base = t * tqb + u * DL_TQ
            ws = pl.multiple_of(jnp.clip(base - DIL_BLK, 0, n - kw_len), DIL_BLK)
            rows = pl.ds(pl.multiple_of(u * DL_TQ, DL_TQ), DL_TQ)
            q = q_ref[0, c, rows, :]
            kw = k_ref[0, c, pl.ds(ws, kw_len), :]
            vw = v_ref[0, c, pl.ds(ws, kw_len), :]
            q_heads = jnp.concatenate([q * qmasks[h] for h in range(N_HEADS)], axis=0)
            s = _dot_nt(q_heads, kw) + bias_ref[(base - ws) // DIL_BLK]
            m = jnp.max(s, axis=-1, keepdims=True)
            e = jnp.exp(s - m)
            den = jnp.sum(e, axis=-1, keepdims=True)
            r_all = _dot(_bf(e), vw) * (1.0 / den)
            lse_col = m + jnp.log(den)
            o = _pick_heads([r_all[h * DL_TQ:(h + 1) * DL_TQ, :] for h in range(N_HEADS)])
            lse = _pick_heads([lse_col[h * DL_TQ:(h + 1) * DL_TQ, :] for h in range(N_HEADS)])
            if not first:
                lse_p = lsep_ref[0, c, rows, :]
                lse_m = jnp.maximum(lse_p, lse)
                w_p = jnp.exp(lse_p - lse_m)
                w_c = jnp.exp(lse - lse_m)
                tot = w_p + w_c
                o = (w_p * op_ref[0, c, rows, :] + w_c * o) / tot
                lse = lse_m + jnp.log(tot)
            if last:
                rest[0][0, c, rows, :] = _bf(o)
            else:
                srows = pl.ds(pl.multiple_of(c * tqb + u * DL_TQ, DL_TQ), DL_TQ)
                for val, scr in ((o, rest[2]), (lse, rest[3])):
                    for lc in range(nlane):
                        scr[lc, srows, :] = val[:, lc * LANES:(lc + 1) * LANES]
            return carry

        lax.fori_loop(0, tqb // DL_TQ, tile_body, 0, unroll=4)

    if not last:
        for out_ref, scr in ((rest[0], rest[2]), (rest[1], rest[3])):
            for c in range(cg):
                for a in range(split):
                    out_ref[0, a, c] = jnp.concatenate(
                        [scr[lc, pl.ds(c * tqb + a, tqb // split, stride=split), :] for lc in range(nlane)], axis=1)


def _dl_bias_table(dil, kw_len):
    slopes = np.asarray(_alibi_slopes(N_HEADS), np.float32)
    qi = np.arange(DL_TQ)[None, :, None]
    ki = np.arange(kw_len)[None, None, :] - DIL_BLK * np.arange(3)[:, None, None]
    step = np.abs(ki - qi)
    dist = (step * dil).astype(np.float32)
    bias = np.where((step <= DIL_BLK)[:, None], -slopes[None, :, None, None] * dist[:, None], np.float32(NEG))
    return jnp.asarray(bias.reshape(3, N_HEADS * DL_TQ, kw_len), F32)


def _dl_call(qkv, state, dil, first, last, split):
    B, _, n, _ = qkv.shape
    kw_len = min(4 * DIL_BLK, n)
    tqb = min(DL_TQB, n)
    cg = min(dil, max(1, DL_TQB // tqb))
    tok = lambda c: pl.BlockSpec((1, cg, tqb, BRANCH_W), lambda b, r, t: (b, r, t, c))
    full = lambda c: pl.BlockSpec((1, cg, n, BRANCH_W), lambda b, r, t: (b, r, 0, c))
    in_specs = [tok(0), full(1), full(2), _const_spec((3, N_HEADS * DL_TQ, kw_len))]
    args = [qkv, qkv, qkv, _dl_bias_table(dil, kw_len)]
    if not first:
        in_specs += [tok(0), tok(0)]
        args += list(state)
    scratch = []
    if last:
        out_shape = jax.ShapeDtypeStruct((B, dil, n, BRANCH_W), BF16)
        out_specs = tok(0)
    else:
        shape = (B, split, dil, n // split, BRANCH_W)
        spec = lambda: pl.BlockSpec((1, split, cg, tqb // split, BRANCH_W), lambda b, r, t: (b, 0, r, t, 0))
        out_shape = (jax.ShapeDtypeStruct(shape, F32), jax.ShapeDtypeStruct(shape, F32))
        out_specs = (spec(), spec())
        scratch = [pltpu.VMEM((BRANCH_W // LANES, cg * tqb, LANES), F32) for _ in range(2)]
    out = pl.pallas_call(
        functools.partial(_dl_kernel, n=n, kw_len=kw_len, tqb=tqb, cg=cg, first=first, last=last, split=split),
        out_shape=out_shape,
        grid=(B, dil // cg, n // tqb),
        in_specs=in_specs,
        out_specs=out_specs,
        scratch_shapes=scratch,
        compiler_params=_params("parallel", "arbitrary", "arbitrary"),
        name=f"dl{dil}",
    )(*args)
    if last:
        return out
    return [o.reshape(B, split * dil, n // split, BRANCH_W) for o in out]


def _dilated_attention(dl1, dl4, dl16):
    (_, d1), (_, d4), (_, d16) = DIL_CONFIGS
    state = _dl_call(dl1[:, None], None, d1, True, False, d4 // d1)
    state = _dl_call(dl4, state, d4, False, False, d16 // d4)
    return _dl_call(dl16, state, d16, False, True, 1)


HG_TT = 512
LOG2E = float(np.log2(np.e))
N_SUB = HG_CHUNK // HG_SUB


def _split3(x):
    hi = _bf(x)
    r1 = x - hi.astype(F32)
    mid = _bf(r1)
    lo = _bf(r1 - mid.astype(F32))
    return hi, mid, lo


def _hg_chunk(q, k, cum, v, st_ref, bd_ref, ones_ref, fwd):
    C, S = HG_CHUNK, HG_SUB
    edge = cum[C - 1:C, :] if fwd else cum[0:1, :]
    vb = _bf(v)
    masks = _head_masks(F32)

    st = st_ref[...]
    o = _dot_nt(_bf(q * jnp.exp(cum)), _bf(st))

    att_blocks = []
    for i in range(N_SUB):
        lo, hi = i * S, (i + 1) * S
        if fwd:
            if i == 0:
                att_blocks.append(None)
                continue
            ref = cum[lo - 1:lo, :]
            k_t = jnp.concatenate([k[0:lo, :] * jnp.exp(ref - cum[0:lo, :]), jnp.zeros((C - lo, BRANCH_W), F32)], axis=0)
        else:
            if i == N_SUB - 1:
                att_blocks.append(None)
                continue
            ref = cum[hi:hi + 1, :]
            k_t = jnp.concatenate([jnp.zeros((hi, BRANCH_W), F32), k[hi:C, :] * jnp.exp(ref - cum[hi:C, :])], axis=0)
        k_t = _bf(k_t)
        q_t = q[lo:hi, :] * jnp.exp(cum[lo:hi, :] - ref)
        q_bd = _bf(jnp.concatenate([q_t * masks[h] for h in range(N_HEADS)], axis=0))
        att = _dot_nt(q_bd, k_t)
        r = _dot(_bf(att), vb)
        att_blocks.append(_pick_heads([r[h * S:(h + 1) * S, :] for h in range(N_HEADS)]))

    ones_bd = ones_ref[...]
    half = S // 2
    half_rows = lax.broadcasted_iota(jnp.int32, (half, 1), 0)
    cum2 = cum * LOG2E
    o_blocks = []
    for i in range(N_SUB):
        lo, hi = i * S, (i + 1) * S
        cq, qq, kk_, vv = cum2[lo:hi, :], q[lo:hi, :], k[lo:hi, :], v[lo:hi, :]
        xs = []
        for s in range(S):
            parts = []
            for r0 in (0, half):
                if fwd:
                    everything, nothing = s <= r0, s > r0 + half - 1
                else:
                    everything, nothing = s >= r0 + half - 1, s < r0
                if nothing:
                    parts.append(jnp.zeros((half, BRANCH_W), F32))
                    continue
                d = cq[r0:r0 + half, :] - cq[s:s + 1, :]
                if not everything:
                    ok = (half_rows >= s - r0) if fwd else (half_rows <= s - r0)
                    d = jnp.where(ok, d, NEG)
                parts.append(qq[r0:r0 + half, :] * kk_[s:s + 1, :] * jnp.exp2(d))
            xs.append(_bf(jnp.concatenate(parts, axis=0)))
        z = _dot(jnp.concatenate(xs, axis=0), ones_bd)
        od_halves = [None, None]
        for s in range(S):
            for hidx, r0 in enumerate((0, half)):
                if (s > r0 + half - 1) if fwd else (s < r0):
                    continue
                term = z[s * S + r0:s * S + r0 + half, :] * vv[s:s + 1, :]
                od_halves[hidx] = term if od_halves[hidx] is None else od_halves[hidx] + term
        od = jnp.concatenate(od_halves, axis=0)
        if att_blocks[i] is not None:
            od = od + att_blocks[i]
        o_blocks.append(od)
    o = o + jnp.concatenate(o_blocks, axis=0)

    k_dec = _bf(k * jnp.exp(edge - cum))
    st_ref[...] = st * jnp.exp(edge) + _dot_tn(vb, k_dec) * bd_ref[...]
    return o


def _hg_kernel(*refs, fwd):
    if fwd:
        f_ref, q_ref, i_ref, lb_ref, tri_ref, bd_ref, ones_ref, o_ref, st_ref, q_scr, k_scr, cum_scr = refs
    else:
        (f_ref, q_ref, i_ref, g_ref, of_ref, lb_ref, gn_ref, tri_ref, bd_ref, ones_ref, o_ref,
         st_ref, q_scr, k_scr, cum_scr, o_scr) = refs

    @pl.when(pl.program_id(1) == 0)
    def _():
        st_ref[...] = jnp.zeros_like(st_ref)

    nchunk = HG_TT // HG_CHUNK

    x = f_ref[0]
    lb = lb_ref[...]
    e = jnp.exp(-jnp.abs(x))
    inv = 1.0 / (1.0 + e)
    pos = x >= 0
    sig_p = jnp.where(pos, inv, e * inv)
    sig_n = jnp.where(pos, e * inv, inv)
    k_scr[...] = (1.0 - lb) * sig_n
    q_pre = q_ref[0]
    q_scr[...] = q_pre * _sigmoid(q_pre)
    parts = _split3(jnp.log(lb + (1.0 - lb) * sig_p))
    tri = tri_ref[...]
    for c in range(nchunk):
        rows = slice(c * HG_CHUNK, (c + 1) * HG_CHUNK)
        cum_scr[rows, :] = sum(_dot(tri, part[rows, :]) for part in parts)

    def chunk_body(ci, carry):
        c = ci if fwd else nchunk - 1 - ci
        rows = pl.ds(pl.multiple_of(c * HG_CHUNK, HG_CHUNK), HG_CHUNK)
        o = _hg_chunk(q_scr[rows, :], k_scr[rows, :], cum_scr[rows, :], i_ref[0, rows, :],
                      st_ref, bd_ref, ones_ref, fwd)
        if fwd:
            o_ref[0, rows, :] = o
        else:
            o_scr[rows, :] = o
        return carry

    lax.fori_loop(0, nchunk, chunk_body, 0, unroll=True)

    if not fwd:
        o = o_scr[...] + of_ref[0]
        sq = o * o
        hi = _bf(sq)
        lo = _bf(sq - hi.astype(F32))
        ms = (_dot(hi, ones_ref[...]) + _dot(lo, ones_ref[...])) * (1.0 / HEAD_DIM)
        g = g_ref[0]
        o_ref[0] = _bf(o * lax.rsqrt(ms + EPS) * gn_ref[...] * (g * _sigmoid(g)))


def _hg_call(hg, o_fwd, lb, gnorm, bd, ones_bd, fwd):
    B, L, _ = hg.shape
    nt = L // HG_TT
    tmap = (lambda t: t) if fwd else (lambda t: nt - 1 - t)
    col = lambda c: pl.BlockSpec((1, HG_TT, BRANCH_W), lambda b, t: (b, tmap(t), c))
    idx = np.arange(HG_CHUNK)
    tri = jnp.asarray((idx[None, :] <= idx[:, None]) if fwd else (idx[None, :] >= idx[:, None]), BF16)
    consts = [_const_spec((HG_CHUNK, HG_CHUNK)), _const_spec((BRANCH_W, BRANCH_W)), _const_spec((BRANCH_W, BRANCH_W))]
    vec = _const_spec((1, BRANCH_W))
    if fwd:
        in_specs = [col(0), col(2), col(3), vec] + consts
        args = [hg, hg, hg, lb, tri, bd, ones_bd]
        out_dtype = F32
    else:
        in_specs = [col(1), col(2), col(3), col(4), col(0), vec, vec] + consts
        args = [hg, hg, hg, hg, o_fwd, lb, gnorm, tri, bd, ones_bd]
        out_dtype = BF16
    return pl.pallas_call(
        functools.partial(_hg_kernel, fwd=fwd),
        out_shape=jax.ShapeDtypeStruct((B, L, BRANCH_W), out_dtype),
        grid=(B, nt),
        in_specs=in_specs,
        out_specs=col(0),
        scratch_shapes=[pltpu.VMEM((BRANCH_W, BRANCH_W), F32)]
        + [pltpu.VMEM((HG_TT, BRANCH_W), F32) for _ in range(3 if fwd else 4)],
        compiler_params=_params("parallel", "arbitrary"),
        name="hg_fwd" if fwd else "hg_bwd",
    )(*args)


MERGE_TM = 512


def _merge_kernel(x_ref, mod_ref, g_ref, na_ref, hg_ref, sg_ref, dl_ref, perm_ref, wg_ref, bg_ref, wb_ref, wo_ref, o_ref):
    x = x_ref[0]
    h = _bf(_mod_norm(x, g_ref[...], mod_ref[0, 1:2, :], mod_ref[0, 0:1, :]))
    dl_classes = jnp.concatenate([dl_ref[0, r] for r in range(dl_ref.shape[1])], axis=0)
    dl = _bf(_dot(perm_ref[...], dl_classes))
    merged = jnp.zeros((MERGE_TM, D_MODEL), F32)
    for bi, br in enumerate((na_ref[0], hg_ref[0], sg_ref[0], dl)):
        gate = _sigmoid(_dot(h, wg_ref[bi]) + bg_ref[bi:bi + 1, :])
        merged = merged + gate * _dot(br, wb_ref[bi])
    y = _dot(_bf(merged), wo_ref[...])
    o_ref[0] = x + mod_ref[0, 2:3, :] * y


def _merge_call(x, mod, g, branches, w_gate, b_gate, w_branch, w_o):
    B, L, _ = x.shape
    tm = MERGE_TM
    d16 = DIL_CONFIGS[2][1]
    tok_idx = np.arange(tm)
    perm = np.zeros((tm, tm), np.float32)
    perm[tok_idx, (tok_idx % d16) * (tm // d16) + tok_idx // d16] = 1.0
    tok = lambda w: pl.BlockSpec((1, tm, w), lambda b, t: (b, t, 0))
    return pl.pallas_call(
        _merge_kernel,
        out_shape=jax.ShapeDtypeStruct((B, L, D_MODEL), F32),
        grid=(B, L // tm),
        in_specs=[tok(D_MODEL),
                  pl.BlockSpec((1, 6, D_MODEL), lambda b, t: (b, 0, 0)),
                  _const_spec((1, D_MODEL)),
                  tok(BRANCH_W), tok(BRANCH_W), tok(BRANCH_W),
                  pl.BlockSpec((1, d16, tm // d16, BRANCH_W), lambda b, t: (b, 0, t, 0)),
                  _const_spec((tm, tm)),
                  _const_spec((N_BRANCH, D_MODEL, D_MODEL)),
                  _const_spec((N_BRANCH, D_MODEL)),
                  _const_spec((N_BRANCH, BRANCH_W, D_MODEL)),
                  _const_spec((D_MODEL, D_MODEL))],
        out_specs=tok(D_MODEL),
        compiler_params=_params("parallel", "arbitrary"),
        name="merge",
    )(x, mod, g, *branches, jnp.asarray(perm, BF16), w_gate, b_gate, w_branch, w_o)


FFN_TM = 512
FFN_SPLIT = 2


def _ffn_kernel(x_ref, mod_ref, g_ref, wg_ref, wu_ref, wd_ref, gf_ref, o_ref, *, final):
    x = x_ref[0]
    h = _bf(_mod_norm(x, g_ref[...], mod_ref[0, 4:5, :], mod_ref[0, 3:4, :]))
    y = jnp.zeros((FFN_TM, D_MODEL), F32)
    fc = D_FF // FFN_SPLIT
    for c in range(FFN_SPLIT):
        a = _dot(h, wg_ref[:, c * fc:(c + 1) * fc])
        b = _dot(h, wu_ref[:, c * fc:(c + 1) * fc])
        y = y + _dot(_bf(a * _sigmoid(a) * b), wd_ref[c * fc:(c + 1) * fc, :])
    x = x + mod_ref[0, 5:6, :] * y
    if final:
        x = x * lax.rsqrt(jnp.mean(x * x, axis=-1, keepdims=True) + EPS) * gf_ref[...]
    o_ref[0] = x


def _ffn_call(x, mod, g, wg, wu, wd, g_final, final):
    B, L, _ = x.shape
    tm = FFN_TM
    tok = pl.BlockSpec((1, tm, D_MODEL), lambda b, t: (b, t, 0))
    return pl.pallas_call(
        functools.partial(_ffn_kernel, final=final),
        out_shape=jax.ShapeDtypeStruct((B, L, D_MODEL), F32),
        grid=(B, L // tm),
        in_specs=[tok,
                  pl.BlockSpec((1, 6, D_MODEL), lambda b, t: (b, 0, 0)),
                  _const_spec((1, D_MODEL)),
                  _const_spec((D_MODEL, D_FF)), _const_spec((D_MODEL, D_FF)), _const_spec((D_FF, D_MODEL)),
                  _const_spec((1, D_MODEL))],
        out_specs=tok,
        compiler_params=_params("parallel", "arbitrary"),
        name="ffn",
    )(x, mod, g, wg, wu, wd, g_final)


def _trunk(x, mods, p):
    for l in range(DEPTH):
        mod = mods[l]
        na_qkv, hg, y_sg, dl1, dl4, dl16 = _proj_call(x, mod, p["g_mix"][l], p["w_in"][l], p["sg_ln_g"][l],
                                                      p["sg_ln_b"][l], p["sg_w"][l], p["sg_b"][l])
        y_na = _na_call(na_qkv, p["na_bias"][l])
        o_f = _hg_call(hg, None, p["lb"][0][l], None, p["bd"], p["ones_bd"], True)
        y_hg = _hg_call(hg, o_f, p["lb"][1][l], p["hg_gnorm"][l], p["bd"], p["ones_bd"], False)
        y_dl = _dilated_attention(dl1, dl4, dl16)
        x = _merge_call(x, mod, p["g_mix"][l], (y_na, y_hg, y_sg, y_dl),
                        p["w_gate"][l], p["b_gate"][l], p["w_branch"][l], p["w_o"][l])
        x = _ffn_call(x, mod, p["g_ffn"][l], p["w_ffn_gate"][l], p["w_ffn_up"][l], p["w_ffn_down"][l],
                      p["g_final"], l == DEPTH - 1)
    return x


def kernel(x_prompt, x_sample, c_prompt, c_sample, w_ada, b_ada, g_norm_mix, g_norm_ffn, w_in, w_gate, b_gate,
           w_branch, w_o, na_rpb, hg_lb, hg_gnorm, sg_ln_g, sg_ln_b, sg_w, sg_b, w_ffn_gate, w_ffn_up,
           w_ffn_down, g_final):
    nbp, nbs = c_prompt.shape[0], c_sample.shape[0]
    nb_pad = -(-(nbp + nbs) // 16) * 16
    c_all = jnp.concatenate([c_prompt, c_sample, jnp.zeros((nb_pad - nbp - nbs, D_MODEL), F32)], axis=0)
    mod_all = _ada_call(c_all, w_ada, b_ada).reshape(DEPTH, nb_pad, 6, D_MODEL)
    mods_p = [mod_all[l, :nbp] for l in range(DEPTH)]
    mods_s = [mod_all[l, nbp:nbp + nbs] for l in range(DEPTH)]

    lb_soft = jax.nn.softmax(hg_lb.astype(F32), axis=1)
    lower = jnp.cumsum(lb_soft, axis=1) - lb_soft[:, :1]
    head = np.arange(BRANCH_W) // HEAD_DIM
    bd = jnp.asarray((head[:, None] == head[None, :]).astype(np.float32))
    vec = lambda a: [a[l].reshape(1, -1).astype(F32) for l in range(DEPTH)]
    p = {
        "g_mix": vec(g_norm_mix), "g_ffn": vec(g_norm_ffn),
        "w_in": [_bf(w_in[l]) for l in range(DEPTH)],
        "w_gate": [_bf(w_gate[l]) for l in range(DEPTH)],
        "b_gate": [b_gate[l] for l in range(DEPTH)],
        "w_branch": [_bf(w_branch[l]) for l in range(DEPTH)],
        "w_o": [_bf(w_o[l]) for l in range(DEPTH)],
        "na_bias": [_na_bias_table(na_rpb[l]) for l in range(DEPTH)],
        "lb": [[lower[d, l].reshape(1, BRANCH_W) for l in range(DEPTH)] for d in range(2)],
        "hg_gnorm": vec(hg_gnorm), "sg_ln_g": vec(sg_ln_g), "sg_ln_b": vec(sg_ln_b),
        "sg_w": [_bf(sg_w[l].transpose(1, 0, 2).reshape(SG_CHUNK, SG_GROUPS * SG_CHUNK)) for l in range(DEPTH)],
        "sg_b": [jnp.repeat(sg_b[l].T.astype(F32), BRANCH_W // SG_GROUPS, axis=1) for l in range(DEPTH)],
        "w_ffn_gate": [_bf(w_ffn_gate[l]) for l in range(DEPTH)],
        "w_ffn_up": [_bf(w_ffn_up[l]) for l in range(DEPTH)],
        "w_ffn_down": [_bf(w_ffn_down[l]) for l in range(DEPTH)],
        "g_final": g_final.reshape(1, D_MODEL).astype(F32),
        "bd": bd, "ones_bd": _bf(bd),
    }
    return (_trunk(x_prompt, mods_p, p), _trunk(x_sample, mods_s, p))
```

```python
import functools

import jax
import jax.numpy as jnp
import numpy as np
from jax import lax
from jax.experimental import pallas as pl
from jax.experimental.pallas import tpu as pltpu

D_MODEL = 1024
DEPTH = 4
GRID_W = 64
BRANCH_W = 256
N_BRANCH = 4
HEAD_DIM = 64
N_HEADS = BRANCH_W // HEAD_DIM
ATTN_SCALE = HEAD_DIM ** -0.5
NA_KH = 8
NA_KW = 16
HG_CHUNK = 64
HG_SUB = 16
SG_GROUPS = 4
SG_CHUNK = 128
DIL_CONFIGS = ((128, 1), (512, 4), (2048, 16))
DIL_BLK = 64
D_FF = 2816
IN_WIDTH = 3328
NEG = -1e30
EPS = 1e-6

NA_ROWS_PER_STEP = 16
NA_TOK = NA_ROWS_PER_STEP * GRID_W
NA_WIN = NA_KH * GRID_W

VMEM_LIMIT = 56 * 1024 * 1024

BF16 = jnp.bfloat16
F32 = jnp.float32


def _bf(x):
    return x.astype(BF16)


def _dot(a, b):
    return jnp.dot(a, b, preferred_element_type=F32)


def _dot_nt(a, b):
    return lax.dot_general(a, b, (((1,), (1,)), ((), ())), preferred_element_type=F32)


def _dot_tn(a, b):
    return lax.dot_general(a, b, (((0,), (0,)), ((), ())), preferred_element_type=F32)


def _sigmoid(x):
    return 1.0 / (1.0 + jnp.exp(-x))


def _head_masks(dtype):
    lane_head = lax.broadcasted_iota(jnp.int32, (1, BRANCH_W), 1) // HEAD_DIM
    return [(lane_head == h).astype(dtype) for h in range(N_HEADS)]


def _pick_heads(blocks):
    lane_head = lax.broadcasted_iota(jnp.int32, (1, BRANCH_W), 1) // HEAD_DIM
    out = blocks[N_HEADS - 1]
    for h in range(N_HEADS - 2, -1, -1):
        out = jnp.where(lane_head <= h, blocks[h], out)
    return out


def _mod_norm(x, g, scale, shift):
    y = x * lax.rsqrt(jnp.mean(x * x, axis=-1, keepdims=True) + EPS)
    return (y * g) * (1.0 + scale) + shift


def _params(*sem):
    return pltpu.CompilerParams(dimension_semantics=sem, vmem_limit_bytes=VMEM_LIMIT)


def _const_spec(shape):
    nd = len(shape)
    return pl.BlockSpec(shape, lambda *_: (0,) * nd, pipeline_mode=pl.Buffered(1))


def _ada_kernel(c_ref, w_ref, b_ref, o_ref):
    c = c_ref[...]
    cs = c * _sigmoid(c)
    o_ref[0] = _dot(_bf(cs), _bf(w_ref[0])) + b_ref[0]


def _ada_call(c_all, w_ada, b_ada):
    nb = c_all.shape[0]
    tn = 1536
    return pl.pallas_call(
        _ada_kernel,
        out_shape=jax.ShapeDtypeStruct((DEPTH, nb, 6 * D_MODEL), F32),
        grid=(DEPTH, 6 * D_MODEL // tn),
        in_specs=[pl.BlockSpec((nb, D_MODEL), lambda l, n: (0, 0)),
                  pl.BlockSpec((1, D_MODEL, tn), lambda l, n: (l, 0, n)),
                  pl.BlockSpec((1, 1, tn), lambda l, n: (l, 0, n))],
        out_specs=pl.BlockSpec((1, nb, tn), lambda l, n: (l, 0, n)),
        compiler_params=_params("arbitrary", "arbitrary"),
        name="ada",
    )(c_all, w_ada, b_ada.reshape(DEPTH, 1, 6 * D_MODEL))


PROJ_TM = 512


LANES = 128


def _store_lane_chunks(scr_ref, val):
    for c in range(val.shape[1] // LANES):
        scr_ref[c] = val[:, c * LANES:(c + 1) * LANES]


def _load_class(scr_ref, r, n, dil):
    return jnp.concatenate([scr_ref[c, pl.ds(r, n, stride=dil), :] for c in range(scr_ref.shape[0])], axis=1)


def _gelu_tanh(x):
    return 0.5 * x * (1.0 + jnp.tanh(np.sqrt(2.0 / np.pi).astype(np.float32) * (x + 0.044715 * (x * x * x))))


def _spatial_gating(uv, ln_g, ln_b, w, b):
    lane_group = lax.broadcasted_iota(jnp.int32, (1, BRANCH_W), 1) // (BRANCH_W // SG_GROUPS)
    u = _gelu_tanh(uv[:, 0:BRANCH_W])
    v = _gelu_tanh(uv[:, BRANCH_W:2 * BRANCH_W])
    mu = jnp.mean(v, axis=-1, keepdims=True)
    var = jnp.mean(jnp.square(v - mu), axis=-1, keepdims=True)
    v = (v - mu) * lax.rsqrt(var + EPS) * ln_g + ln_b
    vstack = jnp.concatenate([_bf(jnp.where(lane_group == g, v, 0.0)) for g in range(SG_GROUPS)], axis=0)
    return u * (_dot(w, vstack) + b)


def _proj_kernel(x_ref, mod_ref, g_ref, w_ref, lng_ref, lnb_ref, sgw_ref, sgb_ref,
                 na_ref, hg_ref, sg_ref, dl1_ref, dl4_ref, dl16_ref, dl_scr):
    h = _bf(_mod_norm(x_ref[0], g_ref[...], mod_ref[0, 1:2, :], mod_ref[0, 0:1, :]))
    uv = _dot(h, w_ref[:, 2048:2560])
    dl = _dot(h, w_ref[:, 2560:3328])
    for n in range(PROJ_TM // SG_CHUNK):
        rows = slice(n * SG_CHUNK, (n + 1) * SG_CHUNK)
        sg_ref[0, rows, :] = _bf(_spatial_gating(uv[rows, :], lng_ref[...], lnb_ref[...], sgw_ref[...], sgb_ref[...]))
    dl1_ref[0] = _bf(dl)
    _store_lane_chunks(dl_scr, dl)
    for dil, ref in ((DIL_CONFIGS[1][1], dl4_ref), (DIL_CONFIGS[2][1], dl16_ref)):
        for r in range(dil):
            ref[0, r] = _bf(_load_class(dl_scr, r, PROJ_TM // dil, dil))
    na_ref[0] = _bf(_dot(h, w_ref[:, 0:768]))
    hg_ref[0] = _dot(h, w_ref[:, 768:2048])


def _proj_call(x, mod, g, w_in, sg_ln_g, sg_ln_b, sg_w, sg_b):
    B, L, _ = x.shape
    tm = PROJ_TM
    d4, d16 = DIL_CONFIGS[1][1], DIL_CONFIGS[2][1]
    tok = lambda w: pl.BlockSpec((1, tm, w), lambda b, t: (b, t, 0))
    cls = lambda d: pl.BlockSpec((1, d, tm // d, 768), lambda b, t: (b, 0, t, 0))
    return pl.pallas_call(
        _proj_kernel,
        out_shape=(jax.ShapeDtypeStruct((B, L, 768), BF16),
                   jax.ShapeDtypeStruct((B, L, 1280), F32),
                   jax.ShapeDtypeStruct((B, L, BRANCH_W), BF16),
                   jax.ShapeDtypeStruct((B, L, 768), BF16),
                   jax.ShapeDtypeStruct((B, d4, L // d4, 768), BF16),
                   jax.ShapeDtypeStruct((B, d16, L // d16, 768), BF16)),
        grid=(B, L // tm),
        in_specs=[tok(D_MODEL),
                  pl.BlockSpec((1, 6, D_MODEL), lambda b, t: (b, 0, 0)),
                  _const_spec((1, D_MODEL)),
                  _const_spec((D_MODEL, IN_WIDTH)),
                  _const_spec((1, BRANCH_W)), _const_spec((1, BRANCH_W)),
                  _const_spec((SG_CHUNK, SG_GROUPS * SG_CHUNK)),
                  _const_spec((SG_CHUNK, BRANCH_W))],
        out_specs=(tok(768), tok(1280), tok(BRANCH_W), tok(768), cls(d4), cls(d16)),
        scratch_shapes=[pltpu.VMEM((768 // LANES, tm, LANES), F32)],
        compiler_params=_params("parallel", "arbitrary"),
        name="proj",
    )(x, mod, g, w_in, sg_ln_g, sg_ln_b, sg_w, sg_b)


def _na_bias_table(rpb):
    c = np.arange(GRID_W)
    kc = np.arange(GRID_W)
    col_off = np.clip(kc[None, :] - c[:, None], -(NA_KW - 1), NA_KW - 1) + NA_KW - 1
    win0 = np.clip(c - NA_KW // 2, 0, GRID_W - NA_KW)
    valid = (kc[None, :] >= win0[:, None]) & (kc[None, :] < win0[:, None] + NA_KW)
    onehot = (col_off[None] == np.arange(2 * NA_KW - 1)[:, None, None]).astype(np.float32)
    cols = jnp.einsum("hrj,jck->hrck", rpb.astype(F32), jnp.asarray(onehot), precision=lax.Precision.HIGHEST)
    cols = jnp.where(valid[None, None], cols, NEG)
    pats = jnp.stack([cols[:, NA_KH - 1 - d:2 * NA_KH - 1 - d] for d in range(NA_KH)])
    return pats.transpose(0, 1, 3, 2, 4).reshape(NA_KH, N_HEADS * GRID_W, NA_WIN)


def _na_kernel(q_ref, k_ref, v_ref, bias_ref, o_ref, *, rows):
    i = pl.program_id(1)
    qmasks = [m * ATTN_SCALE for m in _head_masks(BF16)]

    def row_body(j, carry):
        r = i * NA_ROWS_PER_STEP + j
        start = jnp.clip(r - NA_KH // 2, 0, rows - NA_KH)
        off = pl.multiple_of(start * GRID_W, GRID_W)
        delta = r - start
        qoff = pl.multiple_of(j * GRID_W, GRID_W)
        q = q_ref[0, pl.ds(qoff, GRID_W), :]
        kw = k_ref[0, pl.ds(off, NA_WIN), :]
        vw = v_ref[0, pl.ds(off, NA_WIN), :]
        q_heads = jnp.concatenate([q * qmasks[h] for h in range(N_HEADS)], axis=0)
        s = _dot_nt(q_heads, kw) + bias_ref[delta]
        m = jnp.max(s, axis=-1, keepdims=True)
        e = jnp.exp(s - m)
        den = jnp.sum(e, axis=-1, keepdims=True)
        r_all = _dot(_bf(e), vw) * (1.0 / den)
        out = _pick_heads([r_all[h * GRID_W:(h + 1) * GRID_W, :] for h in range(N_HEADS)])
        o_ref[0, pl.ds(qoff, GRID_W), :] = _bf(out)
        return carry

    lax.fori_loop(0, NA_ROWS_PER_STEP, row_body, 0, unroll=True)


def _na_call(na_qkv, bias_tab):
    B, L, _ = na_qkv.shape
    rows = L // GRID_W
    seq = lambda col: pl.BlockSpec((1, L, BRANCH_W), lambda b, i: (b, 0, col))
    return pl.pallas_call(
        functools.partial(_na_kernel, rows=rows),
        out_shape=jax.ShapeDtypeStruct((B, L, BRANCH_W), BF16),
        grid=(B, rows // NA_ROWS_PER_STEP),
        in_specs=[pl.BlockSpec((1, NA_TOK, BRANCH_W), lambda b, i: (b, i, 0)), seq(1), seq(2),
                  _const_spec((NA_KH, N_HEADS * GRID_W, NA_WIN))],
        out_specs=pl.BlockSpec((1, NA_TOK, BRANCH_W), lambda b, i: (b, i, 0)),
        compiler_params=_params("parallel", "arbitrary"),
        name="na",
    )(na_qkv, na_qkv, na_qkv, bias_tab)


DL_TQ = 128
DL_TQB = 1024


def _alibi_slopes(n):
    start = 2.0 ** (-8.0 / n)
    return [float(np.float32(start ** (h + 1))) for h in range(n)]


def _dl_kernel(*refs, n, kw_len, tqb, cg, first, last, split):
    if first:
        q_ref, k_ref, v_ref, bias_ref = refs[:4]
        rest = refs[4:]
    else:
        q_ref, k_ref, v_ref, bias_ref, op_ref, lsep_ref = refs[:6]
        rest = refs[6:]
    t = pl.program_id(2)
    qmasks = [m * ATTN_SCALE for m in _head_masks(BF16)]
    nlane = BRANCH_W // LANES

    for c in range(cg):
        def tile_body(u, carry, c=c):
            base = t * tqb + u * DL_TQ
            ws = pl.multiple_of(jnp.clip(base - DIL_BLK, 0, n - kw_len), DIL_BLK)
            rows = pl.ds(pl.multiple_of(u * DL_TQ, DL_TQ), DL_TQ)
            q = q_ref[0, c, rows, :]
            kw = k_ref[0, c, pl.ds(ws, kw_len), :]
            vw = v_ref[0, c, pl.ds(ws, kw_len), :]
            q_heads = jnp.concatenate([q * qmasks[h] for h in range(N_HEADS)], axis=0)
            s = _dot_nt(q_heads, kw) + bias_ref[(base - ws) // DIL_BLK]
            m = jnp.max(s, axis=-1, keepdims=True)
            e = jnp.exp(s - m)
            den = jnp.sum(e, axis=-1, keepdims=True)
            r_all = _dot(_bf(e), vw) * (1.0 / den)
            lse_col = m + jnp.log(den)
            o = _pick_heads([r_all[h * DL_TQ:(h + 1) * DL_TQ, :] for h in range(N_HEADS)])
            lse = _pick_heads([lse_col[h * DL_TQ:(h + 1) * DL_TQ, :] for h in range(N_HEADS)])
            if not first:
                lse_p = lsep_ref[0, c, rows, :]
                lse_m = jnp.maximum(lse_p, lse)
                w_p = jnp.exp(lse_p - lse_m)
                w_c = jnp.exp(lse - lse_m)
                tot = w_p + w_c
                o = (w_p * op_ref[0, c, rows, :] + w_c * o) / tot
                lse = lse_m + jnp.log(tot)
            if last:
                rest[0][0, c, rows, :] = _bf(o)
            else:
                srows = pl.ds(pl.multiple_of(c * tqb + u * DL_TQ, DL_TQ), DL_TQ)
                for val, scr in ((o, rest[2]), (lse, rest[3])):
                    for lc in range(nlane):
                        scr[lc, srows, :] = val[:, lc * LANES:(lc + 1) * LANES]
            return carry

        lax.fori_loop(0, tqb // DL_TQ, tile_body, 0, unroll=True)

    if not last:
        for out_ref, scr in ((rest[0], rest[2]), (rest[1], rest[3])):
            for c in range(cg):
                for a in range(split):
                    out_ref[0, a, c] = jnp.concatenate(
                        [scr[lc, pl.ds(c * tqb + a, tqb // split, stride=split), :] for lc in range(nlane)], axis=1)


def _dl_bias_table(dil, kw_len):
    slopes = np.asarray(_alibi_slopes(N_HEADS), np.float32)
    qi = np.arange(DL_TQ)[None, :, None]
    ki = np.arange(kw_len)[None, None, :] - DIL_BLK * np.arange(3)[:, None, None]
    step = np.abs(ki - qi)
    dist = (step * dil).astype(np.float32)
    bias = np.where((step <= DIL_BLK)[:, None], -slopes[None, :, None, None] * dist[:, None], np.float32(NEG))
    return jnp.asarray(bias.reshape(3, N_HEADS * DL_TQ, kw_len), F32)


def _dl_call(qkv, state, dil, first, last, split):
    B, _, n, _ = qkv.shape
    kw_len = min(4 * DIL_BLK, n)
    tqb = min(DL_TQB, n)
    cg = min(dil, max(1, DL_TQB // tqb))
    tok = lambda c: pl.BlockSpec((1, cg, tqb, BRANCH_W), lambda b, r, t: (b, r, t, c))
    full = lambda c: pl.BlockSpec((1, cg, n, BRANCH_W), lambda b, r, t: (b, r, 0, c))
    in_specs = [tok(0), full(1), full(2), _const_spec((3, N_HEADS * DL_TQ, kw_len))]
    args = [qkv, qkv, qkv, _dl_bias_table(dil, kw_len)]
    if not first:
        in_specs += [tok(0), tok(0)]
        args += list(state)
    scratch = []
    if last:
        out_shape = jax.ShapeDtypeStruct((B, dil, n, BRANCH_W), BF16)
        out_specs = tok(0)
    else:
        shape = (B, split, dil, n // split, BRANCH_W)
        spec = lambda: pl.BlockSpec((1, split, cg, tqb // split, BRANCH_W), lambda b, r, t: (b, 0, r, t, 0))
        out_shape = (jax.ShapeDtypeStruct(shape, F32), jax.ShapeDtypeStruct(shape, F32))
        out_specs = (spec(), spec())
        scratch = [pltpu.VMEM((BRANCH_W // LANES, cg * tqb, LANES), F32) for _ in range(2)]
    out = pl.pallas_call(
        functools.partial(_dl_kernel, n=n, kw_len=kw_len, tqb=tqb, cg=cg, first=first, last=last, split=split),
        out_shape=out_shape,
        grid=(B, dil // cg, n // tqb),
        in_specs=in_specs,
        out_specs=out_specs,
        scratch_shapes=scratch,
        compiler_params=_params("parallel", "arbitrary", "arbitrary"),
        name=f"dl{dil}",
    )(*args)
    if last:
        return out
    return [o.reshape(B, split * dil, n // split, BRANCH_W) for o in out]


def _dilated_attention(dl1, dl4, dl16):
    (_, d1), (_, d4), (_, d16) = DIL_CONFIGS
    state = _dl_call(dl1[:, None], None, d1, True, False, d4 // d1)
    state = _dl_call(dl4, state, d4, False, False, d16 // d4)
    return _dl_call(dl16, state, d16, False, True, 1)


HG_TT = 512
LOG2E = float(np.log2(np.e))
N_SUB = HG_CHUNK // HG_SUB


def _split3(x):
    hi = _bf(x)
    r1 = x - hi.astype(F32)
    mid = _bf(r1)
    lo = _bf(r1 - mid.astype(F32))
    return hi, mid, lo


def _hg_chunk(q, k, cum, v, st_ref, bd_ref, ones_ref, fwd):
    C, S = HG_CHUNK, HG_SUB
    edge = cum[C - 1:C, :] if fwd else cum[0:1, :]
    vb = _bf(v)
    masks = _head_masks(F32)

    st = st_ref[...]
    o = _dot_nt(_bf(q * jnp.exp(cum)), _bf(st))

    att_blocks = []
    for i in range(N_SUB):
        lo, hi = i * S, (i + 1) * S
        if fwd:
            if i == 0:
                att_blocks.append(None)
                continue
            ref = cum[lo - 1:lo, :]
            k_t = jnp.concatenate([k[0:lo, :] * jnp.exp(ref - cum[0:lo, :]), jnp.zeros((C - lo, BRANCH_W), F32)], axis=0)
        else:
            if i == N_SUB - 1:
                att_blocks.append(None)
                continue
            ref = cum[hi:hi + 1, :]
            k_t = jnp.concatenate([jnp.zeros((hi, BRANCH_W), F32), k[hi:C, :] * jnp.exp(ref - cum[hi:C, :])], axis=0)
        k_t = _bf(k_t)
        q_t = q[lo:hi, :] * jnp.exp(cum[lo:hi, :] - ref)
        q_bd = _bf(jnp.concatenate([q_t * masks[h] for h in range(N_HEADS)], axis=0))
        att = _dot_nt(q_bd, k_t)
        r = _dot(_bf(att), vb)
        att_blocks.append(_pick_heads([r[h * S:(h + 1) * S, :] for h in range(N_HEADS)]))

    ones_bd = ones_ref[...]
    half = S // 2
    half_rows = lax.broadcasted_iota(jnp.int32, (half, 1), 0)
    cum2 = cum * LOG2E
    o_blocks = []
    for i in range(N_SUB):
        lo, hi = i * S, (i + 1) * S
        cq, qq, kk_, vv = cum2[lo:hi, :], q[lo:hi, :], k[lo:hi, :], v[lo:hi, :]
        xs = []
        for s in range(S):
            parts = []
            for r0 in (0, half):
                if fwd:
                    everything, nothing = s <= r0, s > r0 + half - 1
                else:
                    everything, nothing = s >= r0 + half - 1, s < r0
                if nothing:
                    parts.append(jnp.zeros((half, BRANCH_W), F32))
                    continue
                d = cq[r0:r0 + half, :] - cq[s:s + 1, :]
                if not everything:
                    ok = (half_rows >= s - r0) if fwd else (half_rows <= s - r0)
                    d = jnp.where(ok, d, NEG)
                parts.append(qq[r0:r0 + half, :] * kk_[s:s + 1, :] * jnp.exp2(d))
            xs.append(_bf(jnp.concatenate(parts, axis=0)))
        z = _dot(jnp.concatenate(xs, axis=0), ones_bd)
        od_halves = [None, None]
        for s in range(S):
            for hidx, r0 in enumerate((0, half)):
                if (s > r0 + half - 1) if fwd else (s < r0):
                    continue
                term = z[s * S + r0:s * S + r0 + half, :] * vv[s:s + 1, :]
                od_halves[hidx] = term if od_halves[hidx] is None else od_halves[hidx] + term
        od = jnp.concatenate(od_halves, axis=0)
        if att_blocks[i] is not None:
            od = od + att_blocks[i]
        o_blocks.append(od)
    o = o + jnp.concatenate(o_blocks, axis=0)

    k_dec = _bf(k * jnp.exp(edge - cum))
    st_ref[...] = st * jnp.exp(edge) + _dot_tn(vb, k_dec) * bd_ref[...]
    return o


def _hg_kernel(*refs, fwd):
    if fwd:
        f_ref, q_ref, i_ref, lb_ref, tri_ref, bd_ref, ones_ref, o_ref, st_ref, q_scr, k_scr, cum_scr = refs
    else:
        (f_ref, q_ref, i_ref, g_ref, of_ref, lb_ref, gn_ref, tri_ref, bd_ref, ones_ref, o_ref,
         st_ref, q_scr, k_scr, cum_scr, o_scr) = refs

    @pl.when(pl.program_id(1) == 0)
    def _():
        st_ref[...] = jnp.zeros_like(st_ref)

    nchunk = HG_TT // HG_CHUNK

    x = f_ref[0]
    lb = lb_ref[...]
    e = jnp.exp(-jnp.abs(x))
    inv = 1.0 / (1.0 + e)
    pos = x >= 0
    sig_p = jnp.where(pos, inv, e * inv)
    sig_n = jnp.where(pos, e * inv, inv)
    k_scr[...] = (1.0 - lb) * sig_n
    q_pre = q_ref[0]
    q_scr[...] = q_pre * _sigmoid(q_pre)
    parts = _split3(jnp.log(lb + (1.0 - lb) * sig_p))
    tri = tri_ref[...]
    for c in range(nchunk):
        rows = slice(c * HG_CHUNK, (c + 1) * HG_CHUNK)
        cum_scr[rows, :] = sum(_dot(tri, part[rows, :]) for part in parts)

    def chunk_body(ci, carry):
        c = ci if fwd else nchunk - 1 - ci
        rows = pl.ds(pl.multiple_of(c * HG_CHUNK, HG_CHUNK), HG_CHUNK)
        o = _hg_chunk(q_scr[rows, :], k_scr[rows, :], cum_scr[rows, :], i_ref[0, rows, :],
                      st_ref, bd_ref, ones_ref, fwd)
        if fwd:
            o_ref[0, rows, :] = o
        else:
            o_scr[rows, :] = o
        return carry

    lax.fori_loop(0, nchunk, chunk_body, 0, unroll=True)

    if not fwd:
        o = o_scr[...] + of_ref[0]
        sq = o * o
        hi = _bf(sq)
        lo = _bf(sq - hi.astype(F32))
        ms = (_dot(hi, ones_ref[...]) + _dot(lo, ones_ref[...])) * (1.0 / HEAD_DIM)
        g = g_ref[0]
        o_ref[0] = _bf(o * lax.rsqrt(ms + EPS) * gn_ref[...] * (g * _sigmoid(g)))


def _hg_call(hg, o_fwd, lb, gnorm, bd, ones_bd, fwd):
    B, L, _ = hg.shape
    nt = L // HG_TT
    tmap = (lambda t: t) if fwd else (lambda t: nt - 1 - t)
    col = lambda c: pl.BlockSpec((1, HG_TT, BRANCH_W), lambda b, t: (b, tmap(t), c))
    idx = np.arange(HG_CHUNK)
    tri = jnp.asarray((idx[None, :] <= idx[:, None]) if fwd else (idx[None, :] >= idx[:, None]), BF16)
    consts = [_const_spec((HG_CHUNK, HG_CHUNK)), _const_spec((BRANCH_W, BRANCH_W)), _const_spec((BRANCH_W, BRANCH_W))]
    vec = _const_spec((1, BRANCH_W))
    if fwd:
        in_specs = [col(0), col(2), col(3), vec] + consts
        args = [hg, hg, hg, lb, tri, bd, ones_bd]
        out_dtype = F32
    else:
        in_specs = [col(1), col(2), col(3), col(4), col(0), vec, vec] + consts
        args = [hg, hg, hg, hg, o_fwd, lb, gnorm, tri, bd, ones_bd]
        out_dtype = BF16
    return pl.pallas_call(
        functools.partial(_hg_kernel, fwd=fwd),
        out_shape=jax.ShapeDtypeStruct((B, L, BRANCH_W), out_dtype),
        grid=(B, nt),
        in_specs=in_specs,
        out_specs=col(0),
        scratch_shapes=[pltpu.VMEM((BRANCH_W, BRANCH_W), F32)]
        + [pltpu.VMEM((HG_TT, BRANCH_W), F32) for _ in range(3 if fwd else 4)],
        compiler_params=_params("parallel", "arbitrary"),
        name="hg_fwd" if fwd else "hg_bwd",
    )(*args)


MERGE_TM = 512


def _merge_kernel(x_ref, mod_ref, g_ref, na_ref, hg_ref, sg_ref, dl_ref, perm_ref, wg_ref, bg_ref, wb_ref, wo_ref, o_ref):
    x = x_ref[0]
    h = _bf(_mod_norm(x, g_ref[...], mod_ref[0, 1:2, :], mod_ref[0, 0:1, :]))
    dl_classes = jnp.concatenate([dl_ref[0, r] for r in range(dl_ref.shape[1])], axis=0)
    dl = _bf(_dot(perm_ref[...], dl_classes))
    merged = jnp.zeros((MERGE_TM, D_MODEL), F32)
    for bi, br in enumerate((na_ref[0], hg_ref[0], sg_ref[0], dl)):
        gate = _sigmoid(_dot(h, wg_ref[bi]) + bg_ref[bi:bi + 1, :])
        merged = merged + gate * _dot(br, wb_ref[bi])
    y = _dot(_bf(merged), wo_ref[...])
    o_ref[0] = x + mod_ref[0, 2:3, :] * y


def _merge_call(x, mod, g, branches, w_gate, b_gate, w_branch, w_o):
    B, L, _ = x.shape
    tm = MERGE_TM
    d16 = DIL_CONFIGS[2][1]
    tok_idx = np.arange(tm)
    perm = np.zeros((tm, tm), np.float32)
    perm[tok_idx, (tok_idx % d16) * (tm // d16) + tok_idx // d16] = 1.0
    tok = lambda w: pl.BlockSpec((1, tm, w), lambda b, t: (b, t, 0))
    return pl.pallas_call(
        _merge_kernel,
        out_shape=jax.ShapeDtypeStruct((B, L, D_MODEL), F32),
        grid=(B, L // tm),
        in_specs=[tok(D_MODEL),
                  pl.BlockSpec((1, 6, D_MODEL), lambda b, t: (b, 0, 0)),
                  _const_spec((1, D_MODEL)),
                  tok(BRANCH_W), tok(BRANCH_W), tok(BRANCH_W),
                  pl.BlockSpec((1, d16, tm // d16, BRANCH_W), lambda b, t: (b, 0, t, 0)),
                  _const_spec((tm, tm)),
                  _const_spec((N_BRANCH, D_MODEL, D_MODEL)),
                  _const_spec((N_BRANCH, D_MODEL)),
                  _const_spec((N_BRANCH, BRANCH_W, D_MODEL)),
                  _const_spec((D_MODEL, D_MODEL))],
        out_specs=tok(D_MODEL),
        compiler_params=_params("parallel", "arbitrary"),
        name="merge",
    )(x, mod, g, *branches, jnp.asarray(perm, BF16), w_gate, b_gate, w_branch, w_o)


FFN_TM = 512
FFN_SPLIT = 2


def _ffn_kernel(x_ref, mod_ref, g_ref, wg_ref, wu_ref, wd_ref, gf_ref, o_ref, *, final):
    x = x_ref[0]
    h = _bf(_mod_norm(x, g_ref[...], mod_ref[0, 4:5, :], mod_ref[0, 3:4, :]))
    y = jnp.zeros((FFN_TM, D_MODEL), F32)
    fc = D_FF // FFN_SPLIT
    for c in range(FFN_SPLIT):
        a = _dot(h, wg_ref[:, c * fc:(c + 1) * fc])
        b = _dot(h, wu_ref[:, c * fc:(c + 1) * fc])
        y = y + _dot(_bf(a * _sigmoid(a) * b), wd_ref[c * fc:(c + 1) * fc, :])
    x = x + mod_ref[0, 5:6, :] * y
    if final:
        x = x * lax.rsqrt(jnp.mean(x * x, axis=-1, keepdims=True) + EPS) * gf_ref[...]
    o_ref[0] = x


def _ffn_call(x, mod, g, wg, wu, wd, g_final, final):
    B, L, _ = x.shape
    tm = FFN_TM
    tok = pl.BlockSpec((1, tm, D_MODEL), lambda b, t: (b, t, 0))
    return pl.pallas_call(
        functools.partial(_ffn_kernel, final=final),
        out_shape=jax.ShapeDtypeStruct((B, L, D_MODEL), F32),
        grid=(B, L // tm),
        in_specs=[tok,
                  pl.BlockSpec((1, 6, D_MODEL), lambda b, t: (b, 0, 0)),
                  _const_spec((1, D_MODEL)),
                  _const_spec((D_MODEL, D_FF)), _const_spec((D_MODEL, D_FF)), _const_spec((D_FF, D_MODEL)),
                  _const_spec((1, D_MODEL))],
        out_specs=tok,
        compiler_params=_params("parallel", "arbitrary"),
        name="ffn",
    )(x, mod, g, wg, wu, wd, g_final)


def _trunk(x, mods, p):
    for l in range(DEPTH):
        mod = mods[l]
        na_qkv, hg, y_sg, dl1, dl4, dl16 = _proj_call(x, mod, p["g_mix"][l], p["w_in"][l], p["sg_ln_g"][l],
                                                      p["sg_ln_b"][l], p["sg_w"][l], p["sg_b"][l])
        y_na = _na_call(na_qkv, p["na_bias"][l])
        o_f = _hg_call(hg, None, p["lb"][0][l], None, p["bd"], p["ones_bd"], True)
        y_hg = _hg_call(hg, o_f, p["lb"][1][l], p["hg_gnorm"][l], p["bd"], p["ones_bd"], False)
        y_dl = _dilated_attention(dl1, dl4, dl16)
        x = _merge_call(x, mod, p["g_mix"][l], (y_na, y_hg, y_sg, y_dl),
                        p["w_gate"][l], p["b_gate"][l], p["w_branch"][l], p["w_o"][l])
        x = _ffn_call(x, mod, p["g_ffn"][l], p["w_ffn_gate"][l], p["w_ffn_up"][l], p["w_ffn_down"][l],
                      p["g_final"], l == DEPTH - 1)
    return x


def kernel(x_prompt, x_sample, c_prompt, c_sample, w_ada, b_ada, g_norm_mix, g_norm_ffn, w_in, w_gate, b_gate,
           w_branch, w_o, na_rpb, hg_lb, hg_gnorm, sg_ln_g, sg_ln_b, sg_w, sg_b, w_ffn_gate, w_ffn_up,
           w_ffn_down, g_final):
    nbp, nbs = c_prompt.shape[0], c_sample.shape[0]
    nb_pad = -(-(nbp + nbs) // 16) * 16
    c_all = jnp.concatenate([c_prompt, c_sample, jnp.zeros((nb_pad - nbp - nbs, D_MODEL), F32)], axis=0)
    mod_all = _ada_call(c_all, w_ada, b_ada).reshape(DEPTH, nb_pad, 6, D_MODEL)
    mods_p = [mod_all[l, :nbp] for l in range(DEPTH)]
    mods_s = [mod_all[l, nbp:nbp + nbs] for l in range(DEPTH)]

    lb_soft = jax.nn.softmax(hg_lb.astype(F32), axis=1)
    lower = jnp.cumsum(lb_soft, axis=1) - lb_soft[:, :1]
    head = np.arange(BRANCH_W) // HEAD_DIM
    bd = jnp.asarray((head[:, None] == head[None, :]).astype(np.float32))
    vec = lambda a: [a[l].reshape(1, -1).astype(F32) for l in range(DEPTH)]
    p = {
        "g_mix": vec(g_norm_mix), "g_ffn": vec(g_norm_ffn),
        "w_in": [_bf(w_in[l]) for l in range(DEPTH)],
        "w_gate": [_bf(w_gate[l]) for l in range(DEPTH)],
        "b_gate": [b_gate[l] for l in range(DEPTH)],
        "w_branch": [_bf(w_branch[l]) for l in range(DEPTH)],
        "w_o": [_bf(w_o[l]) for l in range(DEPTH)],
        "na_bias": [_na_bias_table(na_rpb[l]) for l in range(DEPTH)],
        "lb": [[lower[d, l].reshape(1, BRANCH_W) for l in range(DEPTH)] for d in range(2)],
        "hg_gnorm": vec(hg_gnorm), "sg_ln_g": vec(sg_ln_g), "sg_ln_b": vec(sg_ln_b),
        "sg_w": [_bf(sg_w[l].transpose(1, 0, 2).reshape(SG_CHUNK, SG_GROUPS * SG_CHUNK)) for l in range(DEPTH)],
        "sg_b": [jnp.repeat(sg_b[l].T.astype(F32), BRANCH_W // SG_GROUPS, axis=1) for l in range(DEPTH)],
        "w_ffn_gate": [_bf(w_ffn_gate[l]) for l in range(DEPTH)],
        "w_ffn_up": [_bf(w_ffn_up[l]) for l in range(DEPTH)],
        "w_ffn_down": [_bf(w_ffn_down[l]) for l in range(DEPTH)],
        "g_final": g_final.reshape(1, D_MODEL).astype(F32),
        "bd": bd, "ones_bd": _bf(bd),
    }
    return (_trunk(x_prompt, mods_p, p), _trunk(x_sample, mods_s, p))
```

```python
import functools

import jax
import jax.numpy as jnp
import numpy as np
from jax import lax
from jax.experimental import pallas as pl
from jax.experimental.pallas import tpu as pltpu

D_MODEL = 1024
DEPTH = 4
GRID_W = 64
BRANCH_W = 256
N_BRANCH = 4
HEAD_DIM = 64
N_HEADS = BRANCH_W // HEAD_DIM
ATTN_SCALE = HEAD_DIM ** -0.5
NA_KH = 8
NA_KW = 16
HG_CHUNK = 64
HG_SUB = 16
SG_GROUPS = 4
SG_CHUNK = 128
DIL_CONFIGS = ((128, 1), (512, 4), (2048, 16))
DIL_BLK = 64
D_FF = 2816
IN_WIDTH = 3328
NEG = -1e30
EPS = 1e-6

NA_ROWS_PER_STEP = 16
NA_TOK = NA_ROWS_PER_STEP * GRID_W
NA_WIN = NA_KH * GRID_W

VMEM_LIMIT = 56 * 1024 * 1024

BF16 = jnp.bfloat16
F32 = jnp.float32


def _bf(x):
    return x.astype(BF16)


def _dot(a, b):
    return jnp.dot(a, b, preferred_element_type=F32)


def _dot_nt(a, b):
    return lax.dot_general(a, b, (((1,), (1,)), ((), ())), preferred_element_type=F32)


def _dot_tn(a, b):
    return lax.dot_general(a, b, (((0,), (0,)), ((), ())), preferred_element_type=F32)


def _sigmoid(x):
    return 1.0 / (1.0 + jnp.exp(-x))


def _head_masks(dtype):
    lane_head = lax.broadcasted_iota(jnp.int32, (1, BRANCH_W), 1) // HEAD_DIM
    return [(lane_head == h).astype(dtype) for h in range(N_HEADS)]


def _pick_heads(blocks):
    lane_head = lax.broadcasted_iota(jnp.int32, (1, BRANCH_W), 1) // HEAD_DIM
    out = blocks[N_HEADS - 1]
    for h in range(N_HEADS - 2, -1, -1):
        out = jnp.where(lane_head <= h, blocks[h], out)
    return out


def _mod_norm(x, g, scale, shift):
    y = x * lax.rsqrt(jnp.mean(x * x, axis=-1, keepdims=True) + EPS)
    return (y * g) * (1.0 + scale) + shift


def _params(*sem):
    return pltpu.CompilerParams(dimension_semantics=sem, vmem_limit_bytes=VMEM_LIMIT)


def _const_spec(shape):
    nd = len(shape)
    return pl.BlockSpec(shape, lambda *_: (0,) * nd, pipeline_mode=pl.Buffered(1))


def _ada_kernel(c_ref, w_ref, b_ref, o_ref):
    c = c_ref[...]
    cs = c * _sigmoid(c)
    o_ref[0] = _dot(_bf(cs), _bf(w_ref[0])) + b_ref[0]


def _ada_call(c_all, w_ada, b_ada):
    nb = c_all.shape[0]
    tn = 1536
    return pl.pallas_call(
        _ada_kernel,
        out_shape=jax.ShapeDtypeStruct((DEPTH, nb, 6 * D_MODEL), F32),
        grid=(DEPTH, 6 * D_MODEL // tn),
        in_specs=[pl.BlockSpec((nb, D_MODEL), lambda l, n: (0, 0)),
                  pl.BlockSpec((1, D_MODEL, tn), lambda l, n: (l, 0, n)),
                  pl.BlockSpec((1, 1, tn), lambda l, n: (l, 0, n))],
        out_specs=pl.BlockSpec((1, nb, tn), lambda l, n: (l, 0, n)),
        compiler_params=_params("arbitrary", "arbitrary"),
        name="ada",
    )(c_all, w_ada, b_ada.reshape(DEPTH, 1, 6 * D_MODEL))


PROJ_TM = 512


LANES = 128


def _store_lane_chunks(scr_ref, val):
    for c in range(val.shape[1] // LANES):
        scr_ref[c] = val[:, c * LANES:(c + 1) * LANES]


def _load_class(scr_ref, r, n, dil):
    return jnp.concatenate([scr_ref[c, pl.ds(r, n, stride=dil), :] for c in range(scr_ref.shape[0])], axis=1)


def _gelu_tanh(x):
    return 0.5 * x * (1.0 + jnp.tanh(np.sqrt(2.0 / np.pi).astype(np.float32) * (x + 0.044715 * (x * x * x))))


def _spatial_gating(uv, ln_g, ln_b, w, b):
    lane_group = lax.broadcasted_iota(jnp.int32, (1, BRANCH_W), 1) // (BRANCH_W // SG_GROUPS)
    u = _gelu_tanh(uv[:, 0:BRANCH_W])
    v = _gelu_tanh(uv[:, BRANCH_W:2 * BRANCH_W])
    mu = jnp.mean(v, axis=-1, keepdims=True)
    var = jnp.mean(jnp.square(v - mu), axis=-1, keepdims=True)
    v = (v - mu) * lax.rsqrt(var + EPS) * ln_g + ln_b
    vstack = jnp.concatenate([_bf(jnp.where(lane_group == g, v, 0.0)) for g in range(SG_GROUPS)], axis=0)
    return u * (_dot(w, vstack) + b)


def _proj_kernel(x_ref, mod_ref, g_ref, w_ref, lng_ref, lnb_ref, sgw_ref, sgb_ref,
                 na_ref, hg_ref, sg_ref, dl1_ref, dl4_ref, dl16_ref, dl_scr):
    h = _bf(_mod_norm(x_ref[0], g_ref[...], mod_ref[0, 1:2, :], mod_ref[0, 0:1, :]))
    uv = _dot(h, w_ref[:, 2048:2560])
    dl = _dot(h, w_ref[:, 2560:3328])
    for n in range(PROJ_TM // SG_CHUNK):
        rows = slice(n * SG_CHUNK, (n + 1) * SG_CHUNK)
        sg_ref[0, rows, :] = _bf(_spatial_gating(uv[rows, :], lng_ref[...], lnb_ref[...], sgw_ref[...], sgb_ref[...]))
    dl1_ref[0] = _bf(dl)
    _store_lane_chunks(dl_scr, dl)
    for dil, ref in ((DIL_CONFIGS[1][1], dl4_ref), (DIL_CONFIGS[2][1], dl16_ref)):
        for r in range(dil):
            ref[0, r] = _bf(_load_class(dl_scr, r, PROJ_TM // dil, dil))
    na_ref[0] = _bf(_dot(h, w_ref[:, 0:768]))
    hg_ref[0] = _dot(h, w_ref[:, 768:2048])


def _proj_call(x, mod, g, w_in, sg_ln_g, sg_ln_b, sg_w, sg_b):
    B, L, _ = x.shape
    tm = PROJ_TM
    d4, d16 = DIL_CONFIGS[1][1], DIL_CONFIGS[2][1]
    tok = lambda w: pl.BlockSpec((1, tm, w), lambda b, t: (b, t, 0))
    cls = lambda d: pl.BlockSpec((1, d, tm // d, 768), lambda b, t: (b, 0, t, 0))
    return pl.pallas_call(
        _proj_kernel,
        out_shape=(jax.ShapeDtypeStruct((B, L, 768), BF16),
                   jax.ShapeDtypeStruct((B, L, 1280), F32),
                   jax.ShapeDtypeStruct((B, L, BRANCH_W), BF16),
                   jax.ShapeDtypeStruct((B, L, 768), BF16),
                   jax.ShapeDtypeStruct((B, d4, L // d4, 768), BF16),
                   jax.ShapeDtypeStruct((B, d16, L // d16, 768), BF16)),
        grid=(B, L // tm),
        in_specs=[tok(D_MODEL),
                  pl.BlockSpec((1, 6, D_MODEL), lambda b, t: (b, 0, 0)),
                  _const_spec((1, D_MODEL)),
                  _const_spec((D_MODEL, IN_WIDTH)),
                  _const_spec((1, BRANCH_W)), _const_spec((1, BRANCH_W)),
                  _const_spec((SG_CHUNK, SG_GROUPS * SG_CHUNK)),
                  _const_spec((SG_CHUNK, BRANCH_W))],
        out_specs=(tok(768), tok(1280), tok(BRANCH_W), tok(768), cls(d4), cls(d16)),
        scratch_shapes=[pltpu.VMEM((768 // LANES, tm, LANES), F32)],
        compiler_params=_params("parallel", "arbitrary"),
        name="proj",
    )(x, mod, g, w_in, sg_ln_g, sg_ln_b, sg_w, sg_b)


def _na_bias_table(rpb):
    c = np.arange(GRID_W)
    kc = np.arange(GRID_W)
    col_off = np.clip(kc[None, :] - c[:, None], -(NA_KW - 1), NA_KW - 1) + NA_KW - 1
    win0 = np.clip(c - NA_KW // 2, 0, GRID_W - NA_KW)
    valid = (kc[None, :] >= win0[:, None]) & (kc[None, :] < win0[:, None] + NA_KW)
    onehot = (col_off[None] == np.arange(2 * NA_KW - 1)[:, None, None]).astype(np.float32)
    cols = jnp.einsum("hrj,jck->hrck", rpb.astype(F32), jnp.asarray(onehot), precision=lax.Precision.HIGHEST)
    cols = jnp.where(valid[None, None], cols, NEG)
    pats = jnp.stack([cols[:, NA_KH - 1 - d:2 * NA_KH - 1 - d] for d in range(NA_KH)])
    return pats.transpose(0, 1, 3, 2, 4).reshape(NA_KH, N_HEADS * GRID_W, NA_WIN)


def _na_kernel(q_ref, k_ref, v_ref, bias_ref, o_ref, *, rows):
    i = pl.program_id(1)
    qmasks = [m * ATTN_SCALE for m in _head_masks(BF16)]

    def row_body(j, carry):
        r = i * NA_ROWS_PER_STEP + j
        start = jnp.clip(r - NA_KH // 2, 0, rows - NA_KH)
        off = pl.multiple_of(start * GRID_W, GRID_W)
        delta = r - start
        qoff = pl.multiple_of(j * GRID_W, GRID_W)
        q = q_ref[0, pl.ds(qoff, GRID_W), :]
        kw = k_ref[0, pl.ds(off, NA_WIN), :]
        vw = v_ref[0, pl.ds(off, NA_WIN), :]
        q_heads = jnp.concatenate([q * qmasks[h] for h in range(N_HEADS)], axis=0)
        s = _dot_nt(q_heads, kw) + bias_ref[delta]
        m = jnp.max(s, axis=-1, keepdims=True)
        e = jnp.exp(s - m)
        den = jnp.sum(e, axis=-1, keepdims=True)
        r_all = _dot(_bf(e), vw) * (1.0 / den)
        out = _pick_heads([r_all[h * GRID_W:(h + 1) * GRID_W, :] for h in range(N_HEADS)])
        o_ref[0, pl.ds(qoff, GRID_W), :] = _bf(out)
        return carry

    lax.fori_loop(0, NA_ROWS_PER_STEP, row_body, 0, unroll=True)


def _na_call(na_qkv, bias_tab):
    B, L, _ = na_qkv.shape
    rows = L // GRID_W
    seq = lambda col: pl.BlockSpec((1, L, BRANCH_W), lambda b, i: (b, 0, col))
    return pl.pallas_call(
        functools.partial(_na_kernel, rows=rows),
        out_shape=jax.ShapeDtypeStruct((B, L, BRANCH_W), BF16),
        grid=(B, rows // NA_ROWS_PER_STEP),
        in_specs=[pl.BlockSpec((1, NA_TOK, BRANCH_W), lambda b, i: (b, i, 0)), seq(1), seq(2),
                  _const_spec((NA_KH, N_HEADS * GRID_W, NA_WIN))],
        out_specs=pl.BlockSpec((1, NA_TOK, BRANCH_W), lambda b, i: (b, i, 0)),
        compiler_params=_params("parallel", "arbitrary"),
        name="na",
    )(na_qkv, na_qkv, na_qkv, bias_tab)


DL_TQ = 128
DL_TQB = 1024


def _alibi_slopes(n):
    start = 2.0 ** (-8.0 / n)
    return [float(np.float32(start ** (h + 1))) for h in range(n)]


def _dl_kernel(*refs, n, kw_len, tqb, cg, first, last, split):
    if first:
        q_ref, k_ref, v_ref, bias_ref = refs[:4]
        rest = refs[4:]
    else:
        q_ref, k_ref, v_ref, bias_ref, op_ref, lsep_ref = refs[:6]
        rest = refs[6:]
    t = pl.program_id(2)
    qmasks = [m * ATTN_SCALE for m in _head_masks(BF16)]
    nlane = BRANCH_W // LANES

    for c in range(cg):
        def tile_body(u, carry, c=c):
            base = t * tqb + u * DL_TQ
            ws = pl.multiple_of(jnp.clip(base - DIL_BLK, 0, n - kw_len), DIL_BLK)
            rows = pl.ds(pl.multiple_of(u * DL_TQ, DL_TQ), DL_TQ)
            q = q_ref[0, c, rows, :]
            kw = k_ref[0, c, pl.ds(ws, kw_len), :]
            vw = v_ref[0, c, pl.ds(ws, kw_len), :]
            q_heads = jnp.concatenate([q * qmasks[h] for h in range(N_HEADS)], axis=0)
            s = _dot_nt(q_heads, kw) + bias_ref[(base - ws) // DIL_BLK]
            m = jnp.max(s, axis=-1, keepdims=True)
            e = jnp.exp(s - m)
            den = jnp.sum(e, axis=-1, keepdims=True)
            r_all = _dot(_bf(e), vw) * (1.0 / den)
            lse_col = m + jnp.log(den)
            o = _pick_heads([r_all[h * DL_TQ:(h + 1) * DL_TQ, :] for h in range(N_HEADS)])
            lse = _pick_heads([lse_col[h * DL_TQ:(h + 1) * DL_TQ, :] for h in range(N_HEADS)])
            if not first:
                lse_p = lsep_ref[0, c, rows, :]
                lse_m = jnp.maximum(lse_p, lse)
                w_p = jnp.exp(lse_p - lse_m)
                w_c = jnp.exp(lse - lse_m)
                tot = w_p + w_c
                o = (w_p * op_ref[0, c, rows, :] + w_c * o) / tot
                lse = lse_m + jnp.log(tot)
            if last:
                rest[0][0, c, rows, :] = _bf(o)
            else:
                srows = pl.ds(pl.multiple_of(c * tqb + u * DL_TQ, DL_TQ), DL_TQ)
                for val, scr in ((o, rest[2]), (lse, rest[3])):
                    for lc in range(nlane):
                        scr[lc, srows, :] = val[:, lc * LANES:(lc + 1) * LANES]
            return carry

        lax.fori_loop(0, tqb // DL_TQ, tile_body, 0, unroll=True)

    if not last:
        for out_ref, scr in ((rest[0], rest[2]), (rest[1], rest[3])):
            for c in range(cg):
                for a in range(split):
                    out_ref[0, a, c] = jnp.concatenate(
                        [scr[lc, pl.ds(c * tqb + a, tqb // split, stride=split), :] for lc in range(nlane)], axis=1)


def _dl_bias_table(dil, kw_len):
    slopes = np.asarray(_alibi_slopes(N_HEADS), np.float32)
    qi = np.arange(DL_TQ)[None, :, None]
    ki = np.arange(kw_len)[None, None, :] - DIL_BLK * np.arange(3)[:, None, None]
    step = np.abs(ki - qi)
    dist = (step * dil).astype(np.float32)
    bias = np.where((step <= DIL_BLK)[:, None], -slopes[None, :, None, None] * dist[:, None], np.float32(NEG))
    return jnp.asarray(bias.reshape(3, N_HEADS * DL_TQ, kw_len), F32)


def _dl_call(qkv, state, dil, first, last, split):
    B, _, n, _ = qkv.shape
    kw_len = min(4 * DIL_BLK, n)
    tqb = min(DL_TQB, n)
    cg = min(dil, max(1, DL_TQB // tqb))
    tok = lambda c: pl.BlockSpec((1, cg, tqb, BRANCH_W), lambda b, r, t: (b, r, t, c))
    full = lambda c: pl.BlockSpec((1, cg, n, BRANCH_W), lambda b, r, t: (b, r, 0, c))
    in_specs = [tok(0), full(1), full(2), _const_spec((3, N_HEADS * DL_TQ, kw_len))]
    args = [qkv, qkv, qkv, _dl_bias_table(dil, kw_len)]
    if not first:
        in_specs += [tok(0), tok(0)]
        args += list(state)
    scratch = []
    if last:
        out_shape = jax.ShapeDtypeStruct((B, dil, n, BRANCH_W), BF16)
        out_specs = tok(0)
    else:
        shape = (B, split, dil, n // split, BRANCH_W)
        spec = lambda: pl.BlockSpec((1, split, cg, tqb // split, BRANCH_W), lambda b, r, t: (b, 0, r, t, 0))
        out_shape = (jax.ShapeDtypeStruct(shape, F32), jax.ShapeDtypeStruct(shape, F32))
        out_specs = (spec(), spec())
        scratch = [pltpu.VMEM((BRANCH_W // LANES, cg * tqb, LANES), F32) for _ in range(2)]
    out = pl.pallas_call(
        functools.partial(_dl_kernel, n=n, kw_len=kw_len, tqb=tqb, cg=cg, first=first, last=last, split=split),
        out_shape=out_shape,
        grid=(B, dil // cg, n // tqb),
        in_specs=in_specs,
        out_specs=out_specs,
        scratch_shapes=scratch,
        compiler_params=_params("parallel", "arbitrary", "arbitrary"),
        name=f"dl{dil}",
    )(*args)
    if last:
        return out
    return [o.reshape(B, split * dil, n // split, BRANCH_W) for o in out]


def _dilated_attention(dl1, dl4, dl16):
    (_, d1), (_, d4), (_, d16) = DIL_CONFIGS
    state = _dl_call(dl1[:, None], None, d1, True, False, d4 // d1)
    state = _dl_call(dl4, state, d4, False, False, d16 // d4)
    return _dl_call(dl16, state, d16, False, True, 1)


HG_TT = 512
LOG2E = float(np.log2(np.e))
N_SUB = HG_CHUNK // HG_SUB


def _split3(x):
    hi = _bf(x)
    r1 = x - hi.astype(F32)
    mid = _bf(r1)
    lo = _bf(r1 - mid.astype(F32))
    return hi, mid, lo


def _hg_chunk(q, k, cum, v, st_ref, bd_ref, ones_ref, fwd):
    C, S = HG_CHUNK, HG_SUB
    edge = cum[C - 1:C, :] if fwd else cum[0:1, :]
    vb = _bf(v)
    masks = _head_masks(F32)

    st = st_ref[...]
    o = _dot_nt(_bf(q * jnp.exp(cum)), _bf(st))

    att_blocks = []
    for i in range(N_SUB):
        lo, hi = i * S, (i + 1) * S
        if fwd:
            if i == 0:
                att_blocks.append(None)
                continue
            ref = cum[lo - 1:lo, :]
            k_t = jnp.concatenate([k[0:lo, :] * jnp.exp(ref - cum[0:lo, :]), jnp.zeros((C - lo, BRANCH_W), F32)], axis=0)
        else:
            if i == N_SUB - 1:
                att_blocks.append(None)
                continue
            ref = cum[hi:hi + 1, :]
            k_t = jnp.concatenate([jnp.zeros((hi, BRANCH_W), F32), k[hi:C, :] * jnp.exp(ref - cum[hi:C, :])], axis=0)
        k_t = _bf(k_t)
        q_t = q[lo:hi, :] * jnp.exp(cum[lo:hi, :] - ref)
        q_bd = _bf(jnp.concatenate([q_t * masks[h] for h in range(N_HEADS)], axis=0))
        att = _dot_nt(q_bd, k_t)
        r = _dot(_bf(att), vb)
        att_blocks.append(_pick_heads([r[h * S:(h + 1) * S, :] for h in range(N_HEADS)]))

    ones_bd = ones_ref[...]
    half = S // 2
    half_rows = lax.broadcasted_iota(jnp.int32, (half, 1), 0)
    cum2 = cum * LOG2E
    o_blocks = []
    for i in range(N_SUB):
        lo, hi = i * S, (i + 1) * S
        cq, qq, kk_, vv = cum2[lo:hi, :], q[lo:hi, :], k[lo:hi, :], v[lo:hi, :]
        xs = []
        for s in range(S):
            parts = []
            for r0 in (0, half):
                if fwd:
                    everything, nothing = s <= r0, s > r0 + half - 1
                else:
                    everything, nothing = s >= r0 + half - 1, s < r0
                if nothing:
                    parts.append(jnp.zeros((half, BRANCH_W), F32))
                    continue
                d = cq[r0:r0 + half, :] - cq[s:s + 1, :]
                if not everything:
                    ok = (half_rows >= s - r0) if fwd else (half_rows <= s - r0)
                    d = jnp.where(ok, d, NEG)
                parts.append(qq[r0:r0 + half, :] * kk_[s:s + 1, :] * jnp.exp2(d))
            xs.append(_bf(jnp.concatenate(parts, axis=0)))
        z = _dot(jnp.concatenate(xs, axis=0), ones_bd)
        od_halves = [None, None]
        for s in range(S):
            for hidx, r0 in enumerate((0, half)):
                if (s > r0 + half - 1) if fwd else (s < r0):
                    continue
                term = z[s * S + r0:s * S + r0 + half, :] * vv[s:s + 1, :]
                od_halves[hidx] = term if od_halves[hidx] is None else od_halves[hidx] + term
        od = jnp.concatenate(od_halves, axis=0)
        if att_blocks[i] is not None:
            od = od + att_blocks[i]
        o_blocks.append(od)
    o = o + jnp.concatenate(o_blocks, axis=0)

    k_dec = _bf(k * jnp.exp(edge - cum))
    st_ref[...] = st * jnp.exp(edge) + _dot_tn(vb, k_dec) * bd_ref[...]
    return o


def _hg_kernel(*refs, fwd):
    if fwd:
        f_ref, q_ref, i_ref, lb_ref, tri_ref, bd_ref, ones_ref, o_ref, st_ref, q_scr, k_scr, cum_scr = refs
    else:
        (f_ref, q_ref, i_ref, g_ref, of_ref, lb_ref, gn_ref, tri_ref, bd_ref, ones_ref, o_ref,
         st_ref, q_scr, k_scr, cum_scr, o_scr) = refs

    @pl.when(pl.program_id(1) == 0)
    def _():
        st_ref[...] = jnp.zeros_like(st_ref)

    nchunk = HG_TT // HG_CHUNK

    x = f_ref[0]
    lb = lb_ref[...]
    e = jnp.exp(-jnp.abs(x))
    inv = 1.0 / (1.0 + e)
    pos = x >= 0
    sig_p = jnp.where(pos, inv, e * inv)
    sig_n = jnp.where(pos, e * inv, inv)
    k_scr[...] = (1.0 - lb) * sig_n
    q_pre = q_ref[0]
    q_scr[...] = q_pre * _sigmoid(q_pre)
    parts = _split3(jnp.log(lb + (1.0 - lb) * sig_p))
    tri = tri_ref[...]
    for c in range(nchunk):
        rows = slice(c * HG_CHUNK, (c + 1) * HG_CHUNK)
        cum_scr[rows, :] = sum(_dot(tri, part[rows, :]) for part in parts)

    def chunk_body(ci, carry):
        c = ci if fwd else nchunk - 1 - ci
        rows = pl.ds(pl.multiple_of(c * HG_CHUNK, HG_CHUNK), HG_CHUNK)
        o = _hg_chunk(q_scr[rows, :], k_scr[rows, :], cum_scr[rows, :], i_ref[0, rows, :],
                      st_ref, bd_ref, ones_ref, fwd)
        if fwd:
            o_ref[0, rows, :] = o
        else:
            o_scr[rows, :] = o
        return carry

    lax.fori_loop(0, nchunk, chunk_body, 0, unroll=True)

    if not fwd:
        o = o_scr[...] + of_ref[0]
        sq = o * o
        hi = _bf(sq)
        lo = _bf(sq - hi.astype(F32))
        ms = (_dot(hi, ones_ref[...]) + _dot(lo, ones_ref[...])) * (1.0 / HEAD_DIM)
        g = g_ref[0]
        o_ref[0] = _bf(o * lax.rsqrt(ms + EPS) * gn_ref[...] * (g * _sigmoid(g)))


def _hg_call(hg, o_fwd, lb, gnorm, bd, ones_bd, fwd):
    B, L, _ = hg.shape
    nt = L // HG_TT
    tmap = (lambda t: t) if fwd else (lambda t: nt - 1 - t)
    col = lambda c: pl.BlockSpec((1, HG_TT, BRANCH_W), lambda b, t: (b, tmap(t), c))
    idx = np.arange(HG_CHUNK)
    tri = jnp.asarray((idx[None, :] <= idx[:, None]) if fwd else (idx[None, :] >= idx[:, None]), BF16)
    consts = [_const_spec((HG_CHUNK, HG_CHUNK)), _const_spec((BRANCH_W, BRANCH_W)), _const_spec((BRANCH_W, BRANCH_W))]
    vec = _const_spec((1, BRANCH_W))
    if fwd:
        in_specs = [col(0), col(2), col(3), vec] + consts
        args = [hg, hg, hg, lb, tri, bd, ones_bd]
        out_dtype = F32
    else:
        in_specs = [col(1), col(2), col(3), col(4), col(0), vec, vec] + consts
        args = [hg, hg, hg, hg, o_fwd, lb, gnorm, tri, bd, ones_bd]
        out_dtype = BF16
    return pl.pallas_call(
        functools.partial(_hg_kernel, fwd=fwd),
        out_shape=jax.ShapeDtypeStruct((B, L, BRANCH_W), out_dtype),
        grid=(B, nt),
        in_specs=in_specs,
        out_specs=col(0),
        scratch_shapes=[pltpu.VMEM((BRANCH_W, BRANCH_W), F32)]
        + [pltpu.VMEM((HG_TT, BRANCH_W), F32) for _ in range(3 if fwd else 4)],
        compiler_params=_params("parallel", "arbitrary"),
        name="hg_fwd" if fwd else "hg_bwd",
    )(*args)


MERGE_TM = 512


def _merge_kernel(x_ref, mod_ref, g_ref, na_ref, hg_ref, sg_ref, dl_ref, perm_ref, wg_ref, bg_ref, wb_ref, wo_ref, o_ref):
    x = x_ref[0]
    h = _bf(_mod_norm(x, g_ref[...], mod_ref[0, 1:2, :], mod_ref[0, 0:1, :]))
    dl_classes = jnp.concatenate([dl_ref[0, r] for r in range(dl_ref.shape[1])], axis=0)
    dl = _bf(_dot(perm_ref[...], dl_classes))
    merged = jnp.zeros((MERGE_TM, D_MODEL), F32)
    for bi, br in enumerate((na_ref[0], hg_ref[0], sg_ref[0], dl)):
        gate = _sigmoid(_dot(h, wg_ref[bi]) + bg_ref[bi:bi + 1, :])
        merged = merged + gate * _dot(br, wb_ref[bi])
    y = _dot(_bf(merged), wo_ref[...])
    o_ref[0] = x + mod_ref[0, 2:3, :] * y


def _merge_call(x, mod, g, branches, w_gate, b_gate, w_branch, w_o):
    B, L, _ = x.shape
    tm = MERGE_TM
    d16 = DIL_CONFIGS[2][1]
    tok_idx = np.arange(tm)
    perm = np.zeros((tm, tm), np.float32)
    perm[tok_idx, (tok_idx % d16) * (tm // d16) + tok_idx // d16] = 1.0
    tok = lambda w: pl.BlockSpec((1, tm, w), lambda b, t: (b, t, 0))
    return pl.pallas_call(
        _merge_kernel,
        out_shape=jax.ShapeDtypeStruct((B, L, D_MODEL), F32),
        grid=(B, L // tm),
        in_specs=[tok(D_MODEL),
                  pl.BlockSpec((1, 6, D_MODEL), lambda b, t: (b, 0, 0)),
                  _const_spec((1, D_MODEL)),
                  tok(BRANCH_W), tok(BRANCH_W), tok(BRANCH_W),
                  pl.BlockSpec((1, d16, tm // d16, BRANCH_W), lambda b, t: (b, 0, t, 0)),
                  _const_spec((tm, tm)),
                  _const_spec((N_BRANCH, D_MODEL, D_MODEL)),
                  _const_spec((N_BRANCH, D_MODEL)),
                  _const_spec((N_BRANCH, BRANCH_W, D_MODEL)),
                  _const_spec((D_MODEL, D_MODEL))],
        out_specs=tok(D_MODEL),
        compiler_params=_params("parallel", "arbitrary"),
        name="merge",
    )(x, mod, g, *branches, jnp.asarray(perm, BF16), w_gate, b_gate, w_branch, w_o)


FFN_TM = 512
MXU_TILE = 256
FFN_CHUNKS = ((0, 6 * MXU_TILE), (6 * MXU_TILE, D_FF))


def _ffn_kernel(x_ref, mod_ref, g_ref, wg_ref, wu_ref, wd_ref, gf_ref, o_ref, *, final):
    x = x_ref[0]
    h = _bf(_mod_norm(x, g_ref[...], mod_ref[0, 4:5, :], mod_ref[0, 3:4, :]))
    y = jnp.zeros((FFN_TM, D_MODEL), F32)
    for lo, hi in FFN_CHUNKS:
        a = _dot(h, wg_ref[:, lo:hi])
        b = _dot(h, wu_ref[:, lo:hi])
        y = y + _dot(_bf(a * _sigmoid(a) * b), wd_ref[lo:hi, :])
    x = x + mod_ref[0, 5:6, :] * y
    if final:
        x = x * lax.rsqrt(jnp.mean(x * x, axis=-1, keepdims=True) + EPS) * gf_ref[...]
    o_ref[0] = x


def _ffn_call(x, mod, g, wg, wu, wd, g_final, final):
    B, L, _ = x.shape
    tm = FFN_TM
    tok = pl.BlockSpec((1, tm, D_MODEL), lambda b, t: (b, t, 0))
    return pl.pallas_call(
        functools.partial(_ffn_kernel, final=final),
        out_shape=jax.ShapeDtypeStruct((B, L, D_MODEL), F32),
        grid=(B, L // tm),
        in_specs=[tok,
                  pl.BlockSpec((1, 6, D_MODEL), lambda b, t: (b, 0, 0)),
                  _const_spec((1, D_MODEL)),
                  _const_spec((D_MODEL, D_FF)), _const_spec((D_MODEL, D_FF)), _const_spec((D_FF, D_MODEL)),
                  _const_spec((1, D_MODEL))],
        out_specs=tok,
        compiler_params=_params("parallel", "arbitrary"),
        name="ffn",
    )(x, mod, g, wg, wu, wd, g_final)


def _trunk(x, mods, p):
    for l in range(DEPTH):
        mod = mods[l]
        na_qkv, hg, y_sg, dl1, dl4, dl16 = _proj_call(x, mod, p["g_mix"][l], p["w_in"][l], p["sg_ln_g"][l],
                                                      p["sg_ln_b"][l], p["sg_w"][l], p["sg_b"][l])
        y_na = _na_call(na_qkv, p["na_bias"][l])
        o_f = _hg_call(hg, None, p["lb"][0][l], None, p["bd"], p["ones_bd"], True)
        y_hg = _hg_call(hg, o_f, p["lb"][1][l], p["hg_gnorm"][l], p["bd"], p["ones_bd"], False)
        y_dl = _dilated_attention(dl1, dl4, dl16)
        x = _merge_call(x, mod, p["g_mix"][l], (y_na, y_hg, y_sg, y_dl),
                        p["w_gate"][l], p["b_gate"][l], p["w_branch"][l], p["w_o"][l])
        x = _ffn_call(x, mod, p["g_ffn"][l], p["w_ffn_gate"][l], p["w_ffn_up"][l], p["w_ffn_down"][l],
                      p["g_final"], l == DEPTH - 1)
    return x


def kernel(x_prompt, x_sample, c_prompt, c_sample, w_ada, b_ada, g_norm_mix, g_norm_ffn, w_in, w_gate, b_gate,
           w_branch, w_o, na_rpb, hg_lb, hg_gnorm, sg_ln_g, sg_ln_b, sg_w, sg_b, w_ffn_gate, w_ffn_up,
           w_ffn_down, g_final):
    nbp, nbs = c_prompt.shape[0], c_sample.shape[0]
    nb_pad = -(-(nbp + nbs) // 16) * 16
    c_all = jnp.concatenate([c_prompt, c_sample, jnp.zeros((nb_pad - nbp - nbs, D_MODEL), F32)], axis=0)
    mod_all = _ada_call(c_all, w_ada, b_ada).reshape(DEPTH, nb_pad, 6, D_MODEL)
    mods_p = [mod_all[l, :nbp] for l in range(DEPTH)]
    mods_s = [mod_all[l, nbp:nbp + nbs] for l in range(DEPTH)]

    lb_soft = jax.nn.softmax(hg_lb.astype(F32), axis=1)
    lower = jnp.cumsum(lb_soft, axis=1) - lb_soft[:, :1]
    head = np.arange(BRANCH_W) // HEAD_DIM
    bd = jnp.asarray((head[:, None] == head[None, :]).astype(np.float32))
    vec = lambda a: [a[l].reshape(1, -1).astype(F32) for l in range(DEPTH)]
    p = {
        "g_mix": vec(g_norm_mix), "g_ffn": vec(g_norm_ffn),
        "w_in": [_bf(w_in[l]) for l in range(DEPTH)],
        "w_gate": [_bf(w_gate[l]) for l in range(DEPTH)],
        "b_gate": [b_gate[l] for l in range(DEPTH)],
        "w_branch": [_bf(w_branch[l]) for l in range(DEPTH)],
        "w_o": [_bf(w_o[l]) for l in range(DEPTH)],
        "na_bias": [_na_bias_table(na_rpb[l]) for l in range(DEPTH)],
        "lb": [[lower[d, l].reshape(1, BRANCH_W) for l in range(DEPTH)] for d in range(2)],
        "hg_gnorm": vec(hg_gnorm), "sg_ln_g": vec(sg_ln_g), "sg_ln_b": vec(sg_ln_b),
        "sg_w": [_bf(sg_w[l].transpose(1, 0, 2).reshape(SG_CHUNK, SG_GROUPS * SG_CHUNK)) for l in range(DEPTH)],
        "sg_b": [jnp.repeat(sg_b[l].T.astype(F32), BRANCH_W // SG_GROUPS, axis=1) for l in range(DEPTH)],
        "w_ffn_gate": [_bf(w_ffn_gate[l]) for l in range(DEPTH)],
        "w_ffn_up": [_bf(w_ffn_up[l]) for l in range(DEPTH)],
        "w_ffn_down": [_bf(w_ffn_down[l]) for l in range(DEPTH)],
        "g_final": g_final.reshape(1, D_MODEL).astype(F32),
        "bd": bd, "ones_bd": _bf(bd),
    }
    return (_trunk(x_prompt, mods_p, p), _trunk(x_sample, mods_s, p))
```

```python
import functools

import jax
import jax.numpy as jnp
import numpy as np
from jax import lax
from jax.experimental import pallas as pl
from jax.experimental.pallas import tpu as pltpu

D_MODEL = 1024
DEPTH = 4
GRID_W = 64
BRANCH_W = 256
N_BRANCH = 4
HEAD_DIM = 64
N_HEADS = BRANCH_W // HEAD_DIM
ATTN_SCALE = HEAD_DIM ** -0.5
NA_KH = 8
NA_KW = 16
HG_CHUNK = 64
HG_SUB = 16
SG_GROUPS = 4
SG_CHUNK = 128
DIL_CONFIGS = ((128, 1), (512, 4), (2048, 16))
DIL_BLK = 64
D_FF = 2816
IN_WIDTH = 3328
NEG = -1e30
EPS = 1e-6

NA_ROWS_PER_STEP = 16
NA_TOK = NA_ROWS_PER_STEP * GRID_W
NA_WIN = NA_KH * GRID_W

VMEM_LIMIT = 56 * 1024 * 1024

BF16 = jnp.bfloat16
F32 = jnp.float32


def _bf(x):
    return x.astype(BF16)


def _dot(a, b):
    return jnp.dot(a, b, preferred_element_type=F32)


def _dot_nt(a, b):
    return lax.dot_general(a, b, (((1,), (1,)), ((), ())), preferred_element_type=F32)


def _dot_tn(a, b):
    return lax.dot_general(a, b, (((0,), (0,)), ((), ())), preferred_element_type=F32)


def _sigmoid(x):
    return 1.0 / (1.0 + jnp.exp(-x))


def _head_masks(dtype):
    lane_head = lax.broadcasted_iota(jnp.int32, (1, BRANCH_W), 1) // HEAD_DIM
    return [(lane_head == h).astype(dtype) for h in range(N_HEADS)]


def _pick_heads(blocks):
    lane_head = lax.broadcasted_iota(jnp.int32, (1, BRANCH_W), 1) // HEAD_DIM
    out = blocks[N_HEADS - 1]
    for h in range(N_HEADS - 2, -1, -1):
        out = jnp.where(lane_head <= h, blocks[h], out)
    return out


def _mod_norm(x, g, scale, shift):
    y = x * lax.rsqrt(jnp.mean(x * x, axis=-1, keepdims=True) + EPS)
    return (y * g) * (1.0 + scale) + shift


def _params(*sem):
    return pltpu.CompilerParams(dimension_semantics=sem, vmem_limit_bytes=VMEM_LIMIT)


def _const_spec(shape):
    nd = len(shape)
    return pl.BlockSpec(shape, lambda *_: (0,) * nd, pipeline_mode=pl.Buffered(1))


def _ada_kernel(c_ref, w_ref, b_ref, o_ref):
    c = c_ref[...]
    cs = c * _sigmoid(c)
    o_ref[0] = _dot(_bf(cs), _bf(w_ref[0])) + b_ref[0]


def _ada_call(c_all, w_ada, b_ada):
    nb = c_all.shape[0]
    tn = 1536
    return pl.pallas_call(
        _ada_kernel,
        out_shape=jax.ShapeDtypeStruct((DEPTH, nb, 6 * D_MODEL), F32),
        grid=(DEPTH, 6 * D_MODEL // tn),
        in_specs=[pl.BlockSpec((nb, D_MODEL), lambda l, n: (0, 0)),
                  pl.BlockSpec((1, D_MODEL, tn), lambda l, n: (l, 0, n)),
                  pl.BlockSpec((1, 1, tn), lambda l, n: (l, 0, n))],
        out_specs=pl.BlockSpec((1, nb, tn), lambda l, n: (l, 0, n)),
        compiler_params=_params("arbitrary", "arbitrary"),
        name="ada",
    )(c_all, w_ada, b_ada.reshape(DEPTH, 1, 6 * D_MODEL))


PROJ_TM = 1024


LANES = 128


def _store_lane_chunks(scr_ref, val):
    for c in range(val.shape[1] // LANES):
        scr_ref[c] = val[:, c * LANES:(c + 1) * LANES]


def _load_class(scr_ref, r, n, dil):
    return jnp.concatenate([scr_ref[c, pl.ds(r, n, stride=dil), :] for c in range(scr_ref.shape[0])], axis=1)


def _gelu_tanh(x):
    return 0.5 * x * (1.0 + jnp.tanh(np.sqrt(2.0 / np.pi).astype(np.float32) * (x + 0.044715 * (x * x * x))))


def _spatial_gating(uv, ln_g, ln_b, w, b):
    lane_group = lax.broadcasted_iota(jnp.int32, (1, BRANCH_W), 1) // (BRANCH_W // SG_GROUPS)
    u = _gelu_tanh(uv[:, 0:BRANCH_W])
    v = _gelu_tanh(uv[:, BRANCH_W:2 * BRANCH_W])
    mu = jnp.mean(v, axis=-1, keepdims=True)
    var = jnp.mean(jnp.square(v - mu), axis=-1, keepdims=True)
    v = (v - mu) * lax.rsqrt(var + EPS) * ln_g + ln_b
    vstack = jnp.concatenate([_bf(jnp.where(lane_group == g, v, 0.0)) for g in range(SG_GROUPS)], axis=0)
    return u * (_dot(w, vstack) + b)


def _proj_kernel(x_ref, mod_ref, g_ref, w_ref, lng_ref, lnb_ref, sgw_ref, sgb_ref,
                 na_ref, hg_ref, sg_ref, dl1_ref, dl4_ref, dl16_ref, dl_scr):
    h = _bf(_mod_norm(x_ref[0], g_ref[...], mod_ref[0, 1:2, :], mod_ref[0, 0:1, :]))
    uv = _dot(h, w_ref[:, 2048:2560])
    dl = _dot(h, w_ref[:, 2560:3328])
    for n in range(PROJ_TM // SG_CHUNK):
        rows = slice(n * SG_CHUNK, (n + 1) * SG_CHUNK)
        sg_ref[0, rows, :] = _bf(_spatial_gating(uv[rows, :], lng_ref[...], lnb_ref[...], sgw_ref[...], sgb_ref[...]))
    dl1_ref[0] = _bf(dl)
    _store_lane_chunks(dl_scr, dl)
    for dil, ref in ((DIL_CONFIGS[1][1], dl4_ref), (DIL_CONFIGS[2][1], dl16_ref)):
        for r in range(dil):
            ref[0, r] = _bf(_load_class(dl_scr, r, PROJ_TM // dil, dil))
    na_ref[0] = _bf(_dot(h, w_ref[:, 0:768]))
    hg_ref[0] = _dot(h, w_ref[:, 768:2048])


def _proj_call(x, mod, g, w_in, sg_ln_g, sg_ln_b, sg_w, sg_b):
    B, L, _ = x.shape
    tm = PROJ_TM
    d4, d16 = DIL_CONFIGS[1][1], DIL_CONFIGS[2][1]
    tok = lambda w: pl.BlockSpec((1, tm, w), lambda b, t: (b, t, 0))
    cls = lambda d: pl.BlockSpec((1, d, tm // d, 768), lambda b, t: (b, 0, t, 0))
    return pl.pallas_call(
        _proj_kernel,
        out_shape=(jax.ShapeDtypeStruct((B, L, 768), BF16),
                   jax.ShapeDtypeStruct((B, L, 1280), F32),
                   jax.ShapeDtypeStruct((B, L, BRANCH_W), BF16),
                   jax.ShapeDtypeStruct((B, L, 768), BF16),
                   jax.ShapeDtypeStruct((B, d4, L // d4, 768), BF16),
                   jax.ShapeDtypeStruct((B, d16, L // d16, 768), BF16)),
        grid=(B, L // tm),
        in_specs=[tok(D_MODEL),
                  pl.BlockSpec((1, 6, D_MODEL), lambda b, t: (b, 0, 0)),
                  _const_spec((1, D_MODEL)),
                  _const_spec((D_MODEL, IN_WIDTH)),
                  _const_spec((1, BRANCH_W)), _const_spec((1, BRANCH_W)),
                  _const_spec((SG_CHUNK, SG_GROUPS * SG_CHUNK)),
                  _const_spec((SG_CHUNK, BRANCH_W))],
        out_specs=(tok(768), tok(1280), tok(BRANCH_W), tok(768), cls(d4), cls(d16)),
        scratch_shapes=[pltpu.VMEM((768 // LANES, tm, LANES), F32)],
        compiler_params=_params("parallel", "arbitrary"),
        name="proj",
    )(x, mod, g, w_in, sg_ln_g, sg_ln_b, sg_w, sg_b)


def _na_bias_table(rpb):
    c = np.arange(GRID_W)
    kc = np.arange(GRID_W)
    col_off = np.clip(kc[None, :] - c[:, None], -(NA_KW - 1), NA_KW - 1) + NA_KW - 1
    win0 = np.clip(c - NA_KW // 2, 0, GRID_W - NA_KW)
    valid = (kc[None, :] >= win0[:, None]) & (kc[None, :] < win0[:, None] + NA_KW)
    onehot = (col_off[None] == np.arange(2 * NA_KW - 1)[:, None, None]).astype(np.float32)
    cols = jnp.einsum("hrj,jck->hrck", rpb.astype(F32), jnp.asarray(onehot), precision=lax.Precision.HIGHEST)
    cols = jnp.where(valid[None, None], cols, NEG)
    pats = jnp.stack([cols[:, NA_KH - 1 - d:2 * NA_KH - 1 - d] for d in range(NA_KH)])
    return pats.transpose(0, 1, 3, 2, 4).reshape(NA_KH, N_HEADS * GRID_W, NA_WIN)


def _na_kernel(q_ref, k_ref, v_ref, bias_ref, o_ref, *, rows):
    i = pl.program_id(1)
    qmasks = [m * ATTN_SCALE for m in _head_masks(BF16)]

    def row_body(j, carry):
        r = i * NA_ROWS_PER_STEP + j
        start = jnp.clip(r - NA_KH // 2, 0, rows - NA_KH)
        off = pl.multiple_of(start * GRID_W, GRID_W)
        delta = r - start
        qoff = pl.multiple_of(j * GRID_W, GRID_W)
        q = q_ref[0, pl.ds(qoff, GRID_W), :]
        kw = k_ref[0, pl.ds(off, NA_WIN), :]
        vw = v_ref[0, pl.ds(off, NA_WIN), :]
        q_heads = jnp.concatenate([q * qmasks[h] for h in range(N_HEADS)], axis=0)
        s = _dot_nt(q_heads, kw) + bias_ref[delta]
        m = jnp.max(s, axis=-1, keepdims=True)
        e = jnp.exp(s - m)
        den = jnp.sum(e, axis=-1, keepdims=True)
        r_all = _dot(_bf(e), vw) * (1.0 / den)
        out = _pick_heads([r_all[h * GRID_W:(h + 1) * GRID_W, :] for h in range(N_HEADS)])
        o_ref[0, pl.ds(qoff, GRID_W), :] = _bf(out)
        return carry

    lax.fori_loop(0, NA_ROWS_PER_STEP, row_body, 0, unroll=True)


def _na_call(na_qkv, bias_tab):
    B, L, _ = na_qkv.shape
    rows = L // GRID_W
    seq = lambda col: pl.BlockSpec((1, L, BRANCH_W), lambda b, i: (b, 0, col))
    return pl.pallas_call(
        functools.partial(_na_kernel, rows=rows),
        out_shape=jax.ShapeDtypeStruct((B, L, BRANCH_W), BF16),
        grid=(B, rows // NA_ROWS_PER_STEP),
        in_specs=[pl.BlockSpec((1, NA_TOK, BRANCH_W), lambda b, i: (b, i, 0)), seq(1), seq(2),
                  _const_spec((NA_KH, N_HEADS * GRID_W, NA_WIN))],
        out_specs=pl.BlockSpec((1, NA_TOK, BRANCH_W), lambda b, i: (b, i, 0)),
        compiler_params=_params("parallel", "arbitrary"),
        name="na",
    )(na_qkv, na_qkv, na_qkv, bias_tab)


DL_TQ = 128
DL_TQB = 1024


def _alibi_slopes(n):
    start = 2.0 ** (-8.0 / n)
    return [float(np.float32(start ** (h + 1))) for h in range(n)]


def _dl_kernel(*refs, n, kw_len, tqb, cg, first, last, split):
    if first:
        q_ref, k_ref, v_ref, bias_ref = refs[:4]
        rest = refs[4:]
    else:
        q_ref, k_ref, v_ref, bias_ref, op_ref, lsep_ref = refs[:6]
        rest = refs[6:]
    t = pl.program_id(2)
    qmasks = [m * ATTN_SCALE for m in _head_masks(BF16)]
    nlane = BRANCH_W // LANES

    for c in range(cg):
        def tile_body(u, carry, c=c):
            base = t * tqb + u * DL_TQ
            ws = pl.multiple_of(jnp.clip(base - DIL_BLK, 0, n - kw_len), DIL_BLK)
            rows = pl.ds(pl.multiple_of(u * DL_TQ, DL_TQ), DL_TQ)
            q = q_ref[0, c, rows, :]
            kw = k_ref[0, c, pl.ds(ws, kw_len), :]
            vw = v_ref[0, c, pl.ds(ws, kw_len), :]
            q_heads = jnp.concatenate([q * qmasks[h] for h in range(N_HEADS)], axis=0)
            s = _dot_nt(q_heads, kw) + bias_ref[(base - ws) // DIL_BLK]
            m = jnp.max(s, axis=-1, keepdims=True)
            e = jnp.exp(s - m)
            den = jnp.sum(e, axis=-1, keepdims=True)
            r_all = _dot(_bf(e), vw) * (1.0 / den)
            lse_col = m + jnp.log(den)
            o = _pick_heads([r_all[h * DL_TQ:(h + 1) * DL_TQ, :] for h in range(N_HEADS)])
            lse = _pick_heads([lse_col[h * DL_TQ:(h + 1) * DL_TQ, :] for h in range(N_HEADS)])
            if not first:
                lse_p = lsep_ref[0, c, rows, :]
                lse_m = jnp.maximum(lse_p, lse)
                w_p = jnp.exp(lse_p - lse_m)
                w_c = jnp.exp(lse - lse_m)
                tot = w_p + w_c
                o = (w_p * op_ref[0, c, rows, :] + w_c * o) / tot
                lse = lse_m + jnp.log(tot)
            if last:
                rest[0][0, c, rows, :] = _bf(o)
            else:
                srows = pl.ds(pl.multiple_of(c * tqb + u * DL_TQ, DL_TQ), DL_TQ)
                for val, scr in ((o, rest[2]), (lse, rest[3])):
                    for lc in range(nlane):
                        scr[lc, srows, :] = val[:, lc * LANES:(lc + 1) * LANES]
            return carry

        lax.fori_loop(0, tqb // DL_TQ, tile_body, 0, unroll=True)

    if not last:
        for out_ref, scr in ((rest[0], rest[2]), (rest[1], rest[3])):
            for c in range(cg):
                for a in range(split):
                    out_ref[0, a, c] = jnp.concatenate(
                        [scr[lc, pl.ds(c * tqb + a, tqb // split, stride=split), :] for lc in range(nlane)], axis=1)


def _dl_bias_table(dil, kw_len):
    slopes = np.asarray(_alibi_slopes(N_HEADS), np.float32)
    qi = np.arange(DL_TQ)[None, :, None]
    ki = np.arange(kw_len)[None, None, :] - DIL_BLK * np.arange(3)[:, None, None]
    step = np.abs(ki - qi)
    dist = (step * dil).astype(np.float32)
    bias = np.where((step <= DIL_BLK)[:, None], -slopes[None, :, None, None] * dist[:, None], np.float32(NEG))
    return jnp.asarray(bias.reshape(3, N_HEADS * DL_TQ, kw_len), F32)


def _dl_call(qkv, state, dil, first, last, split):
    B, _, n, _ = qkv.shape
    kw_len = min(4 * DIL_BLK, n)
    tqb = min(DL_TQB, n)
    cg = min(dil, max(1, DL_TQB // tqb))
    tok = lambda c: pl.BlockSpec((1, cg, tqb, BRANCH_W), lambda b, r, t: (b, r, t, c))
    full = lambda c: pl.BlockSpec((1, cg, n, BRANCH_W), lambda b, r, t: (b, r, 0, c))
    in_specs = [tok(0), full(1), full(2), _const_spec((3, N_HEADS * DL_TQ, kw_len))]
    args = [qkv, qkv, qkv, _dl_bias_table(dil, kw_len)]
    if not first:
        in_specs += [tok(0), tok(0)]
        args += list(state)
    scratch = []
    if last:
        out_shape = jax.ShapeDtypeStruct((B, dil, n, BRANCH_W), BF16)
        out_specs = tok(0)
    else:
        shape = (B, split, dil, n // split, BRANCH_W)
        spec = lambda: pl.BlockSpec((1, split, cg, tqb // split, BRANCH_W), lambda b, r, t: (b, 0, r, t, 0))
        out_shape = (jax.ShapeDtypeStruct(shape, F32), jax.ShapeDtypeStruct(shape, F32))
        out_specs = (spec(), spec())
        scratch = [pltpu.VMEM((BRANCH_W // LANES, cg * tqb, LANES), F32) for _ in range(2)]
    out = pl.pallas_call(
        functools.partial(_dl_kernel, n=n, kw_len=kw_len, tqb=tqb, cg=cg, first=first, last=last, split=split),
        out_shape=out_shape,
        grid=(B, dil // cg, n // tqb),
        in_specs=in_specs,
        out_specs=out_specs,
        scratch_shapes=scratch,
        compiler_params=_params("parallel", "arbitrary", "arbitrary"),
        name=f"dl{dil}",
    )(*args)
    if last:
        return out
    return [o.reshape(B, split * dil, n // split, BRANCH_W) for o in out]


def _dilated_attention(dl1, dl4, dl16):
    (_, d1), (_, d4), (_, d16) = DIL_CONFIGS
    state = _dl_call(dl1[:, None], None, d1, True, False, d4 // d1)
    state = _dl_call(dl4, state, d4, False, False, d16 // d4)
    return _dl_call(dl16, state, d16, False, True, 1)


HG_TT = 1024
LOG2E = float(np.log2(np.e))
N_SUB = HG_CHUNK // HG_SUB


def _split3(x):
    hi = _bf(x)
    r1 = x - hi.astype(F32)
    mid = _bf(r1)
    lo = _bf(r1 - mid.astype(F32))
    return hi, mid, lo


def _hg_chunk(q, k, cum, v, st_ref, bd_ref, ones_ref, fwd):
    C, S = HG_CHUNK, HG_SUB
    edge = cum[C - 1:C, :] if fwd else cum[0:1, :]
    vb = _bf(v)
    masks = _head_masks(F32)

    st = st_ref[...]
    o = _dot_nt(_bf(q * jnp.exp(cum)), _bf(st))

    att_blocks = []
    for i in range(N_SUB):
        lo, hi = i * S, (i + 1) * S
        if fwd:
            if i == 0:
                att_blocks.append(None)
                continue
            ref = cum[lo - 1:lo, :]
            k_t = jnp.concatenate([k[0:lo, :] * jnp.exp(ref - cum[0:lo, :]), jnp.zeros((C - lo, BRANCH_W), F32)], axis=0)
        else:
            if i == N_SUB - 1:
                att_blocks.append(None)
                continue
            ref = cum[hi:hi + 1, :]
            k_t = jnp.concatenate([jnp.zeros((hi, BRANCH_W), F32), k[hi:C, :] * jnp.exp(ref - cum[hi:C, :])], axis=0)
        k_t = _bf(k_t)
        q_t = q[lo:hi, :] * jnp.exp(cum[lo:hi, :] - ref)
        q_bd = _bf(jnp.concatenate([q_t * masks[h] for h in range(N_HEADS)], axis=0))
        att = _dot_nt(q_bd, k_t)
        r = _dot(_bf(att), vb)
        att_blocks.append(_pick_heads([r[h * S:(h + 1) * S, :] for h in range(N_HEADS)]))

    ones_bd = ones_ref[...]
    half = S // 2
    half_rows = lax.broadcasted_iota(jnp.int32, (half, 1), 0)
    cum2 = cum * LOG2E
    o_blocks = []
    for i in range(N_SUB):
        lo, hi = i * S, (i + 1) * S
        cq, qq, kk_, vv = cum2[lo:hi, :], q[lo:hi, :], k[lo:hi, :], v[lo:hi, :]
        xs = []
        for s in range(S):
            parts = []
            for r0 in (0, half):
                if fwd:
                    everything, nothing = s <= r0, s > r0 + half - 1
                else:
                    everything, nothing = s >= r0 + half - 1, s < r0
                if nothing:
                    parts.append(jnp.zeros((half, BRANCH_W), F32))
                    continue
                d = cq[r0:r0 + half, :] - cq[s:s + 1, :]
                if not everything:
                    ok = (half_rows >= s - r0) if fwd else (half_rows <= s - r0)
                    d = jnp.where(ok, d, NEG)
                parts.append(qq[r0:r0 + half, :] * kk_[s:s + 1, :] * jnp.exp2(d))
            xs.append(_bf(jnp.concatenate(parts, axis=0)))
        z = _dot(jnp.concatenate(xs, axis=0), ones_bd)
        od_halves = [None, None]
        for s in range(S):
            for hidx, r0 in enumerate((0, half)):
                if (s > r0 + half - 1) if fwd else (s < r0):
                    continue
                term = z[s * S + r0:s * S + r0 + half, :] * vv[s:s + 1, :]
                od_halves[hidx] = term if od_halves[hidx] is None else od_halves[hidx] + term
        od = jnp.concatenate(od_halves, axis=0)
        if att_blocks[i] is not None:
            od = od + att_blocks[i]
        o_blocks.append(od)
    o = o + jnp.concatenate(o_blocks, axis=0)

    k_dec = _bf(k * jnp.exp(edge - cum))
    st_ref[...] = st * jnp.exp(edge) + _dot_tn(vb, k_dec) * bd_ref[...]
    return o


def _hg_kernel(*refs, fwd):
    if fwd:
        f_ref, q_ref, i_ref, lb_ref, tri_ref, bd_ref, ones_ref, o_ref, st_ref, q_scr, k_scr, cum_scr = refs
    else:
        (f_ref, q_ref, i_ref, g_ref, of_ref, lb_ref, gn_ref, tri_ref, bd_ref, ones_ref, o_ref,
         st_ref, q_scr, k_scr, cum_scr, o_scr) = refs

    @pl.when(pl.program_id(1) == 0)
    def _():
        st_ref[...] = jnp.zeros_like(st_ref)

    nchunk = HG_TT // HG_CHUNK

    x = f_ref[0]
    lb = lb_ref[...]
    e = jnp.exp(-jnp.abs(x))
    inv = 1.0 / (1.0 + e)
    pos = x >= 0
    sig_p = jnp.where(pos, inv, e * inv)
    sig_n = jnp.where(pos, e * inv, inv)
    k_scr[...] = (1.0 - lb) * sig_n
    q_pre = q_ref[0]
    q_scr[...] = q_pre * _sigmoid(q_pre)
    parts = _split3(jnp.log(lb + (1.0 - lb) * sig_p))
    tri = tri_ref[...]
    for c in range(nchunk):
        rows = slice(c * HG_CHUNK, (c + 1) * HG_CHUNK)
        cum_scr[rows, :] = sum(_dot(tri, part[rows, :]) for part in parts)

    def chunk_body(ci, carry):
        c = ci if fwd else nchunk - 1 - ci
        rows = pl.ds(pl.multiple_of(c * HG_CHUNK, HG_CHUNK), HG_CHUNK)
        o = _hg_chunk(q_scr[rows, :], k_scr[rows, :], cum_scr[rows, :], i_ref[0, rows, :],
                      st_ref, bd_ref, ones_ref, fwd)
        if fwd:
            o_ref[0, rows, :] = o
        else:
            o_scr[rows, :] = o
        return carry

    lax.fori_loop(0, nchunk, chunk_body, 0, unroll=True)

    if not fwd:
        o = o_scr[...] + of_ref[0]
        sq = o * o
        hi = _bf(sq)
        lo = _bf(sq - hi.astype(F32))
        ms = (_dot(hi, ones_ref[...]) + _dot(lo, ones_ref[...])) * (1.0 / HEAD_DIM)
        g = g_ref[0]
        o_ref[0] = _bf(o * lax.rsqrt(ms + EPS) * gn_ref[...] * (g * _sigmoid(g)))


def _hg_call(hg, o_fwd, lb, gnorm, bd, ones_bd, fwd):
    B, L, _ = hg.shape
    nt = L // HG_TT
    tmap = (lambda t: t) if fwd else (lambda t: nt - 1 - t)
    col = lambda c: pl.BlockSpec((1, HG_TT, BRANCH_W), lambda b, t: (b, tmap(t), c))
    idx = np.arange(HG_CHUNK)
    tri = jnp.asarray((idx[None, :] <= idx[:, None]) if fwd else (idx[None, :] >= idx[:, None]), BF16)
    consts = [_const_spec((HG_CHUNK, HG_CHUNK)), _const_spec((BRANCH_W, BRANCH_W)), _const_spec((BRANCH_W, BRANCH_W))]
    vec = _const_spec((1, BRANCH_W))
    if fwd:
        in_specs = [col(0), col(2), col(3), vec] + consts
        args = [hg, hg, hg, lb, tri, bd, ones_bd]
        out_dtype = F32
    else:
        in_specs = [col(1), col(2), col(3), col(4), col(0), vec, vec] + consts
        args = [hg, hg, hg, hg, o_fwd, lb, gnorm, tri, bd, ones_bd]
        out_dtype = BF16
    return pl.pallas_call(
        functools.partial(_hg_kernel, fwd=fwd),
        out_shape=jax.ShapeDtypeStruct((B, L, BRANCH_W), out_dtype),
        grid=(B, nt),
        in_specs=in_specs,
        out_specs=col(0),
        scratch_shapes=[pltpu.VMEM((BRANCH_W, BRANCH_W), F32)]
        + [pltpu.VMEM((HG_TT, BRANCH_W), F32) for _ in range(3 if fwd else 4)],
        compiler_params=_params("parallel", "arbitrary"),
        name="hg_fwd" if fwd else "hg_bwd",
    )(*args)


MERGE_TM = 512


def _merge_kernel(x_ref, mod_ref, g_ref, na_ref, hg_ref, sg_ref, dl_ref, perm_ref, wg_ref, bg_ref, wb_ref, wo_ref, o_ref):
    x = x_ref[0]
    h = _bf(_mod_norm(x, g_ref[...], mod_ref[0, 1:2, :], mod_ref[0, 0:1, :]))
    dl_classes = jnp.concatenate([dl_ref[0, r] for r in range(dl_ref.shape[1])], axis=0)
    dl = _bf(_dot(perm_ref[...], dl_classes))
    merged = jnp.zeros((MERGE_TM, D_MODEL), F32)
    for bi, br in enumerate((na_ref[0], hg_ref[0], sg_ref[0], dl)):
        gate = _sigmoid(_dot(h, wg_ref[bi]) + bg_ref[bi:bi + 1, :])
        merged = merged + gate * _dot(br, wb_ref[bi])
    y = _dot(_bf(merged), wo_ref[...])
    o_ref[0] = x + mod_ref[0, 2:3, :] * y


def _merge_call(x, mod, g, branches, w_gate, b_gate, w_branch, w_o):
    B, L, _ = x.shape
    tm = MERGE_TM
    d16 = DIL_CONFIGS[2][1]
    tok_idx = np.arange(tm)
    perm = np.zeros((tm, tm), np.float32)
    perm[tok_idx, (tok_idx % d16) * (tm // d16) + tok_idx // d16] = 1.0
    tok = lambda w: pl.BlockSpec((1, tm, w), lambda b, t: (b, t, 0))
    return pl.pallas_call(
        _merge_kernel,
        out_shape=jax.ShapeDtypeStruct((B, L, D_MODEL), F32),
        grid=(B, L // tm),
        in_specs=[tok(D_MODEL),
                  pl.BlockSpec((1, 6, D_MODEL), lambda b, t: (b, 0, 0)),
                  _const_spec((1, D_MODEL)),
                  tok(BRANCH_W), tok(BRANCH_W), tok(BRANCH_W),
                  pl.BlockSpec((1, d16, tm // d16, BRANCH_W), lambda b, t: (b, 0, t, 0)),
                  _const_spec((tm, tm)),
                  _const_spec((N_BRANCH, D_MODEL, D_MODEL)),
                  _const_spec((N_BRANCH, D_MODEL)),
                  _const_spec((N_BRANCH, BRANCH_W, D_MODEL)),
                  _const_spec((D_MODEL, D_MODEL))],
        out_specs=tok(D_MODEL),
        compiler_params=_params("parallel", "arbitrary"),
        name="merge",
    )(x, mod, g, *branches, jnp.asarray(perm, BF16), w_gate, b_gate, w_branch, w_o)


FFN_TM = 1024
MXU_TILE = 256
FFN_CHUNKS = ((0, 3 * MXU_TILE), (3 * MXU_TILE, 6 * MXU_TILE), (6 * MXU_TILE, 9 * MXU_TILE), (9 * MXU_TILE, D_FF))


def _ffn_kernel(x_ref, mod_ref, g_ref, wg_ref, wu_ref, wd_ref, gf_ref, o_ref, *, final):
    x = x_ref[0]
    h = _bf(_mod_norm(x, g_ref[...], mod_ref[0, 4:5, :], mod_ref[0, 3:4, :]))
    y = jnp.zeros((FFN_TM, D_MODEL), F32)
    for lo, hi in FFN_CHUNKS:
        a = _dot(h, wg_ref[:, lo:hi])
        b = _dot(h, wu_ref[:, lo:hi])
        y = y + _dot(_bf(a * _sigmoid(a) * b), wd_ref[lo:hi, :])
    x = x + mod_ref[0, 5:6, :] * y
    if final:
        x = x * lax.rsqrt(jnp.mean(x * x, axis=-1, keepdims=True) + EPS) * gf_ref[...]
    o_ref[0] = x


def _ffn_call(x, mod, g, wg, wu, wd, g_final, final):
    B, L, _ = x.shape
    tm = FFN_TM
    tok = pl.BlockSpec((1, tm, D_MODEL), lambda b, t: (b, t, 0))
    return pl.pallas_call(
        functools.partial(_ffn_kernel, final=final),
        out_shape=jax.ShapeDtypeStruct((B, L, D_MODEL), F32),
        grid=(B, L // tm),
        in_specs=[tok,
                  pl.BlockSpec((1, 6, D_MODEL), lambda b, t: (b, 0, 0)),
                  _const_spec((1, D_MODEL)),
                  _const_spec((D_MODEL, D_FF)), _const_spec((D_MODEL, D_FF)), _const_spec((D_FF, D_MODEL)),
                  _const_spec((1, D_MODEL))],
        out_specs=tok,
        compiler_params=_params("parallel", "arbitrary"),
        name="ffn",
    )(x, mod, g, wg, wu, wd, g_final)


def _trunk(x, mods, p):
    for l in range(DEPTH):
        mod = mods[l]
        na_qkv, hg, y_sg, dl1, dl4, dl16 = _proj_call(x, mod, p["g_mix"][l], p["w_in"][l], p["sg_ln_g"][l],
                                                      p["sg_ln_b"][l], p["sg_w"][l], p["sg_b"][l])
        y_na = _na_call(na_qkv, p["na_bias"][l])
        o_f = _hg_call(hg, None, p["lb"][0][l], None, p["bd"], p["ones_bd"], True)
        y_hg = _hg_call(hg, o_f, p["lb"][1][l], p["hg_gnorm"][l], p["bd"], p["ones_bd"], False)
        y_dl = _dilated_attention(dl1, dl4, dl16)
        x = _merge_call(x, mod, p["g_mix"][l], (y_na, y_hg, y_sg, y_dl),
                        p["w_gate"][l], p["b_gate"][l], p["w_branch"][l], p["w_o"][l])
        x = _ffn_call(x, mod, p["g_ffn"][l], p["w_ffn_gate"][l], p["w_ffn_up"][l], p["w_ffn_down"][l],
                      p["g_final"], l == DEPTH - 1)
    return x


def kernel(x_prompt, x_sample, c_prompt, c_sample, w_ada, b_ada, g_norm_mix, g_norm_ffn, w_in, w_gate, b_gate,
           w_branch, w_o, na_rpb, hg_lb, hg_gnorm, sg_ln_g, sg_ln_b, sg_w, sg_b, w_ffn_gate, w_ffn_up,
           w_ffn_down, g_final):
    nbp, nbs = c_prompt.shape[0], c_sample.shape[0]
    nb_pad = -(-(nbp + nbs) // 16) * 16
    c_all = jnp.concatenate([c_prompt, c_sample, jnp.zeros((nb_pad - nbp - nbs, D_MODEL), F32)], axis=0)
    mod_all = _ada_call(c_all, w_ada, b_ada).reshape(DEPTH, nb_pad, 6, D_MODEL)
    mods_p = [mod_all[l, :nbp] for l in range(DEPTH)]
    mods_s = [mod_all[l, nbp:nbp + nbs] for l in range(DEPTH)]

    lb_soft = jax.nn.softmax(hg_lb.astype(F32), axis=1)
    lower = jnp.cumsum(lb_soft, axis=1) - lb_soft[:, :1]
    head = np.arange(BRANCH_W) // HEAD_DIM
    bd = jnp.asarray((head[:, None] == head[None, :]).astype(np.float32))
    vec = lambda a: [a[l].reshape(1, -1).astype(F32) for l in range(DEPTH)]
    p = {
        "g_mix": vec(g_norm_mix), "g_ffn": vec(g_norm_ffn),
        "w_in": [_bf(w_in[l]) for l in range(DEPTH)],
        "w_gate": [_bf(w_gate[l]) for l in range(DEPTH)],
        "b_gate": [b_gate[l] for l in range(DEPTH)],
        "w_branch": [_bf(w_branch[l]) for l in range(DEPTH)],
        "w_o": [_bf(w_o[l]) for l in range(DEPTH)],
        "na_bias": [_na_bias_table(na_rpb[l]) for l in range(DEPTH)],
        "lb": [[lower[d, l].reshape(1, BRANCH_W) for l in range(DEPTH)] for d in range(2)],
        "hg_gnorm": vec(hg_gnorm), "sg_ln_g": vec(sg_ln_g), "sg_ln_b": vec(sg_ln_b),
        "sg_w": [_bf(sg_w[l].transpose(1, 0, 2).reshape(SG_CHUNK, SG_GROUPS * SG_CHUNK)) for l in range(DEPTH)],
        "sg_b": [jnp.repeat(sg_b[l].T.astype(F32), BRANCH_W // SG_GROUPS, axis=1) for l in range(DEPTH)],
        "w_ffn_gate": [_bf(w_ffn_gate[l]) for l in range(DEPTH)],
        "w_ffn_up": [_bf(w_ffn_up[l]) for l in range(DEPTH)],
        "w_ffn_down": [_bf(w_ffn_down[l]) for l in range(DEPTH)],
        "g_final": g_final.reshape(1, D_MODEL).astype(F32),
        "bd": bd, "ones_bd": _bf(bd),
    }
    return (_trunk(x_prompt, mods_p, p), _trunk(x_sample, mods_s, p))
```

```python
import functools

import jax
import jax.numpy as jnp
import numpy as np
from jax import lax
from jax.experimental import pallas as pl
from jax.experimental.pallas import tpu as pltpu

D_MODEL = 1024
DEPTH = 4
GRID_W = 64
BRANCH_W = 256
N_BRANCH = 4
HEAD_DIM = 64
N_HEADS = BRANCH_W // HEAD_DIM
ATTN_SCALE = HEAD_DIM ** -0.5
NA_KH = 8
NA_KW = 16
HG_CHUNK = 64
HG_SUB = 16
SG_GROUPS = 4
SG_CHUNK = 128
DIL_CONFIGS = ((128, 1), (512, 4), (2048, 16))
DIL_BLK = 64
D_FF = 2816
IN_WIDTH = 3328
NEG = -1e30
EPS = 1e-6

NA_ROWS_PER_STEP = 32
NA_TOK = NA_ROWS_PER_STEP * GRID_W
NA_WIN = NA_KH * GRID_W

VMEM_LIMIT = 56 * 1024 * 1024

BF16 = jnp.bfloat16
F32 = jnp.float32


def _bf(x):
    return x.astype(BF16)


def _dot(a, b):
    return jnp.dot(a, b, preferred_element_type=F32)


def _dot_nt(a, b):
    return lax.dot_general(a, b, (((1,), (1,)), ((), ())), preferred_element_type=F32)


def _dot_tn(a, b):
    return lax.dot_general(a, b, (((0,), (0,)), ((), ())), preferred_element_type=F32)


def _sigmoid(x):
    return 1.0 / (1.0 + jnp.exp(-x))


def _head_masks(dtype):
    lane_head = lax.broadcasted_iota(jnp.int32, (1, BRANCH_W), 1) // HEAD_DIM
    return [(lane_head == h).astype(dtype) for h in range(N_HEADS)]


def _pick_heads(blocks):
    lane_head = lax.broadcasted_iota(jnp.int32, (1, BRANCH_W), 1) // HEAD_DIM
    out = blocks[N_HEADS - 1]
    for h in range(N_HEADS - 2, -1, -1):
        out = jnp.where(lane_head <= h, blocks[h], out)
    return out


def _mod_norm(x, g, scale, shift):
    y = x * lax.rsqrt(jnp.mean(x * x, axis=-1, keepdims=True) + EPS)
    return (y * g) * (1.0 + scale) + shift


def _params(*sem):
    return pltpu.CompilerParams(dimension_semantics=sem, vmem_limit_bytes=VMEM_LIMIT)


def _const_spec(shape):
    nd = len(shape)
    return pl.BlockSpec(shape, lambda *_: (0,) * nd, pipeline_mode=pl.Buffered(1))


ADA_TN = 1536

def _ada_kernel(c_ref, w_ref, b_ref, o_ref):
    c = c_ref[...]
    cs = c * _sigmoid(c)
    o_ref[0] = _dot(_bf(cs), _bf(w_ref[0])) + b_ref[0]


def _ada_call(c_all, w_ada, b_ada):
    nb = c_all.shape[0]
    tn = ADA_TN
    return pl.pallas_call(
        _ada_kernel,
        out_shape=jax.ShapeDtypeStruct((DEPTH, nb, 6 * D_MODEL), F32),
        grid=(DEPTH, 6 * D_MODEL // tn),
        in_specs=[pl.BlockSpec((nb, D_MODEL), lambda l, n: (0, 0)),
                  pl.BlockSpec((1, D_MODEL, tn), lambda l, n: (l, 0, n)),
                  pl.BlockSpec((1, 1, tn), lambda l, n: (l, 0, n))],
        out_specs=pl.BlockSpec((1, nb, tn), lambda l, n: (l, 0, n)),
        compiler_params=_params("arbitrary", "arbitrary"),
        name="ada",
    )(c_all, w_ada, b_ada.reshape(DEPTH, 1, 6 * D_MODEL))


PROJ_TM = 1024


LANES = 128


def _store_lane_chunks(scr_ref, val):
    for c in range(val.shape[1] // LANES):
        scr_ref[c] = val[:, c * LANES:(c + 1) * LANES]


def _load_class(scr_ref, r, n, dil):
    return jnp.concatenate([scr_ref[c, pl.ds(r, n, stride=dil), :] for c in range(scr_ref.shape[0])], axis=1)


def _gelu_tanh(x):
    return 0.5 * x * (1.0 + jnp.tanh(np.sqrt(2.0 / np.pi).astype(np.float32) * (x + 0.044715 * (x * x * x))))


def _spatial_gating(uv, ln_g, ln_b, w, b):
    lane_group = lax.broadcasted_iota(jnp.int32, (1, BRANCH_W), 1) // (BRANCH_W // SG_GROUPS)
    u = _gelu_tanh(uv[:, 0:BRANCH_W])
    v = _gelu_tanh(uv[:, BRANCH_W:2 * BRANCH_W])
    mu = jnp.mean(v, axis=-1, keepdims=True)
    var = jnp.mean(jnp.square(v - mu), axis=-1, keepdims=True)
    v = (v - mu) * lax.rsqrt(var + EPS) * ln_g + ln_b
    vstack = jnp.concatenate([_bf(jnp.where(lane_group == g, v, 0.0)) for g in range(SG_GROUPS)], axis=0)
    return u * (_dot(w, vstack) + b)


def _proj_kernel(x_ref, mod_ref, g_ref, w_ref, lng_ref, lnb_ref, sgw_ref, sgb_ref,
                 na_ref, hg_ref, sg_ref, dl1_ref, dl4_ref, dl16_ref, dl_scr):
    h = _bf(_mod_norm(x_ref[0], g_ref[...], mod_ref[0, 1:2, :], mod_ref[0, 0:1, :]))
    uv = _dot(h, w_ref[:, 2048:2560])
    dl = _dot(h, w_ref[:, 2560:3328])
    for n in range(PROJ_TM // SG_CHUNK):
        rows = slice(n * SG_CHUNK, (n + 1) * SG_CHUNK)
        sg_ref[0, rows, :] = _bf(_spatial_gating(uv[rows, :], lng_ref[...], lnb_ref[...], sgw_ref[...], sgb_ref[...]))
    dl1_ref[0] = _bf(dl)
    _store_lane_chunks(dl_scr, dl)
    for dil, ref in ((DIL_CONFIGS[1][1], dl4_ref), (DIL_CONFIGS[2][1], dl16_ref)):
        for r in range(dil):
            ref[0, r] = _bf(_load_class(dl_scr, r, PROJ_TM // dil, dil))
    na_ref[0] = _bf(_dot(h, w_ref[:, 0:768]))
    hg_ref[0] = _dot(h, w_ref[:, 768:2048])


def _proj_call(x, mod, g, w_in, sg_ln_g, sg_ln_b, sg_w, sg_b):
    B, L, _ = x.shape
    tm = PROJ_TM
    d4, d16 = DIL_CONFIGS[1][1], DIL_CONFIGS[2][1]
    tok = lambda w: pl.BlockSpec((1, tm, w), lambda b, t: (b, t, 0))
    cls = lambda d: pl.BlockSpec((1, d, tm // d, 768), lambda b, t: (b, 0, t, 0))
    return pl.pallas_call(
        _proj_kernel,
        out_shape=(jax.ShapeDtypeStruct((B, L, 768), BF16),
                   jax.ShapeDtypeStruct((B, L, 1280), F32),
                   jax.ShapeDtypeStruct((B, L, BRANCH_W), BF16),
                   jax.ShapeDtypeStruct((B, L, 768), BF16),
                   jax.ShapeDtypeStruct((B, d4, L // d4, 768), BF16),
                   jax.ShapeDtypeStruct((B, d16, L // d16, 768), BF16)),
        grid=(B, L // tm),
        in_specs=[tok(D_MODEL),
                  pl.BlockSpec((1, 6, D_MODEL), lambda b, t: (b, 0, 0)),
                  _const_spec((1, D_MODEL)),
                  _const_spec((D_MODEL, IN_WIDTH)),
                  _const_spec((1, BRANCH_W)), _const_spec((1, BRANCH_W)),
                  _const_spec((SG_CHUNK, SG_GROUPS * SG_CHUNK)),
                  _const_spec((SG_CHUNK, BRANCH_W))],
        out_specs=(tok(768), tok(1280), tok(BRANCH_W), tok(768), cls(d4), cls(d16)),
        scratch_shapes=[pltpu.VMEM((768 // LANES, tm, LANES), F32)],
        compiler_params=_params("parallel", "arbitrary"),
        name="proj",
    )(x, mod, g, w_in, sg_ln_g, sg_ln_b, sg_w, sg_b)


def _na_bias_table(rpb):
    c = np.arange(GRID_W)
    kc = np.arange(GRID_W)
    col_off = np.clip(kc[None, :] - c[:, None], -(NA_KW - 1), NA_KW - 1) + NA_KW - 1
    win0 = np.clip(c - NA_KW // 2, 0, GRID_W - NA_KW)
    valid = (kc[None, :] >= win0[:, None]) & (kc[None, :] < win0[:, None] + NA_KW)
    onehot = (col_off[None] == np.arange(2 * NA_KW - 1)[:, None, None]).astype(np.float32)
    cols = jnp.einsum("hrj,jck->hrck", rpb.astype(F32), jnp.asarray(onehot), precision=lax.Precision.HIGHEST)
    cols = jnp.where(valid[None, None], cols, NEG)
    pats = jnp.stack([cols[:, NA_KH - 1 - d:2 * NA_KH - 1 - d] for d in range(NA_KH)])
    return pats.transpose(0, 1, 3, 2, 4).reshape(NA_KH, N_HEADS * GRID_W, NA_WIN)


def _na_kernel(q_ref, k_ref, v_ref, bias_ref, o_ref, *, rows):
    i = pl.program_id(1)
    qmasks = [m * ATTN_SCALE for m in _head_masks(BF16)]

    def row_body(j, carry):
        r = i * NA_ROWS_PER_STEP + j
        start = jnp.clip(r - NA_KH // 2, 0, rows - NA_KH)
        off = pl.multiple_of(start * GRID_W, GRID_W)
        delta = r - start
        qoff = pl.multiple_of(j * GRID_W, GRID_W)
        q = q_ref[0, pl.ds(qoff, GRID_W), :]
        kw = k_ref[0, pl.ds(off, NA_WIN), :]
        vw = v_ref[0, pl.ds(off, NA_WIN), :]
        q_heads = jnp.concatenate([q * qmasks[h] for h in range(N_HEADS)], axis=0)
        s = _dot_nt(q_heads, kw) + bias_ref[delta]
        m = jnp.max(s, axis=-1, keepdims=True)
        e = jnp.exp(s - m)
        den = jnp.sum(e, axis=-1, keepdims=True)
        r_all = _dot(_bf(e), vw) * (1.0 / den)
        out = _pick_heads([r_all[h * GRID_W:(h + 1) * GRID_W, :] for h in range(N_HEADS)])
        o_ref[0, pl.ds(qoff, GRID_W), :] = _bf(out)
        return carry

    lax.fori_loop(0, NA_ROWS_PER_STEP, row_body, 0, unroll=True)


def _na_call(na_qkv, bias_tab):
    B, L, _ = na_qkv.shape
    rows = L // GRID_W
    seq = lambda col: pl.BlockSpec((1, L, BRANCH_W), lambda b, i: (b, 0, col))
    return pl.pallas_call(
        functools.partial(_na_kernel, rows=rows),
        out_shape=jax.ShapeDtypeStruct((B, L, BRANCH_W), BF16),
        grid=(B, rows // NA_ROWS_PER_STEP),
        in_specs=[pl.BlockSpec((1, NA_TOK, BRANCH_W), lambda b, i: (b, i, 0)), seq(1), seq(2),
                  _const_spec((NA_KH, N_HEADS * GRID_W, NA_WIN))],
        out_specs=pl.BlockSpec((1, NA_TOK, BRANCH_W), lambda b, i: (b, i, 0)),
        compiler_params=_params("parallel", "arbitrary"),
        name="na",
    )(na_qkv, na_qkv, na_qkv, bias_tab)


DL_TQ = 128
DL_TQB = 2048


def _alibi_slopes(n):
    start = 2.0 ** (-8.0 / n)
    return [float(np.float32(start ** (h + 1))) for h in range(n)]


def _dl_kernel(*refs, n, kw_len, tqb, cg, first, last, split):
    if first:
        q_ref, k_ref, v_ref, bias_ref = refs[:4]
        rest = refs[4:]
    else:
        q_ref, k_ref, v_ref, bias_ref, op_ref, lsep_ref = refs[:6]
        rest = refs[6:]
    t = pl.program_id(2)
    qmasks = [m * ATTN_SCALE for m in _head_masks(BF16)]
    nlane = BRANCH_W // LANES

    for c in range(cg):
        def tile_body(u, carry, c=c):
            base = t * tqb + u * DL_TQ
            ws = pl.multiple_of(jnp.clip(base - DIL_BLK, 0, n - kw_len), DIL_BLK)
            rows = pl.ds(pl.multiple_of(u * DL_TQ, DL_TQ), DL_TQ)
            q = q_ref[0, c, rows, :]
            kw = k_ref[0, c, pl.ds(ws, kw_len), :]
            vw = v_ref[0, c, pl.ds(ws, kw_len), :]
            q_heads = jnp.concatenate([q * qmasks[h] for h in range(N_HEADS)], axis=0)
            s = _dot_nt(q_heads, kw) + bias_ref[(base - ws) // DIL_BLK]
            m = jnp.max(s, axis=-1, keepdims=True)
            e = jnp.exp(s - m)
            den = jnp.sum(e, axis=-1, keepdims=True)
            r_all = _dot(_bf(e), vw) * (1.0 / den)
            lse_col = m + jnp.log(den)
            o = _pick_heads([r_all[h * DL_TQ:(h + 1) * DL_TQ, :] for h in range(N_HEADS)])
            lse = _pick_heads([lse_col[h * DL_TQ:(h + 1) * DL_TQ, :] for h in range(N_HEADS)])
            if not first:
                lse_p = lsep_ref[0, c, rows, :]
                lse_m = jnp.maximum(lse_p, lse)
                w_p = jnp.exp(lse_p - lse_m)
                w_c = jnp.exp(lse - lse_m)
                tot = w_p + w_c
                o = (w_p * op_ref[0, c, rows, :] + w_c * o) / tot
                lse = lse_m + jnp.log(tot)
            if last:
                rest[0][0, c, rows, :] = _bf(o)
            else:
                srows = pl.ds(pl.multiple_of(c * tqb + u * DL_TQ, DL_TQ), DL_TQ)
                for val, scr in ((o, rest[2]), (lse, rest[3])):
                    for lc in range(nlane):
                        scr[lc, srows, :] = val[:, lc * LANES:(lc + 1) * LANES]
            return carry

        lax.fori_loop(0, tqb // DL_TQ, tile_body, 0, unroll=True)

    if not last:
        for out_ref, scr in ((rest[0], rest[2]), (rest[1], rest[3])):
            for c in range(cg):
                for a in range(split):
                    out_ref[0, a, c] = jnp.concatenate(
                        [scr[lc, pl.ds(c * tqb + a, tqb // split, stride=split), :] for lc in range(nlane)], axis=1)


def _dl_bias_table(dil, kw_len):
    slopes = np.asarray(_alibi_slopes(N_HEADS), np.float32)
    qi = np.arange(DL_TQ)[None, :, None]
    ki = np.arange(kw_len)[None, None, :] - DIL_BLK * np.arange(3)[:, None, None]
    step = np.abs(ki - qi)
    dist = (step * dil).astype(np.float32)
    bias = np.where((step <= DIL_BLK)[:, None], -slopes[None, :, None, None] * dist[:, None], np.float32(NEG))
    return jnp.asarray(bias.reshape(3, N_HEADS * DL_TQ, kw_len), F32)


def _dl_call(qkv, state, dil, first, last, split):
    B, _, n, _ = qkv.shape
    kw_len = min(4 * DIL_BLK, n)
    tqb = min(DL_TQB, n)
    cg = min(dil, max(1, DL_TQB // tqb))
    tok = lambda c: pl.BlockSpec((1, cg, tqb, BRANCH_W), lambda b, r, t: (b, r, t, c))
    full = lambda c: pl.BlockSpec((1, cg, n, BRANCH_W), lambda b, r, t: (b, r, 0, c))
    in_specs = [tok(0), full(1), full(2), _const_spec((3, N_HEADS * DL_TQ, kw_len))]
    args = [qkv, qkv, qkv, _dl_bias_table(dil, kw_len)]
    if not first:
        in_specs += [tok(0), tok(0)]
        args += list(state)
    scratch = []
    if last:
        out_shape = jax.ShapeDtypeStruct((B, dil, n, BRANCH_W), BF16)
        out_specs = tok(0)
    else:
        shape = (B, split, dil, n // split, BRANCH_W)
        spec = lambda: pl.BlockSpec((1, split, cg, tqb // split, BRANCH_W), lambda b, r, t: (b, 0, r, t, 0))
        out_shape = (jax.ShapeDtypeStruct(shape, F32), jax.ShapeDtypeStruct(shape, F32))
        out_specs = (spec(), spec())
        scratch = [pltpu.VMEM((BRANCH_W // LANES, cg * tqb, LANES), F32) for _ in range(2)]
    out = pl.pallas_call(
        functools.partial(_dl_kernel, n=n, kw_len=kw_len, tqb=tqb, cg=cg, first=first, last=last, split=split),
        out_shape=out_shape,
        grid=(B, dil // cg, n // tqb),
        in_specs=in_specs,
        out_specs=out_specs,
        scratch_shapes=scratch,
        compiler_params=_params("parallel", "arbitrary", "arbitrary"),
        name=f"dl{dil}",
    )(*args)
    if last:
        return out
    return [o.reshape(B, split * dil, n // split, BRANCH_W) for o in out]


def _dilated_attention(dl1, dl4, dl16):
    (_, d1), (_, d4), (_, d16) = DIL_CONFIGS
    state = _dl_call(dl1[:, None], None, d1, True, False, d4 // d1)
    state = _dl_call(dl4, state, d4, False, False, d16 // d4)
    return _dl_call(dl16, state, d16, False, True, 1)


HG_TT = 1024
LOG2E = float(np.log2(np.e))
N_SUB = HG_CHUNK // HG_SUB


def _split3(x):
    hi = _bf(x)
    r1 = x - hi.astype(F32)
    mid = _bf(r1)
    lo = _bf(r1 - mid.astype(F32))
    return hi, mid, lo


def _hg_chunk(q, k, cum, v, st_ref, bd_ref, ones_ref, fwd):
    C, S = HG_CHUNK, HG_SUB
    edge = cum[C - 1:C, :] if fwd else cum[0:1, :]
    vb = _bf(v)
    masks = _head_masks(F32)

    st = st_ref[...]
    o = _dot_nt(_bf(q * jnp.exp(cum)), _bf(st))

    att_blocks = []
    for i in range(N_SUB):
        lo, hi = i * S, (i + 1) * S
        if fwd:
            if i == 0:
                att_blocks.append(None)
                continue
            ref = cum[lo - 1:lo, :]
            k_t = jnp.concatenate([k[0:lo, :] * jnp.exp(ref - cum[0:lo, :]), jnp.zeros((C - lo, BRANCH_W), F32)], axis=0)
        else:
            if i == N_SUB - 1:
                att_blocks.append(None)
                continue
            ref = cum[hi:hi + 1, :]
            k_t = jnp.concatenate([jnp.zeros((hi, BRANCH_W), F32), k[hi:C, :] * jnp.exp(ref - cum[hi:C, :])], axis=0)
        k_t = _bf(k_t)
        q_t = q[lo:hi, :] * jnp.exp(cum[lo:hi, :] - ref)
        q_bd = _bf(jnp.concatenate([q_t * masks[h] for h in range(N_HEADS)], axis=0))
        att = _dot_nt(q_bd, k_t)
        r = _dot(_bf(att), vb)
        att_blocks.append(_pick_heads([r[h * S:(h + 1) * S, :] for h in range(N_HEADS)]))

    ones_bd = ones_ref[...]
    half = S // 2
    half_rows = lax.broadcasted_iota(jnp.int32, (half, 1), 0)
    cum2 = cum * LOG2E
    o_blocks = []
    for i in range(N_SUB):
        lo, hi = i * S, (i + 1) * S
        cq, qq, kk_, vv = cum2[lo:hi, :], q[lo:hi, :], k[lo:hi, :], v[lo:hi, :]
        xs = []
        for s in range(S):
            parts = []
            for r0 in (0, half):
                if fwd:
                    everything, nothing = s <= r0, s > r0 + half - 1
                else:
                    everything, nothing = s >= r0 + half - 1, s < r0
                if nothing:
                    parts.append(jnp.zeros((half, BRANCH_W), F32))
                    continue
                d = cq[r0:r0 + half, :] - cq[s:s + 1, :]
                if not everything:
                    ok = (half_rows >= s - r0) if fwd else (half_rows <= s - r0)
                    d = jnp.where(ok, d, NEG)
                parts.append(qq[r0:r0 + half, :] * kk_[s:s + 1, :] * jnp.exp2(d))
            xs.append(_bf(jnp.concatenate(parts, axis=0)))
        z = _dot(jnp.concatenate(xs, axis=0), ones_bd)
        od_halves = [None, None]
        for s in range(S):
            for hidx, r0 in enumerate((0, half)):
                if (s > r0 + half - 1) if fwd else (s < r0):
                    continue
                term = z[s * S + r0:s * S + r0 + half, :] * vv[s:s + 1, :]
                od_halves[hidx] = term if od_halves[hidx] is None else od_halves[hidx] + term
        od = jnp.concatenate(od_halves, axis=0)
        if att_blocks[i] is not None:
            od = od + att_blocks[i]
        o_blocks.append(od)
    o = o + jnp.concatenate(o_blocks, axis=0)

    k_dec = _bf(k * jnp.exp(edge - cum))
    st_ref[...] = st * jnp.exp(edge) + _dot_tn(vb, k_dec) * bd_ref[...]
    return o


def _hg_kernel(*refs, fwd):
    if fwd:
        f_ref, q_ref, i_ref, lb_ref, tri_ref, bd_ref, ones_ref, o_ref, st_ref, q_scr, k_scr, cum_scr = refs
    else:
        (f_ref, q_ref, i_ref, g_ref, of_ref, lb_ref, gn_ref, tri_ref, bd_ref, ones_ref, o_ref,
         st_ref, q_scr, k_scr, cum_scr, o_scr) = refs

    @pl.when(pl.program_id(1) == 0)
    def _():
        st_ref[...] = jnp.zeros_like(st_ref)

    nchunk = HG_TT // HG_CHUNK

    x = f_ref[0]
    lb = lb_ref[...]
    e = jnp.exp(-jnp.abs(x))
    inv = 1.0 / (1.0 + e)
    pos = x >= 0
    sig_p = jnp.where(pos, inv, e * inv)
    sig_n = jnp.where(pos, e * inv, inv)
    k_scr[...] = (1.0 - lb) * sig_n
    q_pre = q_ref[0]
    q_scr[...] = q_pre * _sigmoid(q_pre)
    parts = _split3(jnp.log(lb + (1.0 - lb) * sig_p))
    tri = tri_ref[...]
    for c in range(nchunk):
        rows = slice(c * HG_CHUNK, (c + 1) * HG_CHUNK)
        cum_scr[rows, :] = sum(_dot(tri, part[rows, :]) for part in parts)

    def chunk_body(ci, carry):
        c = ci if fwd else nchunk - 1 - ci
        rows = pl.ds(pl.multiple_of(c * HG_CHUNK, HG_CHUNK), HG_CHUNK)
        o = _hg_chunk(q_scr[rows, :], k_scr[rows, :], cum_scr[rows, :], i_ref[0, rows, :],
                      st_ref, bd_ref, ones_ref, fwd)
        if fwd:
            o_ref[0, rows, :] = o
        else:
            o_scr[rows, :] = o
        return carry

    lax.fori_loop(0, nchunk, chunk_body, 0, unroll=True)

    if not fwd:
        o = o_scr[...] + of_ref[0]
        sq = o * o
        hi = _bf(sq)
        lo = _bf(sq - hi.astype(F32))
        ms = (_dot(hi, ones_ref[...]) + _dot(lo, ones_ref[...])) * (1.0 / HEAD_DIM)
        g = g_ref[0]
        o_ref[0] = _bf(o * lax.rsqrt(ms + EPS) * gn_ref[...] * (g * _sigmoid(g)))


def _hg_call(hg, o_fwd, lb, gnorm, bd, ones_bd, fwd):
    B, L, _ = hg.shape
    nt = L // HG_TT
    tmap = (lambda t: t) if fwd else (lambda t: nt - 1 - t)
    col = lambda c: pl.BlockSpec((1, HG_TT, BRANCH_W), lambda b, t: (b, tmap(t), c))
    idx = np.arange(HG_CHUNK)
    tri = jnp.asarray((idx[None, :] <= idx[:, None]) if fwd else (idx[None, :] >= idx[:, None]), BF16)
    consts = [_const_spec((HG_CHUNK, HG_CHUNK)), _const_spec((BRANCH_W, BRANCH_W)), _const_spec((BRANCH_W, BRANCH_W))]
    vec = _const_spec((1, BRANCH_W))
    if fwd:
        in_specs = [col(0), col(2), col(3), vec] + consts
        args = [hg, hg, hg, lb, tri, bd, ones_bd]
        out_dtype = F32
    else:
        in_specs = [col(1), col(2), col(3), col(4), col(0), vec, vec] + consts
        args = [hg, hg, hg, hg, o_fwd, lb, gnorm, tri, bd, ones_bd]
        out_dtype = BF16
    return pl.pallas_call(
        functools.partial(_hg_kernel, fwd=fwd),
        out_shape=jax.ShapeDtypeStruct((B, L, BRANCH_W), out_dtype),
        grid=(B, nt),
        in_specs=in_specs,
        out_specs=col(0),
        scratch_shapes=[pltpu.VMEM((BRANCH_W, BRANCH_W), F32)]
        + [pltpu.VMEM((HG_TT, BRANCH_W), F32) for _ in range(3 if fwd else 4)],
        compiler_params=_params("parallel", "arbitrary"),
        name="hg_fwd" if fwd else "hg_bwd",
    )(*args)


MERGE_TM = 512


def _merge_kernel(x_ref, mod_ref, g_ref, na_ref, hg_ref, sg_ref, dl_ref, perm_ref, wg_ref, bg_ref, wb_ref, wo_ref, o_ref):
    x = x_ref[0]
    h = _bf(_mod_norm(x, g_ref[...], mod_ref[0, 1:2, :], mod_ref[0, 0:1, :]))
    dl_classes = jnp.concatenate([dl_ref[0, r] for r in range(dl_ref.shape[1])], axis=0)
    dl = _bf(_dot(perm_ref[...], dl_classes))
    merged = jnp.zeros((MERGE_TM, D_MODEL), F32)
    for bi, br in enumerate((na_ref[0], hg_ref[0], sg_ref[0], dl)):
        gate = _sigmoid(_dot(h, wg_ref[bi]) + bg_ref[bi:bi + 1, :])
        merged = merged + gate * _dot(br, wb_ref[bi])
    y = _dot(_bf(merged), wo_ref[...])
    o_ref[0] = x + mod_ref[0, 2:3, :] * y


def _merge_call(x, mod, g, branches, w_gate, b_gate, w_branch, w_o):
    B, L, _ = x.shape
    tm = MERGE_TM
    d16 = DIL_CONFIGS[2][1]
    tok_idx = np.arange(tm)
    perm = np.zeros((tm, tm), np.float32)
    perm[tok_idx, (tok_idx % d16) * (tm // d16) + tok_idx // d16] = 1.0
    tok = lambda w: pl.BlockSpec((1, tm, w), lambda b, t: (b, t, 0))
    return pl.pallas_call(
        _merge_kernel,
        out_shape=jax.ShapeDtypeStruct((B, L, D_MODEL), F32),
        grid=(B, L // tm),
        in_specs=[tok(D_MODEL),
                  pl.BlockSpec((1, 6, D_MODEL), lambda b, t: (b, 0, 0)),
                  _const_spec((1, D_MODEL)),
                  tok(BRANCH_W), tok(BRANCH_W), tok(BRANCH_W),
                  pl.BlockSpec((1, d16, tm // d16, BRANCH_W), lambda b, t: (b, 0, t, 0)),
                  _const_spec((tm, tm)),
                  _const_spec((N_BRANCH, D_MODEL, D_MODEL)),
                  _const_spec((N_BRANCH, D_MODEL)),
                  _const_spec((N_BRANCH, BRANCH_W, D_MODEL)),
                  _const_spec((D_MODEL, D_MODEL))],
        out_specs=tok(D_MODEL),
        compiler_params=_params("parallel", "arbitrary"),
        name="merge",
    )(x, mod, g, *branches, jnp.asarray(perm, BF16), w_gate, b_gate, w_branch, w_o)


FFN_TM = 1024
MXU_TILE = 256
FFN_CHUNKS = ((0, 3 * MXU_TILE), (3 * MXU_TILE, 6 * MXU_TILE), (6 * MXU_TILE, 9 * MXU_TILE), (9 * MXU_TILE, D_FF))


def _ffn_kernel(x_ref, mod_ref, g_ref, wg_ref, wu_ref, wd_ref, gf_ref, o_ref, *, final):
    x = x_ref[0]
    h = _bf(_mod_norm(x, g_ref[...], mod_ref[0, 4:5, :], mod_ref[0, 3:4, :]))
    y = jnp.zeros((FFN_TM, D_MODEL), F32)
    for lo, hi in FFN_CHUNKS:
        a = _dot(h, wg_ref[:, lo:hi])
        b = _dot(h, wu_ref[:, lo:hi])
        y = y + _dot(_bf(a * _sigmoid(a) * b), wd_ref[lo:hi, :])
    x = x + mod_ref[0, 5:6, :] * y
    if final:
        x = x * lax.rsqrt(jnp.mean(x * x, axis=-1, keepdims=True) + EPS) * gf_ref[...]
    o_ref[0] = x


def _ffn_call(x, mod, g, wg, wu, wd, g_final, final):
    B, L, _ = x.shape
    tm = FFN_TM
    tok = pl.BlockSpec((1, tm, D_MODEL), lambda b, t: (b, t, 0))
    return pl.pallas_call(
        functools.partial(_ffn_kernel, final=final),
        out_shape=jax.ShapeDtypeStruct((B, L, D_MODEL), F32),
        grid=(B, L // tm),
        in_specs=[tok,
                  pl.BlockSpec((1, 6, D_MODEL), lambda b, t: (b, 0, 0)),
                  _const_spec((1, D_MODEL)),
                  _const_spec((D_MODEL, D_FF)), _const_spec((D_MODEL, D_FF)), _const_spec((D_FF, D_MODEL)),
                  _const_spec((1, D_MODEL))],
        out_specs=tok,
        compiler_params=_params("parallel", "arbitrary"),
        name="ffn",
    )(x, mod, g, wg, wu, wd, g_final)


def _trunk(x, mods, p):
    B, L, D = x.shape
    assert D == D_MODEL and x.dtype == F32 and mods[0].shape == (B, 6, D_MODEL)
    assert L % max(PROJ_TM, MERGE_TM, FFN_TM, HG_TT, NA_TOK) == 0
    assert (L // DIL_CONFIGS[-1][1]) % DL_TQ == 0
    for l in range(DEPTH):
        mod = mods[l]
        na_qkv, hg, y_sg, dl1, dl4, dl16 = _proj_call(x, mod, p["g_mix"][l], p["w_in"][l], p["sg_ln_g"][l],
                                                      p["sg_ln_b"][l], p["sg_w"][l], p["sg_b"][l])
        y_na = _na_call(na_qkv, p["na_bias"][l])
        o_f = _hg_call(hg, None, p["lb"][0][l], None, p["bd"], p["ones_bd"], True)
        y_hg = _hg_call(hg, o_f, p["lb"][1][l], p["hg_gnorm"][l], p["bd"], p["ones_bd"], False)
        y_dl = _dilated_attention(dl1, dl4, dl16)
        x = _merge_call(x, mod, p["g_mix"][l], (y_na, y_hg, y_sg, y_dl),
                        p["w_gate"][l], p["b_gate"][l], p["w_branch"][l], p["w_o"][l])
        x = _ffn_call(x, mod, p["g_ffn"][l], p["w_ffn_gate"][l], p["w_ffn_up"][l], p["w_ffn_down"][l],
                      p["g_final"], l == DEPTH - 1)
    return x


def kernel(x_prompt, x_sample, c_prompt, c_sample, w_ada, b_ada, g_norm_mix, g_norm_ffn, w_in, w_gate, b_gate,
           w_branch, w_o, na_rpb, hg_lb, hg_gnorm, sg_ln_g, sg_ln_b, sg_w, sg_b, w_ffn_gate, w_ffn_up,
           w_ffn_down, g_final):
    nbp, nbs = c_prompt.shape[0], c_sample.shape[0]
    nb_pad = -(-(nbp + nbs) // 16) * 16
    c_all = jnp.concatenate([c_prompt, c_sample, jnp.zeros((nb_pad - nbp - nbs, D_MODEL), F32)], axis=0)
    mod_all = _ada_call(c_all, w_ada, b_ada).reshape(DEPTH, nb_pad, 6, D_MODEL)
    mods_p = [mod_all[l, :nbp] for l in range(DEPTH)]
    mods_s = [mod_all[l, nbp:nbp + nbs] for l in range(DEPTH)]

    lb_soft = jax.nn.softmax(hg_lb.astype(F32), axis=1)
    lower = jnp.cumsum(lb_soft, axis=1) - lb_soft[:, :1]
    head = np.arange(BRANCH_W) // HEAD_DIM
    bd = jnp.asarray((head[:, None] == head[None, :]).astype(np.float32))
    vec = lambda a: [a[l].reshape(1, -1).astype(F32) for l in range(DEPTH)]
    p = {
        "g_mix": vec(g_norm_mix), "g_ffn": vec(g_norm_ffn),
        "w_in": [_bf(w_in[l]) for l in range(DEPTH)],
        "w_gate": [_bf(w_gate[l]) for l in range(DEPTH)],
        "b_gate": [b_gate[l] for l in range(DEPTH)],
        "w_branch": [_bf(w_branch[l]) for l in range(DEPTH)],
        "w_o": [_bf(w_o[l]) for l in range(DEPTH)],
        "na_bias": [_na_bias_table(na_rpb[l]) for l in range(DEPTH)],
        "lb": [[lower[d, l].reshape(1, BRANCH_W) for l in range(DEPTH)] for d in range(2)],
        "hg_gnorm": vec(hg_gnorm), "sg_ln_g": vec(sg_ln_g), "sg_ln_b": vec(sg_ln_b),
        "sg_w": [_bf(sg_w[l].transpose(1, 0, 2).reshape(SG_CHUNK, SG_GROUPS * SG_CHUNK)) for l in range(DEPTH)],
        "sg_b": [jnp.repeat(sg_b[l].T.astype(F32), BRANCH_W // SG_GROUPS, axis=1) for l in range(DEPTH)],
        "w_ffn_gate": [_bf(w_ffn_gate[l]) for l in range(DEPTH)],
        "w_ffn_up": [_bf(w_ffn_up[l]) for l in range(DEPTH)],
        "w_ffn_down": [_bf(w_ffn_down[l]) for l in range(DEPTH)],
        "g_final": g_final.reshape(1, D_MODEL).astype(F32),
        "bd": bd, "ones_bd": _bf(bd),
    }
    return (_trunk(x_prompt, mods_p, p), _trunk(x_sample, mods_s, p))
```

```python
import functools

import jax
import jax.numpy as jnp
import numpy as np
from jax import lax
from jax.experimental import pallas as pl
from jax.experimental.pallas import tpu as pltpu

D_MODEL = 1024
DEPTH = 4
GRID_W = 64
BRANCH_W = 256
N_BRANCH = 4
HEAD_DIM = 64
N_HEADS = BRANCH_W // HEAD_DIM
ATTN_SCALE = HEAD_DIM ** -0.5
NA_KH = 8
NA_KW = 16
HG_CHUNK = 64
HG_SUB = 16
SG_GROUPS = 4
SG_CHUNK = 128
DIL_CONFIGS = ((128, 1), (512, 4), (2048, 16))
DIL_BLK = 64
D_FF = 2816
IN_WIDTH = 3328
NEG = -1e30
EPS = 1e-6

NA_ROWS_PER_STEP = 32
NA_TOK = NA_ROWS_PER_STEP * GRID_W
NA_WIN = NA_KH * GRID_W

VMEM_LIMIT = 56 * 1024 * 1024

BF16 = jnp.bfloat16
F32 = jnp.float32


def _bf(x):
    return x.astype(BF16)


def _dot(a, b):
    return jnp.dot(a, b, preferred_element_type=F32)


def _dot_nt(a, b):
    return lax.dot_general(a, b, (((1,), (1,)), ((), ())), preferred_element_type=F32)


def _dot_tn(a, b):
    return lax.dot_general(a, b, (((0,), (0,)), ((), ())), preferred_element_type=F32)


def _sigmoid(x):
    return 1.0 / (1.0 + jnp.exp(-x))


def _head_masks(dtype):
    lane_head = lax.broadcasted_iota(jnp.int32, (1, BRANCH_W), 1) // HEAD_DIM
    return [(lane_head == h).astype(dtype) for h in range(N_HEADS)]


def _pick_heads(blocks):
    lane_head = lax.broadcasted_iota(jnp.int32, (1, BRANCH_W), 1) // HEAD_DIM
    out = blocks[N_HEADS - 1]
    for h in range(N_HEADS - 2, -1, -1):
        out = jnp.where(lane_head <= h, blocks[h], out)
    return out


def _mod_norm(x, g, scale, shift):
    y = x * lax.rsqrt(jnp.mean(x * x, axis=-1, keepdims=True) + EPS)
    return (y * g) * (1.0 + scale) + shift


def _params(*sem):
    return pltpu.CompilerParams(dimension_semantics=sem, vmem_limit_bytes=VMEM_LIMIT)


def _const_spec(shape):
    nd = len(shape)
    return pl.BlockSpec(shape, lambda *_: (0,) * nd, pipeline_mode=pl.Buffered(1))


def _layer_spec(shape, layer):
    nd = len(shape)
    return pl.BlockSpec((None,) + tuple(shape), lambda *_: (layer,) + (0,) * nd, pipeline_mode=pl.Buffered(1))


ADA_TN = 1536

def _ada_kernel(c_ref, w_ref, b_ref, o_ref):
    c = c_ref[...]
    cs = c * _sigmoid(c)
    o_ref[0] = _dot(_bf(cs), _bf(w_ref[0])) + b_ref[0]


def _ada_call(c_all, w_ada, b_ada):
    nb = c_all.shape[0]
    tn = ADA_TN
    return pl.pallas_call(
        _ada_kernel,
        out_shape=jax.ShapeDtypeStruct((DEPTH, nb, 6 * D_MODEL), F32),
        grid=(DEPTH, 6 * D_MODEL // tn),
        in_specs=[pl.BlockSpec((nb, D_MODEL), lambda l, n: (0, 0)),
                  pl.BlockSpec((1, D_MODEL, tn), lambda l, n: (l, 0, n)),
                  pl.BlockSpec((1, 1, tn), lambda l, n: (l, 0, n))],
        out_specs=pl.BlockSpec((1, nb, tn), lambda l, n: (l, 0, n)),
        compiler_params=_params("arbitrary", "arbitrary"),
        name="ada",
    )(c_all, w_ada, b_ada.reshape(DEPTH, 1, 6 * D_MODEL))


PROJ_TM = 1024


LANES = 128


def _store_lane_chunks(scr_ref, val):
    for c in range(val.shape[1] // LANES):
        scr_ref[c] = val[:, c * LANES:(c + 1) * LANES]


def _load_class(scr_ref, r, n, dil):
    return jnp.concatenate([scr_ref[c, pl.ds(r, n, stride=dil), :] for c in range(scr_ref.shape[0])], axis=1)


def _gelu_tanh(x):
    return 0.5 * x * (1.0 + jnp.tanh(np.sqrt(2.0 / np.pi).astype(np.float32) * (x + 0.044715 * (x * x * x))))


def _spatial_gating(uv, ln_g, ln_b, w, b):
    lane_group = lax.broadcasted_iota(jnp.int32, (1, BRANCH_W), 1) // (BRANCH_W // SG_GROUPS)
    u = _gelu_tanh(uv[:, 0:BRANCH_W])
    v = _gelu_tanh(uv[:, BRANCH_W:2 * BRANCH_W])
    mu = jnp.mean(v, axis=-1, keepdims=True)
    var = jnp.mean(jnp.square(v - mu), axis=-1, keepdims=True)
    v = (v - mu) * lax.rsqrt(var + EPS) * ln_g + ln_b
    vstack = jnp.concatenate([_bf(jnp.where(lane_group == g, v, 0.0)) for g in range(SG_GROUPS)], axis=0)
    return u * (_dot(w, vstack) + b)


def _proj_kernel(x_ref, mod_ref, g_ref, w_ref, lng_ref, lnb_ref, sgw_ref, sgb_ref,
                 na_ref, hg_ref, sg_ref, dl1_ref, dl4_ref, dl16_ref, dl_scr):
    h = _bf(_mod_norm(x_ref[0], g_ref[...], mod_ref[0, 1:2, :], mod_ref[0, 0:1, :]))
    uv = _dot(h, w_ref[:, 2048:2560])
    dl = _dot(h, w_ref[:, 2560:3328])
    for n in range(PROJ_TM // SG_CHUNK):
        rows = slice(n * SG_CHUNK, (n + 1) * SG_CHUNK)
        sg_ref[0, rows, :] = _bf(_spatial_gating(uv[rows, :], lng_ref[...], lnb_ref[...], sgw_ref[...], sgb_ref[...]))
    dl1_ref[0] = _bf(dl)
    _store_lane_chunks(dl_scr, dl)
    for dil, ref in ((DIL_CONFIGS[1][1], dl4_ref), (DIL_CONFIGS[2][1], dl16_ref)):
        for r in range(dil):
            ref[0, r] = _bf(_load_class(dl_scr, r, PROJ_TM // dil, dil))
    na_ref[0] = _bf(_dot(h, w_ref[:, 0:768]))
    hg_ref[0] = _dot(h, w_ref[:, 768:2048])


def _proj_call(x, mod, g, w_in, layer, sg_ln_g, sg_ln_b, sg_w, sg_b):
    B, L, _ = x.shape
    tm = PROJ_TM
    d4, d16 = DIL_CONFIGS[1][1], DIL_CONFIGS[2][1]
    tok = lambda w: pl.BlockSpec((1, tm, w), lambda b, t: (b, t, 0))
    cls = lambda d: pl.BlockSpec((1, d, tm // d, 768), lambda b, t: (b, 0, t, 0))
    return pl.pallas_call(
        _proj_kernel,
        out_shape=(jax.ShapeDtypeStruct((B, L, 768), BF16),
                   jax.ShapeDtypeStruct((B, L, 1280), F32),
                   jax.ShapeDtypeStruct((B, L, BRANCH_W), BF16),
                   jax.ShapeDtypeStruct((B, L, 768), BF16),
                   jax.ShapeDtypeStruct((B, d4, L // d4, 768), BF16),
                   jax.ShapeDtypeStruct((B, d16, L // d16, 768), BF16)),
        grid=(B, L // tm),
        in_specs=[tok(D_MODEL),
                  pl.BlockSpec((1, 6, D_MODEL), lambda b, t: (b, 0, 0)),
                  _const_spec((1, D_MODEL)),
                  _layer_spec((D_MODEL, IN_WIDTH), layer),
                  _const_spec((1, BRANCH_W)), _const_spec((1, BRANCH_W)),
                  _const_spec((SG_CHUNK, SG_GROUPS * SG_CHUNK)),
                  _const_spec((SG_CHUNK, BRANCH_W))],
        out_specs=(tok(768), tok(1280), tok(BRANCH_W), tok(768), cls(d4), cls(d16)),
        scratch_shapes=[pltpu.VMEM((768 // LANES, tm, LANES), F32)],
        compiler_params=_params("parallel", "arbitrary"),
        name="proj",
    )(x, mod, g, w_in, sg_ln_g, sg_ln_b, sg_w, sg_b)


def _na_bias_table(rpb):
    c = np.arange(GRID_W)
    kc = np.arange(GRID_W)
    col_off = np.clip(kc[None, :] - c[:, None], -(NA_KW - 1), NA_KW - 1) + NA_KW - 1
    win0 = np.clip(c - NA_KW // 2, 0, GRID_W - NA_KW)
    valid = (kc[None, :] >= win0[:, None]) & (kc[None, :] < win0[:, None] + NA_KW)
    onehot = (col_off[None] == np.arange(2 * NA_KW - 1)[:, None, None]).astype(np.float32)
    cols = jnp.einsum("hrj,jck->hrck", rpb.astype(F32), jnp.asarray(onehot), precision=lax.Precision.HIGHEST)
    cols = jnp.where(valid[None, None], cols, NEG)
    pats = jnp.stack([cols[:, NA_KH - 1 - d:2 * NA_KH - 1 - d] for d in range(NA_KH)])
    return pats.transpose(0, 1, 3, 2, 4).reshape(NA_KH, N_HEADS * GRID_W, NA_WIN)


def _na_kernel(q_ref, k_ref, v_ref, bias_ref, o_ref, *, rows):
    i = pl.program_id(1)
    qmasks = [m * ATTN_SCALE for m in _head_masks(BF16)]

    def row_body(j, carry):
        r = i * NA_ROWS_PER_STEP + j
        start = jnp.clip(r - NA_KH // 2, 0, rows - NA_KH)
        off = pl.multiple_of(start * GRID_W, GRID_W)
        delta = r - start
        qoff = pl.multiple_of(j * GRID_W, GRID_W)
        q = q_ref[0, pl.ds(qoff, GRID_W), :]
        kw = k_ref[0, pl.ds(off, NA_WIN), :]
        vw = v_ref[0, pl.ds(off, NA_WIN), :]
        q_heads = jnp.concatenate([q * qmasks[h] for h in range(N_HEADS)], axis=0)
        s = _dot_nt(q_heads, kw) + bias_ref[delta]
        m = jnp.max(s, axis=-1, keepdims=True)
        e = jnp.exp(s - m)
        den = jnp.sum(e, axis=-1, keepdims=True)
        r_all = _dot(_bf(e), vw) * (1.0 / den)
        out = _pick_heads([r_all[h * GRID_W:(h + 1) * GRID_W, :] for h in range(N_HEADS)])
        o_ref[0, pl.ds(qoff, GRID_W), :] = _bf(out)
        return carry

    lax.fori_loop(0, NA_ROWS_PER_STEP, row_body, 0, unroll=True)


def _na_call(na_qkv, bias_tab):
    B, L, _ = na_qkv.shape
    rows = L // GRID_W
    seq = lambda col: pl.BlockSpec((1, L, BRANCH_W), lambda b, i: (b, 0, col))
    return pl.pallas_call(
        functools.partial(_na_kernel, rows=rows),
        out_shape=jax.ShapeDtypeStruct((B, L, BRANCH_W), BF16),
        grid=(B, rows // NA_ROWS_PER_STEP),
        in_specs=[pl.BlockSpec((1, NA_TOK, BRANCH_W), lambda b, i: (b, i, 0)), seq(1), seq(2),
                  _const_spec((NA_KH, N_HEADS * GRID_W, NA_WIN))],
        out_specs=pl.BlockSpec((1, NA_TOK, BRANCH_W), lambda b, i: (b, i, 0)),
        compiler_params=_params("parallel", "arbitrary"),
        name="na",
    )(na_qkv, na_qkv, na_qkv, bias_tab)


DL_TQ = 128
DL_TQB = 2048


def _alibi_slopes(n):
    start = 2.0 ** (-8.0 / n)
    return [float(np.float32(start ** (h + 1))) for h in range(n)]


def _dl_kernel(*refs, n, kw_len, tqb, cg, first, last, split):
    if first:
        q_ref, k_ref, v_ref, bias_ref = refs[:4]
        rest = refs[4:]
    else:
        q_ref, k_ref, v_ref, bias_ref, op_ref, lsep_ref = refs[:6]
        rest = refs[6:]
    t = pl.program_id(2)
    qmasks = [m * ATTN_SCALE for m in _head_masks(BF16)]
    nlane = BRANCH_W // LANES

    for c in range(cg):
        def tile_body(u, carry, c=c):
            base = t * tqb + u * DL_TQ
            ws = pl.multiple_of(jnp.clip(base - DIL_BLK, 0, n - kw_len), DIL_BLK)
            rows = pl.ds(pl.multiple_of(u * DL_TQ, DL_TQ), DL_TQ)
            q = q_ref[0, c, rows, :]
            kw = k_ref[0, c, pl.ds(ws, kw_len), :]
            vw = v_ref[0, c, pl.ds(ws, kw_len), :]
            q_heads = jnp.concatenate([q * qmasks[h] for h in range(N_HEADS)], axis=0)
            s = _dot_nt(q_heads, kw) + bias_ref[(base - ws) // DIL_BLK]
            m = jnp.max(s, axis=-1, keepdims=True)
            e = jnp.exp(s - m)
            den = jnp.sum(e, axis=-1, keepdims=True)
            r_all = _dot(_bf(e), vw) * (1.0 / den)
            lse_col = m + jnp.log(den)
            o = _pick_heads([r_all[h * DL_TQ:(h + 1) * DL_TQ, :] for h in range(N_HEADS)])
            lse = _pick_heads([lse_col[h * DL_TQ:(h + 1) * DL_TQ, :] for h in range(N_HEADS)])
            if not first:
                lse_p = lsep_ref[0, c, rows, :]
                lse_m = jnp.maximum(lse_p, lse)
                w_p = jnp.exp(lse_p - lse_m)
                w_c = jnp.exp(lse - lse_m)
                tot = w_p + w_c
                o = (w_p * op_ref[0, c, rows, :] + w_c * o) / tot
                lse = lse_m + jnp.log(tot)
            if last:
                rest[0][0, c, rows, :] = _bf(o)
            else:
                srows = pl.ds(pl.multiple_of(c * tqb + u * DL_TQ, DL_TQ), DL_TQ)
                for val, scr in ((o, rest[2]), (lse, rest[3])):
                    for lc in range(nlane):
                        scr[lc, srows, :] = val[:, lc * LANES:(lc + 1) * LANES]
            return carry

        lax.fori_loop(0, tqb // DL_TQ, tile_body, 0, unroll=True)

    if not last:
        for out_ref, scr in ((rest[0], rest[2]), (rest[1], rest[3])):
            for c in range(cg):
                for a in range(split):
                    out_ref[0, a, c] = jnp.concatenate(
                        [scr[lc, pl.ds(c * tqb + a, tqb // split, stride=split), :] for lc in range(nlane)], axis=1)


def _dl_bias_table(dil, kw_len):
    slopes = np.asarray(_alibi_slopes(N_HEADS), np.float32)
    qi = np.arange(DL_TQ)[None, :, None]
    ki = np.arange(kw_len)[None, None, :] - DIL_BLK * np.arange(3)[:, None, None]
    step = np.abs(ki - qi)
    dist = (step * dil).astype(np.float32)
    bias = np.where((step <= DIL_BLK)[:, None], -slopes[None, :, None, None] * dist[:, None], np.float32(NEG))
    return jnp.asarray(bias.reshape(3, N_HEADS * DL_TQ, kw_len), F32)


def _dl_call(qkv, state, dil, first, last, split):
    B, _, n, _ = qkv.shape
    kw_len = min(4 * DIL_BLK, n)
    tqb = min(DL_TQB, n)
    cg = min(dil, max(1, DL_TQB // tqb))
    tok = lambda c: pl.BlockSpec((1, cg, tqb, BRANCH_W), lambda b, r, t: (b, r, t, c))
    full = lambda c: pl.BlockSpec((1, cg, n, BRANCH_W), lambda b, r, t: (b, r, 0, c))
    in_specs = [tok(0), full(1), full(2), _const_spec((3, N_HEADS * DL_TQ, kw_len))]
    args = [qkv, qkv, qkv, _dl_bias_table(dil, kw_len)]
    if not first:
        in_specs += [tok(0), tok(0)]
        args += list(state)
    scratch = []
    if last:
        out_shape = jax.ShapeDtypeStruct((B, dil, n, BRANCH_W), BF16)
        out_specs = tok(0)
    else:
        shape = (B, split, dil, n // split, BRANCH_W)
        spec = lambda: pl.BlockSpec((1, split, cg, tqb // split, BRANCH_W), lambda b, r, t: (b, 0, r, t, 0))
        out_shape = (jax.ShapeDtypeStruct(shape, F32), jax.ShapeDtypeStruct(shape, F32))
        out_specs = (spec(), spec())
        scratch = [pltpu.VMEM((BRANCH_W // LANES, cg * tqb, LANES), F32) for _ in range(2)]
    out = pl.pallas_call(
        functools.partial(_dl_kernel, n=n, kw_len=kw_len, tqb=tqb, cg=cg, first=first, last=last, split=split),
        out_shape=out_shape,
        grid=(B, dil // cg, n // tqb),
        in_specs=in_specs,
        out_specs=out_specs,
        scratch_shapes=scratch,
        compiler_params=_params("parallel", "arbitrary", "arbitrary"),
        name=f"dl{dil}",
    )(*args)
    if last:
        return out
    return [o.reshape(B, split * dil, n // split, BRANCH_W) for o in out]


def _dilated_attention(dl1, dl4, dl16):
    (_, d1), (_, d4), (_, d16) = DIL_CONFIGS
    state = _dl_call(dl1[:, None], None, d1, True, False, d4 // d1)
    state = _dl_call(dl4, state, d4, False, False, d16 // d4)
    return _dl_call(dl16, state, d16, False, True, 1)


HG_TT = 1024
LOG2E = float(np.log2(np.e))
N_SUB = HG_CHUNK // HG_SUB


def _split3(x):
    hi = _bf(x)
    r1 = x - hi.astype(F32)
    mid = _bf(r1)
    lo = _bf(r1 - mid.astype(F32))
    return hi, mid, lo


def _hg_chunk(q, k, cum, v, st_ref, bd_ref, ones_ref, fwd):
    C, S = HG_CHUNK, HG_SUB
    edge = cum[C - 1:C, :] if fwd else cum[0:1, :]
    vb = _bf(v)
    masks = _head_masks(F32)

    st = st_ref[...]
    o = _dot_nt(_bf(q * jnp.exp(cum)), _bf(st))

    att_blocks = []
    for i in range(N_SUB):
        lo, hi = i * S, (i + 1) * S
        if fwd:
            if i == 0:
                att_blocks.append(None)
                continue
            ref = cum[lo - 1:lo, :]
            k_t = jnp.concatenate([k[0:lo, :] * jnp.exp(ref - cum[0:lo, :]), jnp.zeros((C - lo, BRANCH_W), F32)], axis=0)
        else:
            if i == N_SUB - 1:
                att_blocks.append(None)
                continue
            ref = cum[hi:hi + 1, :]
            k_t = jnp.concatenate([jnp.zeros((hi, BRANCH_W), F32), k[hi:C, :] * jnp.exp(ref - cum[hi:C, :])], axis=0)
        k_t = _bf(k_t)
        q_t = q[lo:hi, :] * jnp.exp(cum[lo:hi, :] - ref)
        q_bd = _bf(jnp.concatenate([q_t * masks[h] for h in range(N_HEADS)], axis=0))
        att = _dot_nt(q_bd, k_t)
        r = _dot(_bf(att), vb)
        att_blocks.append(_pick_heads([r[h * S:(h + 1) * S, :] for h in range(N_HEADS)]))

    ones_bd = ones_ref[...]
    half = S // 2
    half_rows = lax.broadcasted_iota(jnp.int32, (half, 1), 0)
    cum2 = cum * LOG2E
    o_blocks = []
    for i in range(N_SUB):
        lo, hi = i * S, (i + 1) * S
        cq, qq, kk_, vv = cum2[lo:hi, :], q[lo:hi, :], k[lo:hi, :], v[lo:hi, :]
        xs = []
        for s in range(S):
            parts = []
            for r0 in (0, half):
                if fwd:
                    everything, nothing = s <= r0, s > r0 + half - 1
                else:
                    everything, nothing = s >= r0 + half - 1, s < r0
                if nothing:
                    parts.append(jnp.zeros((half, BRANCH_W), F32))
                    continue
                d = cq[r0:r0 + half, :] - cq[s:s + 1, :]
                if not everything:
                    ok = (half_rows >= s - r0) if fwd else (half_rows <= s - r0)
                    d = jnp.where(ok, d, NEG)
                parts.append(qq[r0:r0 + half, :] * kk_[s:s + 1, :] * jnp.exp2(d))
            xs.append(_bf(jnp.concatenate(parts, axis=0)))
        z = _dot(jnp.concatenate(xs, axis=0), ones_bd)
        od_halves = [None, None]
        for s in range(S):
            for hidx, r0 in enumerate((0, half)):
                if (s > r0 + half - 1) if fwd else (s < r0):
                    continue
                term = z[s * S + r0:s * S + r0 + half, :] * vv[s:s + 1, :]
                od_halves[hidx] = term if od_halves[hidx] is None else od_halves[hidx] + term
        od = jnp.concatenate(od_halves, axis=0)
        if att_blocks[i] is not None:
            od = od + att_blocks[i]
        o_blocks.append(od)
    o = o + jnp.concatenate(o_blocks, axis=0)

    k_dec = _bf(k * jnp.exp(edge - cum))
    st_ref[...] = st * jnp.exp(edge) + _dot_tn(vb, k_dec) * bd_ref[...]
    return o


def _hg_kernel(*refs, fwd):
    if fwd:
        f_ref, q_ref, i_ref, lb_ref, tri_ref, bd_ref, ones_ref, o_ref, st_ref, q_scr, k_scr, cum_scr = refs
    else:
        (f_ref, q_ref, i_ref, g_ref, of_ref, lb_ref, gn_ref, tri_ref, bd_ref, ones_ref, o_ref,
         st_ref, q_scr, k_scr, cum_scr, o_scr) = refs

    @pl.when(pl.program_id(1) == 0)
    def _():
        st_ref[...] = jnp.zeros_like(st_ref)

    nchunk = HG_TT // HG_CHUNK

    x = f_ref[0]
    lb = lb_ref[...]
    e = jnp.exp(-jnp.abs(x))
    inv = 1.0 / (1.0 + e)
    pos = x >= 0
    sig_p = jnp.where(pos, inv, e * inv)
    sig_n = jnp.where(pos, e * inv, inv)
    k_scr[...] = (1.0 - lb) * sig_n
    q_pre = q_ref[0]
    q_scr[...] = q_pre * _sigmoid(q_pre)
    parts = _split3(jnp.log(lb + (1.0 - lb) * sig_p))
    tri = tri_ref[...]
    for c in range(nchunk):
        rows = slice(c * HG_CHUNK, (c + 1) * HG_CHUNK)
        cum_scr[rows, :] = sum(_dot(tri, part[rows, :]) for part in parts)

    def chunk_body(ci, carry):
        c = ci if fwd else nchunk - 1 - ci
        rows = pl.ds(pl.multiple_of(c * HG_CHUNK, HG_CHUNK), HG_CHUNK)
        o = _hg_chunk(q_scr[rows, :], k_scr[rows, :], cum_scr[rows, :], i_ref[0, rows, :],
                      st_ref, bd_ref, ones_ref, fwd)
        if fwd:
            o_ref[0, rows, :] = o
        else:
            o_scr[rows, :] = o
        return carry

    lax.fori_loop(0, nchunk, chunk_body, 0, unroll=True)

    if not fwd:
        o = o_scr[...] + of_ref[0]
        sq = o * o
        hi = _bf(sq)
        lo = _bf(sq - hi.astype(F32))
        ms = (_dot(hi, ones_ref[...]) + _dot(lo, ones_ref[...])) * (1.0 / HEAD_DIM)
        g = g_ref[0]
        o_ref[0] = _bf(o * lax.rsqrt(ms + EPS) * gn_ref[...] * (g * _sigmoid(g)))


def _hg_call(hg, o_fwd, lb, gnorm, bd, ones_bd, fwd):
    B, L, _ = hg.shape
    nt = L // HG_TT
    tmap = (lambda t: t) if fwd else (lambda t: nt - 1 - t)
    col = lambda c: pl.BlockSpec((1, HG_TT, BRANCH_W), lambda b, t: (b, tmap(t), c))
    idx = np.arange(HG_CHUNK)
    tri = jnp.asarray((idx[None, :] <= idx[:, None]) if fwd else (idx[None, :] >= idx[:, None]), BF16)
    consts = [_const_spec((HG_CHUNK, HG_CHUNK)), _const_spec((BRANCH_W, BRANCH_W)), _const_spec((BRANCH_W, BRANCH_W))]
    vec = _const_spec((1, BRANCH_W))
    if fwd:
        in_specs = [col(0), col(2), col(3), vec] + consts
        args = [hg, hg, hg, lb, tri, bd, ones_bd]
        out_dtype = F32
    else:
        in_specs = [col(1), col(2), col(3), col(4), col(0), vec, vec] + consts
        args = [hg, hg, hg, hg, o_fwd, lb, gnorm, tri, bd, ones_bd]
        out_dtype = BF16
    return pl.pallas_call(
        functools.partial(_hg_kernel, fwd=fwd),
        out_shape=jax.ShapeDtypeStruct((B, L, BRANCH_W), out_dtype),
        grid=(B, nt),
        in_specs=in_specs,
        out_specs=col(0),
        scratch_shapes=[pltpu.VMEM((BRANCH_W, BRANCH_W), F32)]
        + [pltpu.VMEM((HG_TT, BRANCH_W), F32) for _ in range(3 if fwd else 4)],
        compiler_params=_params("parallel", "arbitrary"),
        name="hg_fwd" if fwd else "hg_bwd",
    )(*args)


MERGE_TM = 512


def _merge_kernel(x_ref, mod_ref, g_ref, na_ref, hg_ref, sg_ref, dl_ref, perm_ref, wg_ref, bg_ref, wb_ref, wo_ref, o_ref):
    x = x_ref[0]
    h = _bf(_mod_norm(x, g_ref[...], mod_ref[0, 1:2, :], mod_ref[0, 0:1, :]))
    dl_classes = jnp.concatenate([dl_ref[0, r] for r in range(dl_ref.shape[1])], axis=0)
    dl = _bf(_dot(perm_ref[...], dl_classes))
    merged = jnp.zeros((MERGE_TM, D_MODEL), F32)
    for bi, br in enumerate((na_ref[0], hg_ref[0], sg_ref[0], dl)):
        gate = _sigmoid(_dot(h, wg_ref[bi]) + bg_ref[bi:bi + 1, :])
        merged = merged + gate * _dot(br, wb_ref[bi])
    y = _dot(_bf(merged), wo_ref[...])
    o_ref[0] = x + mod_ref[0, 2:3, :] * y


def _merge_call(x, mod, g, branches, w_gate, b_gate, w_branch, w_o, layer):
    B, L, _ = x.shape
    tm = MERGE_TM
    d16 = DIL_CONFIGS[2][1]
    tok_idx = np.arange(tm)
    perm = np.zeros((tm, tm), np.float32)
    perm[tok_idx, (tok_idx % d16) * (tm // d16) + tok_idx // d16] = 1.0
    tok = lambda w: pl.BlockSpec((1, tm, w), lambda b, t: (b, t, 0))
    return pl.pallas_call(
        _merge_kernel,
        out_shape=jax.ShapeDtypeStruct((B, L, D_MODEL), F32),
        grid=(B, L // tm),
        in_specs=[tok(D_MODEL),
                  pl.BlockSpec((1, 6, D_MODEL), lambda b, t: (b, 0, 0)),
                  _const_spec((1, D_MODEL)),
                  tok(BRANCH_W), tok(BRANCH_W), tok(BRANCH_W),
                  pl.BlockSpec((1, d16, tm // d16, BRANCH_W), lambda b, t: (b, 0, t, 0)),
                  _const_spec((tm, tm)),
                  _layer_spec((N_BRANCH, D_MODEL, D_MODEL), layer),
                  _layer_spec((N_BRANCH, D_MODEL), layer),
                  _layer_spec((N_BRANCH, BRANCH_W, D_MODEL), layer),
                  _layer_spec((D_MODEL, D_MODEL), layer)],
        out_specs=tok(D_MODEL),
        compiler_params=_params("parallel", "arbitrary"),
        name="merge",
    )(x, mod, g, *branches, jnp.asarray(perm, BF16), w_gate, b_gate, w_branch, w_o)


FFN_TM = 1024
MXU_TILE = 256
FFN_CHUNKS = ((0, 3 * MXU_TILE), (3 * MXU_TILE, 6 * MXU_TILE), (6 * MXU_TILE, 9 * MXU_TILE), (9 * MXU_TILE, D_FF))


def _ffn_kernel(x_ref, mod_ref, g_ref, wg_ref, wu_ref, wd_ref, gf_ref, o_ref, *, final):
    x = x_ref[0]
    h = _bf(_mod_norm(x, g_ref[...], mod_ref[0, 4:5, :], mod_ref[0, 3:4, :]))
    y = jnp.zeros((FFN_TM, D_MODEL), F32)
    for lo, hi in FFN_CHUNKS:
        a = _dot(h, wg_ref[:, lo:hi])
        b = _dot(h, wu_ref[:, lo:hi])
        y = y + _dot(_bf(a * _sigmoid(a) * b), wd_ref[lo:hi, :])
    x = x + mod_ref[0, 5:6, :] * y
    if final:
        x = x * lax.rsqrt(jnp.mean(x * x, axis=-1, keepdims=True) + EPS) * gf_ref[...]
    o_ref[0] = x


def _ffn_call(x, mod, g, wg, wu, wd, layer, g_final, final):
    B, L, _ = x.shape
    tm = FFN_TM
    tok = pl.BlockSpec((1, tm, D_MODEL), lambda b, t: (b, t, 0))
    return pl.pallas_call(
        functools.partial(_ffn_kernel, final=final),
        out_shape=jax.ShapeDtypeStruct((B, L, D_MODEL), F32),
        grid=(B, L // tm),
        in_specs=[tok,
                  pl.BlockSpec((1, 6, D_MODEL), lambda b, t: (b, 0, 0)),
                  _const_spec((1, D_MODEL)),
                  _layer_spec((D_MODEL, D_FF), layer), _layer_spec((D_MODEL, D_FF), layer),
                  _layer_spec((D_FF, D_MODEL), layer),
                  _const_spec((1, D_MODEL))],
        out_specs=tok,
        compiler_params=_params("parallel", "arbitrary"),
        name="ffn",
    )(x, mod, g, wg, wu, wd, g_final)


def _trunk(x, mods, p):
    B, L, D = x.shape
    assert D == D_MODEL and x.dtype == F32 and mods[0].shape == (B, 6, D_MODEL)
    assert L % max(PROJ_TM, MERGE_TM, FFN_TM, HG_TT, NA_TOK) == 0
    assert (L // DIL_CONFIGS[-1][1]) % DL_TQ == 0
    for l in range(DEPTH):
        mod = mods[l]
        na_qkv, hg, y_sg, dl1, dl4, dl16 = _proj_call(x, mod, p["g_mix"][l], p["w_in"], l, p["sg_ln_g"][l],
                                                      p["sg_ln_b"][l], p["sg_w"][l], p["sg_b"][l])
        y_na = _na_call(na_qkv, p["na_bias"][l])
        o_f = _hg_call(hg, None, p["lb"][0][l], None, p["bd"], p["ones_bd"], True)
        y_hg = _hg_call(hg, o_f, p["lb"][1][l], p["hg_gnorm"][l], p["bd"], p["ones_bd"], False)
        y_dl = _dilated_attention(dl1, dl4, dl16)
        x = _merge_call(x, mod, p["g_mix"][l], (y_na, y_hg, y_sg, y_dl),
                        p["w_gate"], p["b_gate"], p["w_branch"], p["w_o"], l)
        x = _ffn_call(x, mod, p["g_ffn"][l], p["w_ffn_gate"], p["w_ffn_up"], p["w_ffn_down"], l,
                      p["g_final"], l == DEPTH - 1)
    return x


def kernel(x_prompt, x_sample, c_prompt, c_sample, w_ada, b_ada, g_norm_mix, g_norm_ffn, w_in, w_gate, b_gate,
           w_branch, w_o, na_rpb, hg_lb, hg_gnorm, sg_ln_g, sg_ln_b, sg_w, sg_b, w_ffn_gate, w_ffn_up,
           w_ffn_down, g_final):
    nbp, nbs = c_prompt.shape[0], c_sample.shape[0]
    nb_pad = -(-(nbp + nbs) // 16) * 16
    c_all = jnp.concatenate([c_prompt, c_sample, jnp.zeros((nb_pad - nbp - nbs, D_MODEL), F32)], axis=0)
    mod_all = _ada_call(c_all, w_ada, b_ada).reshape(DEPTH, nb_pad, 6, D_MODEL)
    mods_p = [mod_all[l, :nbp] for l in range(DEPTH)]
    mods_s = [mod_all[l, nbp:nbp + nbs] for l in range(DEPTH)]

    lb_soft = jax.nn.softmax(hg_lb.astype(F32), axis=1)
    lower = jnp.cumsum(lb_soft, axis=1) - lb_soft[:, :1]
    head = np.arange(BRANCH_W) // HEAD_DIM
    bd = jnp.asarray((head[:, None] == head[None, :]).astype(np.float32))
    vec = lambda a: [a[l].reshape(1, -1).astype(F32) for l in range(DEPTH)]
    p = {
        "g_mix": vec(g_norm_mix), "g_ffn": vec(g_norm_ffn),
        "w_in": _bf(w_in), "w_gate": _bf(w_gate), "b_gate": b_gate.astype(F32), "w_branch": _bf(w_branch),
        "w_o": _bf(w_o),
        "na_bias": [_na_bias_table(na_rpb[l]) for l in range(DEPTH)],
        "lb": [[lower[d, l].reshape(1, BRANCH_W) for l in range(DEPTH)] for d in range(2)],
        "hg_gnorm": vec(hg_gnorm), "sg_ln_g": vec(sg_ln_g), "sg_ln_b": vec(sg_ln_b),
        "sg_w": [_bf(sg_w[l].transpose(1, 0, 2).reshape(SG_CHUNK, SG_GROUPS * SG_CHUNK)) for l in range(DEPTH)],
        "sg_b": [jnp.repeat(sg_b[l].T.astype(F32), BRANCH_W // SG_GROUPS, axis=1) for l in range(DEPTH)],
        "w_ffn_gate": _bf(w_ffn_gate), "w_ffn_up": _bf(w_ffn_up), "w_ffn_down": _bf(w_ffn_down),
        "g_final": g_final.reshape(1, D_MODEL).astype(F32),
        "bd": bd, "ones_bd": _bf(bd),
    }
    return (_trunk(x_prompt, mods_p, p), _trunk(x_sample, mods_s, p))
```

```python
import functools

import jax
import jax.numpy as jnp
import numpy as np
from jax import lax
from jax.experimental import pallas as pl
from jax.experimental.pallas import tpu as pltpu

D_MODEL = 1024
DEPTH = 4
GRID_W = 64
BRANCH_W = 256
N_BRANCH = 4
HEAD_DIM = 64
N_HEADS = BRANCH_W // HEAD_DIM
ATTN_SCALE = HEAD_DIM ** -0.5
NA_KH = 8
NA_KW = 16
HG_CHUNK = 64
HG_SUB = 16
SG_GROUPS = 4
SG_CHUNK = 128
DIL_CONFIGS = ((128, 1), (512, 4), (2048, 16))
DIL_BLK = 64
D_FF = 2816
IN_WIDTH = 3328
NEG = -1e30
EPS = 1e-6

NA_ROWS_PER_STEP = 32
NA_TOK = NA_ROWS_PER_STEP * GRID_W
NA_WIN = NA_KH * GRID_W

VMEM_LIMIT = 56 * 1024 * 1024

BF16 = jnp.bfloat16
F32 = jnp.float32


def _bf(x):
    return x.astype(BF16)


def _dot(a, b):
    return jnp.dot(a, b, preferred_element_type=F32)


def _dot_nt(a, b):
    return lax.dot_general(a, b, (((1,), (1,)), ((), ())), preferred_element_type=F32)


def _dot_tn(a, b):
    return lax.dot_general(a, b, (((0,), (0,)), ((), ())), preferred_element_type=F32)


def _sigmoid(x):
    return 1.0 / (1.0 + jnp.exp(-x))


def _head_masks(dtype):
    lane_head = lax.broadcasted_iota(jnp.int32, (1, BRANCH_W), 1) // HEAD_DIM
    return [(lane_head == h).astype(dtype) for h in range(N_HEADS)]


def _pick_heads(blocks):
    lane_head = lax.broadcasted_iota(jnp.int32, (1, BRANCH_W), 1) // HEAD_DIM
    out = blocks[N_HEADS - 1]
    for h in range(N_HEADS - 2, -1, -1):
        out = jnp.where(lane_head <= h, blocks[h], out)
    return out


def _mod_norm(x, g, scale, shift):
    y = x * lax.rsqrt(jnp.mean(x * x, axis=-1, keepdims=True) + EPS)
    return (y * g) * (1.0 + scale) + shift


def _params(*sem):
    return pltpu.CompilerParams(dimension_semantics=sem, vmem_limit_bytes=VMEM_LIMIT)


def _const_spec(shape):
    nd = len(shape)
    return pl.BlockSpec(shape, lambda *_: (0,) * nd, pipeline_mode=pl.Buffered(1))


def _layer_spec(shape, layer):
    nd = len(shape)
    return pl.BlockSpec((None,) + tuple(shape), lambda *_: (layer,) + (0,) * nd, pipeline_mode=pl.Buffered(1))


ADA_TN = 1536

def _ada_kernel(c_ref, w_ref, b_ref, o_ref):
    c = c_ref[...]
    cs = c * _sigmoid(c)
    o_ref[0] = _dot(_bf(cs), _bf(w_ref[0])) + b_ref[0]


def _ada_call(c_all, w_ada, b_ada):
    nb = c_all.shape[0]
    tn = ADA_TN
    return pl.pallas_call(
        _ada_kernel,
        out_shape=jax.ShapeDtypeStruct((DEPTH, nb, 6 * D_MODEL), F32),
        grid=(DEPTH, 6 * D_MODEL // tn),
        in_specs=[pl.BlockSpec((nb, D_MODEL), lambda l, n: (0, 0)),
                  pl.BlockSpec((1, D_MODEL, tn), lambda l, n: (l, 0, n)),
                  pl.BlockSpec((1, 1, tn), lambda l, n: (l, 0, n))],
        out_specs=pl.BlockSpec((1, nb, tn), lambda l, n: (l, 0, n)),
        compiler_params=_params("arbitrary", "arbitrary"),
        name="ada",
    )(c_all, w_ada, b_ada.reshape(DEPTH, 1, 6 * D_MODEL))


PROJ_TM = 1024


LANES = 128


def _store_lane_chunks(scr_ref, val):
    for c in range(val.shape[1] // LANES):
        scr_ref[c] = val[:, c * LANES:(c + 1) * LANES]


def _load_class(scr_ref, r, n, dil):
    return jnp.concatenate([scr_ref[c, pl.ds(r, n, stride=dil), :] for c in range(scr_ref.shape[0])], axis=1)


def _gelu_tanh(x):
    return 0.5 * x * (1.0 + jnp.tanh(np.sqrt(2.0 / np.pi).astype(np.float32) * (x + 0.044715 * (x * x * x))))


def _spatial_gating(uv, ln_g, ln_b, w, b):
    lane_group = lax.broadcasted_iota(jnp.int32, (1, BRANCH_W), 1) // (BRANCH_W // SG_GROUPS)
    u = _gelu_tanh(uv[:, 0:BRANCH_W])
    v = _gelu_tanh(uv[:, BRANCH_W:2 * BRANCH_W])
    mu = jnp.mean(v, axis=-1, keepdims=True)
    var = jnp.mean(jnp.square(v - mu), axis=-1, keepdims=True)
    v = (v - mu) * lax.rsqrt(var + EPS) * ln_g + ln_b
    vstack = jnp.concatenate([_bf(jnp.where(lane_group == g, v, 0.0)) for g in range(SG_GROUPS)], axis=0)
    return u * (_dot(w, vstack) + b)


def _proj_kernel(x_ref, mod_ref, g_ref, w_ref, lng_ref, lnb_ref, sgw_ref, sgb_ref,
                 na_ref, hg_ref, sg_ref, dl1_ref, dl4_ref, dl16_ref, dl_scr):
    h = _bf(_mod_norm(x_ref[0], g_ref[...], mod_ref[0, 1:2, :], mod_ref[0, 0:1, :]))
    uv = _dot(h, w_ref[:, 2048:2560])
    dl = _dot(h, w_ref[:, 2560:3328])
    for n in range(PROJ_TM // SG_CHUNK):
        rows = slice(n * SG_CHUNK, (n + 1) * SG_CHUNK)
        sg_ref[0, rows, :] = _bf(_spatial_gating(uv[rows, :], lng_ref[...], lnb_ref[...], sgw_ref[...], sgb_ref[...]))
    dl1_ref[0] = _bf(dl)
    _store_lane_chunks(dl_scr, dl)
    for dil, ref in ((DIL_CONFIGS[1][1], dl4_ref), (DIL_CONFIGS[2][1], dl16_ref)):
        for r in range(dil):
            ref[0, r] = _bf(_load_class(dl_scr, r, PROJ_TM // dil, dil))
    na_ref[0] = _bf(_dot(h, w_ref[:, 0:768]))
    hg_ref[0] = _dot(h, w_ref[:, 768:2048])


def _proj_call(x, mod, g, w_in, layer, sg_ln_g, sg_ln_b, sg_w, sg_b):
    B, L, _ = x.shape
    tm = PROJ_TM
    d4, d16 = DIL_CONFIGS[1][1], DIL_CONFIGS[2][1]
    tok = lambda w: pl.BlockSpec((1, tm, w), lambda b, t: (b, t, 0))
    cls = lambda d: pl.BlockSpec((1, d, tm // d, 768), lambda b, t: (b, 0, t, 0))
    return pl.pallas_call(
        _proj_kernel,
        out_shape=(jax.ShapeDtypeStruct((B, L, 768), BF16),
                   jax.ShapeDtypeStruct((B, L, 1280), F32),
                   jax.ShapeDtypeStruct((B, L, BRANCH_W), BF16),
                   jax.ShapeDtypeStruct((B, L, 768), BF16),
                   jax.ShapeDtypeStruct((B, d4, L // d4, 768), BF16),
                   jax.ShapeDtypeStruct((B, d16, L // d16, 768), BF16)),
        grid=(B, L // tm),
        in_specs=[tok(D_MODEL),
                  pl.BlockSpec((1, 6, D_MODEL), lambda b, t: (b, 0, 0)),
                  _const_spec((1, D_MODEL)),
                  _layer_spec((D_MODEL, IN_WIDTH), layer),
                  _const_spec((1, BRANCH_W)), _const_spec((1, BRANCH_W)),
                  _const_spec((SG_CHUNK, SG_GROUPS * SG_CHUNK)),
                  _const_spec((SG_CHUNK, BRANCH_W))],
        out_specs=(tok(768), tok(1280), tok(BRANCH_W), tok(768), cls(d4), cls(d16)),
        scratch_shapes=[pltpu.VMEM((768 // LANES, tm, LANES), F32)],
        compiler_params=_params("parallel", "arbitrary"),
        name="proj",
    )(x, mod, g, w_in, sg_ln_g, sg_ln_b, sg_w, sg_b)


def _na_bias_table(rpb):
    c = np.arange(GRID_W)
    kc = np.arange(GRID_W)
    col_off = np.clip(kc[None, :] - c[:, None], -(NA_KW - 1), NA_KW - 1) + NA_KW - 1
    win0 = np.clip(c - NA_KW // 2, 0, GRID_W - NA_KW)
    valid = (kc[None, :] >= win0[:, None]) & (kc[None, :] < win0[:, None] + NA_KW)
    onehot = (col_off[None] == np.arange(2 * NA_KW - 1)[:, None, None]).astype(np.float32)
    cols = jnp.einsum("hrj,jck->hrck", rpb.astype(F32), jnp.asarray(onehot), precision=lax.Precision.HIGHEST)
    cols = jnp.where(valid[None, None], cols, NEG)
    pats = jnp.stack([cols[:, NA_KH - 1 - d:2 * NA_KH - 1 - d] for d in range(NA_KH)])
    return pats.transpose(0, 1, 3, 2, 4).reshape(NA_KH, N_HEADS * GRID_W, NA_WIN)


def _na_kernel(q_ref, k_ref, v_ref, bias_ref, o_ref, *, rows):
    i = pl.program_id(1)
    qmasks = [m * ATTN_SCALE for m in _head_masks(BF16)]

    def row_body(j, carry):
        r = i * NA_ROWS_PER_STEP + j
        start = jnp.clip(r - NA_KH // 2, 0, rows - NA_KH)
        off = pl.multiple_of(start * GRID_W, GRID_W)
        delta = r - start
        qoff = pl.multiple_of(j * GRID_W, GRID_W)
        q = q_ref[0, pl.ds(qoff, GRID_W), :]
        kw = k_ref[0, pl.ds(off, NA_WIN), :]
        vw = v_ref[0, pl.ds(off, NA_WIN), :]
        q_heads = jnp.concatenate([q * qmasks[h] for h in range(N_HEADS)], axis=0)
        s = _dot_nt(q_heads, kw) + bias_ref[delta]
        m = jnp.max(s, axis=-1, keepdims=True)
        e = jnp.exp(s - m)
        den = jnp.sum(e, axis=-1, keepdims=True)
        r_all = _dot(_bf(e), vw) * (1.0 / den)
        out = _pick_heads([r_all[h * GRID_W:(h + 1) * GRID_W, :] for h in range(N_HEADS)])
        o_ref[0, pl.ds(qoff, GRID_W), :] = _bf(out)
        return carry

    lax.fori_loop(0, NA_ROWS_PER_STEP, row_body, 0, unroll=True)


def _na_call(na_qkv, bias_tab):
    B, L, _ = na_qkv.shape
    rows = L // GRID_W
    seq = lambda col: pl.BlockSpec((1, L, BRANCH_W), lambda b, i: (b, 0, col))
    return pl.pallas_call(
        functools.partial(_na_kernel, rows=rows),
        out_shape=jax.ShapeDtypeStruct((B, L, BRANCH_W), BF16),
        grid=(B, rows // NA_ROWS_PER_STEP),
        in_specs=[pl.BlockSpec((1, NA_TOK, BRANCH_W), lambda b, i: (b, i, 0)), seq(1), seq(2),
                  _const_spec((NA_KH, N_HEADS * GRID_W, NA_WIN))],
        out_specs=pl.BlockSpec((1, NA_TOK, BRANCH_W), lambda b, i: (b, i, 0)),
        compiler_params=_params("parallel", "arbitrary"),
        name="na",
    )(na_qkv, na_qkv, na_qkv, bias_tab)


DL_TQ = 128
DL_TQB = 2048


def _alibi_slopes(n):
    start = 2.0 ** (-8.0 / n)
    return [float(np.float32(start ** (h + 1))) for h in range(n)]


def _dl_kernel(*refs, n, kw_len, tqb, cg, first, last, split):
    if first:
        q_ref, k_ref, v_ref, bias_ref = refs[:4]
        rest = refs[4:]
    else:
        q_ref, k_ref, v_ref, bias_ref, op_ref, lsep_ref = refs[:6]
        rest = refs[6:]
    t = pl.program_id(2)
    qmasks = [m * ATTN_SCALE for m in _head_masks(BF16)]
    nlane = BRANCH_W // LANES

    for c in range(cg):
        def tile_body(u, carry, c=c):
            base = t * tqb + u * DL_TQ
            ws = pl.multiple_of(jnp.clip(base - DIL_BLK, 0, n - kw_len), DIL_BLK)
            rows = pl.ds(pl.multiple_of(u * DL_TQ, DL_TQ), DL_TQ)
            q = q_ref[0, c, rows, :]
            kw = k_ref[0, c, pl.ds(ws, kw_len), :]
            vw = v_ref[0, c, pl.ds(ws, kw_len), :]
            q_heads = jnp.concatenate([q * qmasks[h] for h in range(N_HEADS)], axis=0)
            s = _dot_nt(q_heads, kw) + bias_ref[(base - ws) // DIL_BLK]
            m = jnp.max(s, axis=-1, keepdims=True)
            e = jnp.exp(s - m)
            den = jnp.sum(e, axis=-1, keepdims=True)
            r_all = _dot(_bf(e), vw) * (1.0 / den)
            lse_col = m + jnp.log(den)
            o = _pick_heads([r_all[h * DL_TQ:(h + 1) * DL_TQ, :] for h in range(N_HEADS)])
            lse = _pick_heads([lse_col[h * DL_TQ:(h + 1) * DL_TQ, :] for h in range(N_HEADS)])
            if not first:
                lse_p = lsep_ref[0, c, rows, :]
                lse_m = jnp.maximum(lse_p, lse)
                w_p = jnp.exp(lse_p - lse_m)
                w_c = jnp.exp(lse - lse_m)
                tot = w_p + w_c
                o = (w_p * op_ref[0, c, rows, :] + w_c * o) / tot
                lse = lse_m + jnp.log(tot)
            if last:
                rest[0][0, c, rows, :] = _bf(o)
            else:
                srows = pl.ds(pl.multiple_of(c * tqb + u * DL_TQ, DL_TQ), DL_TQ)
                for val, scr in ((o, rest[2]), (lse, rest[3])):
                    for lc in range(nlane):
                        scr[lc, srows, :] = val[:, lc * LANES:(lc + 1) * LANES]
            return carry

        lax.fori_loop(0, tqb // DL_TQ, tile_body, 0, unroll=True)

    if not last:
        for out_ref, scr in ((rest[0], rest[2]), (rest[1], rest[3])):
            for c in range(cg):
                for a in range(split):
                    out_ref[0, a, c] = jnp.concatenate(
                        [scr[lc, pl.ds(c * tqb + a, tqb // split, stride=split), :] for lc in range(nlane)], axis=1)


def _dl_bias_table(dil, kw_len):
    slopes = np.asarray(_alibi_slopes(N_HEADS), np.float32)
    qi = np.arange(DL_TQ)[None, :, None]
    ki = np.arange(kw_len)[None, None, :] - DIL_BLK * np.arange(3)[:, None, None]
    step = np.abs(ki - qi)
    dist = (step * dil).astype(np.float32)
    bias = np.where((step <= DIL_BLK)[:, None], -slopes[None, :, None, None] * dist[:, None], np.float32(NEG))
    return jnp.asarray(bias.reshape(3, N_HEADS * DL_TQ, kw_len), F32)


def _dl_call(qkv, state, dil, first, last, split):
    B, _, n, _ = qkv.shape
    kw_len = min(4 * DIL_BLK, n)
    tqb = min(DL_TQB, n)
    cg = min(dil, max(1, DL_TQB // tqb))
    tok = lambda c: pl.BlockSpec((1, cg, tqb, BRANCH_W), lambda b, r, t: (b, r, t, c))
    full = lambda c: pl.BlockSpec((1, cg, n, BRANCH_W), lambda b, r, t: (b, r, 0, c))
    in_specs = [tok(0), full(1), full(2), _const_spec((3, N_HEADS * DL_TQ, kw_len))]
    args = [qkv, qkv, qkv, _dl_bias_table(dil, kw_len)]
    if not first:
        in_specs += [tok(0), tok(0)]
        args += list(state)
    scratch = []
    if last:
        out_shape = jax.ShapeDtypeStruct((B, dil, n, BRANCH_W), BF16)
        out_specs = tok(0)
    else:
        shape = (B, split, dil, n // split, BRANCH_W)
        spec = lambda: pl.BlockSpec((1, split, cg, tqb // split, BRANCH_W), lambda b, r, t: (b, 0, r, t, 0))
        out_shape = (jax.ShapeDtypeStruct(shape, F32), jax.ShapeDtypeStruct(shape, F32))
        out_specs = (spec(), spec())
        scratch = [pltpu.VMEM((BRANCH_W // LANES, cg * tqb, LANES), F32) for _ in range(2)]
    out = pl.pallas_call(
        functools.partial(_dl_kernel, n=n, kw_len=kw_len, tqb=tqb, cg=cg, first=first, last=last, split=split),
        out_shape=out_shape,
        grid=(B, dil // cg, n // tqb),
        in_specs=in_specs,
        out_specs=out_specs,
        scratch_shapes=scratch,
        compiler_params=_params("parallel", "arbitrary", "arbitrary"),
        name=f"dl{dil}",
    )(*args)
    if last:
        return out
    return [o.reshape(B, split * dil, n // split, BRANCH_W) for o in out]


def _dilated_attention(dl1, dl4, dl16):
    (_, d1), (_, d4), (_, d16) = DIL_CONFIGS
    state = _dl_call(dl1[:, None], None, d1, True, False, d4 // d1)
    state = _dl_call(dl4, state, d4, False, False, d16 // d4)
    return _dl_call(dl16, state, d16, False, True, 1)


HG_TT = 1024
LOG2E = float(np.log2(np.e))
N_SUB = HG_CHUNK // HG_SUB


def _split3(x):
    hi = _bf(x)
    r1 = x - hi.astype(F32)
    mid = _bf(r1)
    lo = _bf(r1 - mid.astype(F32))
    return hi, mid, lo


def _hg_chunk(q, k, cum, v, st_ref, bd_ref, ones_ref, fwd):
    C, S = HG_CHUNK, HG_SUB
    edge = cum[C - 1:C, :] if fwd else cum[0:1, :]
    vb = _bf(v)
    masks = _head_masks(F32)

    st = st_ref[...]
    o = _dot_nt(_bf(q * jnp.exp(cum)), _bf(st))

    att_blocks = []
    for i in range(N_SUB):
        lo, hi = i * S, (i + 1) * S
        if fwd:
            if i == 0:
                att_blocks.append(None)
                continue
            ref = cum[lo - 1:lo, :]
            k_t = jnp.concatenate([k[0:lo, :] * jnp.exp(ref - cum[0:lo, :]), jnp.zeros((C - lo, BRANCH_W), F32)], axis=0)
        else:
            if i == N_SUB - 1:
                att_blocks.append(None)
                continue
            ref = cum[hi:hi + 1, :]
            k_t = jnp.concatenate([jnp.zeros((hi, BRANCH_W), F32), k[hi:C, :] * jnp.exp(ref - cum[hi:C, :])], axis=0)
        k_t = _bf(k_t)
        q_t = q[lo:hi, :] * jnp.exp(cum[lo:hi, :] - ref)
        q_bd = _bf(jnp.concatenate([q_t * masks[h] for h in range(N_HEADS)], axis=0))
        att = _dot_nt(q_bd, k_t)
        r = _dot(_bf(att), vb)
        att_blocks.append(_pick_heads([r[h * S:(h + 1) * S, :] for h in range(N_HEADS)]))

    ones_bd = ones_ref[...]
    half = S // 2
    half_rows = lax.broadcasted_iota(jnp.int32, (half, 1), 0)
    cum2 = cum * LOG2E
    o_blocks = []
    for i in range(N_SUB):
        lo, hi = i * S, (i + 1) * S
        cq, qq, kk_, vv = cum2[lo:hi, :], q[lo:hi, :], k[lo:hi, :], v[lo:hi, :]
        xs = []
        for s in range(S):
            parts = []
            for r0 in (0, half):
                if fwd:
                    everything, nothing = s <= r0, s > r0 + half - 1
                else:
                    everything, nothing = s >= r0 + half - 1, s < r0
                if nothing:
                    parts.append(jnp.zeros((half, BRANCH_W), F32))
                    continue
                d = cq[r0:r0 + half, :] - cq[s:s + 1, :]
                if not everything:
                    ok = (half_rows >= s - r0) if fwd else (half_rows <= s - r0)
                    d = jnp.where(ok, d, NEG)
                parts.append(qq[r0:r0 + half, :] * kk_[s:s + 1, :] * jnp.exp2(d))
            xs.append(_bf(jnp.concatenate(parts, axis=0)))
        z = _dot(jnp.concatenate(xs, axis=0), ones_bd)
        od_halves = [None, None]
        for s in range(S):
            for hidx, r0 in enumerate((0, half)):
                if (s > r0 + half - 1) if fwd else (s < r0):
                    continue
                term = z[s * S + r0:s * S + r0 + half, :] * vv[s:s + 1, :]
                od_halves[hidx] = term if od_halves[hidx] is None else od_halves[hidx] + term
        od = jnp.concatenate(od_halves, axis=0)
        if att_blocks[i] is not None:
            od = od + att_blocks[i]
        o_blocks.append(od)
    o = o + jnp.concatenate(o_blocks, axis=0)

    k_dec = _bf(k * jnp.exp(edge - cum))
    st_ref[...] = st * jnp.exp(edge) + _dot_tn(vb, k_dec) * bd_ref[...]
    return o


def _hg_kernel(*refs, fwd):
    if fwd:
        f_ref, q_ref, i_ref, lb_ref, tri_ref, bd_ref, ones_ref, o_ref, st_ref, q_scr, k_scr, cum_scr = refs
    else:
        (f_ref, q_ref, i_ref, of_ref, lb_ref, tri_ref, bd_ref, ones_ref, o_ref,
         st_ref, q_scr, k_scr, cum_scr) = refs

    @pl.when(pl.program_id(1) == 0)
    def _():
        st_ref[...] = jnp.zeros_like(st_ref)

    nchunk = HG_TT // HG_CHUNK

    x = f_ref[0]
    lb = lb_ref[...]
    e = jnp.exp(-jnp.abs(x))
    inv = 1.0 / (1.0 + e)
    pos = x >= 0
    sig_p = jnp.where(pos, inv, e * inv)
    sig_n = jnp.where(pos, e * inv, inv)
    k_scr[...] = (1.0 - lb) * sig_n
    q_pre = q_ref[0]
    q_scr[...] = q_pre * _sigmoid(q_pre)
    parts = _split3(jnp.log(lb + (1.0 - lb) * sig_p))
    tri = tri_ref[...]
    for c in range(nchunk):
        rows = slice(c * HG_CHUNK, (c + 1) * HG_CHUNK)
        cum_scr[rows, :] = sum(_dot(tri, part[rows, :]) for part in parts)

    def chunk_body(ci, carry):
        c = ci if fwd else nchunk - 1 - ci
        rows = pl.ds(pl.multiple_of(c * HG_CHUNK, HG_CHUNK), HG_CHUNK)
        o = _hg_chunk(q_scr[rows, :], k_scr[rows, :], cum_scr[rows, :], i_ref[0, rows, :],
                      st_ref, bd_ref, ones_ref, fwd)
        o_ref[0, rows, :] = o if fwd else o + of_ref[0, rows, :]
        return carry

    lax.fori_loop(0, nchunk, chunk_body, 0, unroll=True)


def _hg_output(o, g, gnorm, ones_bd):
    sq = o * o
    hi = _bf(sq)
    lo = _bf(sq - hi.astype(F32))
    ms = (_dot(hi, ones_bd) + _dot(lo, ones_bd)) * (1.0 / HEAD_DIM)
    return o * lax.rsqrt(ms + EPS) * gnorm * (g * _sigmoid(g))


def _hg_call(hg, o_fwd, lb, bd, ones_bd, fwd):
    B, L, _ = hg.shape
    nt = L // HG_TT
    tmap = (lambda t: t) if fwd else (lambda t: nt - 1 - t)
    col = lambda c: pl.BlockSpec((1, HG_TT, BRANCH_W), lambda b, t: (b, tmap(t), c))
    idx = np.arange(HG_CHUNK)
    tri = jnp.asarray((idx[None, :] <= idx[:, None]) if fwd else (idx[None, :] >= idx[:, None]), BF16)
    consts = [_const_spec((HG_CHUNK, HG_CHUNK)), _const_spec((BRANCH_W, BRANCH_W)), _const_spec((BRANCH_W, BRANCH_W))]
    vec = _const_spec((1, BRANCH_W))
    if fwd:
        in_specs = [col(0), col(2), col(3), vec] + consts
        args = [hg, hg, hg, lb, tri, bd, ones_bd]
    else:
        in_specs = [col(1), col(2), col(3), col(0), vec] + consts
        args = [hg, hg, hg, o_fwd, lb, tri, bd, ones_bd]
    return pl.pallas_call(
        functools.partial(_hg_kernel, fwd=fwd),
        out_shape=jax.ShapeDtypeStruct((B, L, BRANCH_W), F32),
        grid=(B, nt),
        in_specs=in_specs,
        out_specs=col(0),
        scratch_shapes=[pltpu.VMEM((BRANCH_W, BRANCH_W), F32)]
        + [pltpu.VMEM((HG_TT, BRANCH_W), F32) for _ in range(3)],
        compiler_params=_params("parallel", "arbitrary"),
        name="hg_fwd" if fwd else "hg_bwd",
    )(*args)


MERGE_TM = 512


def _merge_kernel(x_ref, mod_ref, g_ref, na_ref, hgo_ref, hgg_ref, gn_ref, ones_ref, sg_ref, dl_ref, perm_ref,
                  wg_ref, bg_ref, wb_ref, wo_ref, o_ref):
    x = x_ref[0]
    h = _bf(_mod_norm(x, g_ref[...], mod_ref[0, 1:2, :], mod_ref[0, 0:1, :]))
    y_hg = _bf(_hg_output(hgo_ref[0], hgg_ref[0], gn_ref[...], ones_ref[...]))
    dl_classes = jnp.concatenate([dl_ref[0, r] for r in range(dl_ref.shape[1])], axis=0)
    dl = _bf(_dot(perm_ref[...], dl_classes))
    merged = jnp.zeros((MERGE_TM, D_MODEL), F32)
    for bi, br in enumerate((na_ref[0], y_hg, sg_ref[0], dl)):
        gate = _sigmoid(_dot(h, wg_ref[bi]) + bg_ref[bi:bi + 1, :])
        merged = merged + gate * _dot(br, wb_ref[bi])
    y = _dot(_bf(merged), wo_ref[...])
    o_ref[0] = x + mod_ref[0, 2:3, :] * y


def _merge_call(x, mod, g, y_na, hg_sum, hg, hg_gnorm, ones_bd, y_sg, y_dl, w_gate, b_gate, w_branch, w_o, layer):
    B, L, _ = x.shape
    tm = MERGE_TM
    d16 = DIL_CONFIGS[2][1]
    tok_idx = np.arange(tm)
    perm = np.zeros((tm, tm), np.float32)
    perm[tok_idx, (tok_idx % d16) * (tm // d16) + tok_idx // d16] = 1.0
    tok = lambda w: pl.BlockSpec((1, tm, w), lambda b, t: (b, t, 0))
    return pl.pallas_call(
        _merge_kernel,
        out_shape=jax.ShapeDtypeStruct((B, L, D_MODEL), F32),
        grid=(B, L // tm),
        in_specs=[tok(D_MODEL),
                  pl.BlockSpec((1, 6, D_MODEL), lambda b, t: (b, 0, 0)),
                  _const_spec((1, D_MODEL)),
                  tok(BRANCH_W), tok(BRANCH_W),
                  pl.BlockSpec((1, tm, BRANCH_W), lambda b, t: (b, t, 4)),
                  _const_spec((1, BRANCH_W)), _const_spec((BRANCH_W, BRANCH_W)),
                  tok(BRANCH_W),
                  pl.BlockSpec((1, d16, tm // d16, BRANCH_W), lambda b, t: (b, 0, t, 0)),
                  _const_spec((tm, tm)),
                  _layer_spec((N_BRANCH, D_MODEL, D_MODEL), layer),
                  _layer_spec((N_BRANCH, D_MODEL), layer),
                  _layer_spec((N_BRANCH, BRANCH_W, D_MODEL), layer),
                  _layer_spec((D_MODEL, D_MODEL), layer)],
        out_specs=tok(D_MODEL),
        compiler_params=_params("parallel", "arbitrary"),
        name="merge",
    )(x, mod, g, y_na, hg_sum, hg, hg_gnorm, ones_bd, y_sg, y_dl, jnp.asarray(perm, BF16),
      w_gate, b_gate, w_branch, w_o)


FFN_TM = 1024
MXU_TILE = 256
FFN_CHUNKS = ((0, 3 * MXU_TILE), (3 * MXU_TILE, 6 * MXU_TILE), (6 * MXU_TILE, 9 * MXU_TILE), (9 * MXU_TILE, D_FF))


def _ffn_kernel(x_ref, mod_ref, g_ref, wg_ref, wu_ref, wd_ref, gf_ref, o_ref, *, final):
    x = x_ref[0]
    h = _bf(_mod_norm(x, g_ref[...], mod_ref[0, 4:5, :], mod_ref[0, 3:4, :]))
    y = jnp.zeros((FFN_TM, D_MODEL), F32)
    for lo, hi in FFN_CHUNKS:
        a = _dot(h, wg_ref[:, lo:hi])
        b = _dot(h, wu_ref[:, lo:hi])
        y = y + _dot(_bf(a * _sigmoid(a) * b), wd_ref[lo:hi, :])
    x = x + mod_ref[0, 5:6, :] * y
    if final:
        x = x * lax.rsqrt(jnp.mean(x * x, axis=-1, keepdims=True) + EPS) * gf_ref[...]
    o_ref[0] = x


def _ffn_call(x, mod, g, wg, wu, wd, layer, g_final, final):
    B, L, _ = x.shape
    tm = FFN_TM
    tok = pl.BlockSpec((1, tm, D_MODEL), lambda b, t: (b, t, 0))
    return pl.pallas_call(
        functools.partial(_ffn_kernel, final=final),
        out_shape=jax.ShapeDtypeStruct((B, L, D_MODEL), F32),
        grid=(B, L // tm),
        in_specs=[tok,
                  pl.BlockSpec((1, 6, D_MODEL), lambda b, t: (b, 0, 0)),
                  _const_spec((1, D_MODEL)),
                  _layer_spec((D_MODEL, D_FF), layer), _layer_spec((D_MODEL, D_FF), layer),
                  _layer_spec((D_FF, D_MODEL), layer),
                  _const_spec((1, D_MODEL))],
        out_specs=tok,
        compiler_params=_params("parallel", "arbitrary"),
        name="ffn",
    )(x, mod, g, wg, wu, wd, g_final)


def _trunk(x, mods, p):
    B, L, D = x.shape
    assert D == D_MODEL and x.dtype == F32 and mods[0].shape == (B, 6, D_MODEL)
    assert L % max(PROJ_TM, MERGE_TM, FFN_TM, HG_TT, NA_TOK) == 0
    assert (L // DIL_CONFIGS[-1][1]) % DL_TQ == 0
    for l in range(DEPTH):
        mod = mods[l]
        na_qkv, hg, y_sg, dl1, dl4, dl16 = _proj_call(x, mod, p["g_mix"][l], p["w_in"], l, p["sg_ln_g"][l],
                                                      p["sg_ln_b"][l], p["sg_w"][l], p["sg_b"][l])
        y_na = _na_call(na_qkv, p["na_bias"][l])
        o_f = _hg_call(hg, None, p["lb"][0][l], p["bd"], p["ones_bd"], True)
        hg_sum = _hg_call(hg, o_f, p["lb"][1][l], p["bd"], p["ones_bd"], False)
        y_dl = _dilated_attention(dl1, dl4, dl16)
        x = _merge_call(x, mod, p["g_mix"][l], y_na, hg_sum, hg, p["hg_gnorm"][l], p["ones_bd"], y_sg, y_dl,
                        p["w_gate"], p["b_gate"], p["w_branch"], p["w_o"], l)
        x = _ffn_call(x, mod, p["g_ffn"][l], p["w_ffn_gate"], p["w_ffn_up"], p["w_ffn_down"], l,
                      p["g_final"], l == DEPTH - 1)
    return x


def kernel(x_prompt, x_sample, c_prompt, c_sample, w_ada, b_ada, g_norm_mix, g_norm_ffn, w_in, w_gate, b_gate,
           w_branch, w_o, na_rpb, hg_lb, hg_gnorm, sg_ln_g, sg_ln_b, sg_w, sg_b, w_ffn_gate, w_ffn_up,
           w_ffn_down, g_final):
    nbp, nbs = c_prompt.shape[0], c_sample.shape[0]
    nb_pad = -(-(nbp + nbs) // 16) * 16
    c_all = jnp.concatenate([c_prompt, c_sample, jnp.zeros((nb_pad - nbp - nbs, D_MODEL), F32)], axis=0)
    mod_all = _ada_call(c_all, w_ada, b_ada).reshape(DEPTH, nb_pad, 6, D_MODEL)
    mods_p = [mod_all[l, :nbp] for l in range(DEPTH)]
    mods_s = [mod_all[l, nbp:nbp + nbs] for l in range(DEPTH)]

    lb_soft = jax.nn.softmax(hg_lb.astype(F32), axis=1)
    lower = jnp.cumsum(lb_soft, axis=1) - lb_soft[:, :1]
    head = np.arange(BRANCH_W) // HEAD_DIM
    bd = jnp.asarray((head[:, None] == head[None, :]).astype(np.float32))
    vec = lambda a: [a[l].reshape(1, -1).astype(F32) for l in range(DEPTH)]
    p = {
        "g_mix": vec(g_norm_mix), "g_ffn": vec(g_norm_ffn),
        "w_in": _bf(w_in), "w_gate": _bf(w_gate), "b_gate": b_gate.astype(F32), "w_branch": _bf(w_branch),
        "w_o": _bf(w_o),
        "na_bias": [_na_bias_table(na_rpb[l]) for l in range(DEPTH)],
        "lb": [[lower[d, l].reshape(1, BRANCH_W) for l in range(DEPTH)] for d in range(2)],
        "hg_gnorm": vec(hg_gnorm), "sg_ln_g": vec(sg_ln_g), "sg_ln_b": vec(sg_ln_b),
        "sg_w": [_bf(sg_w[l].transpose(1, 0, 2).reshape(SG_CHUNK, SG_GROUPS * SG_CHUNK)) for l in range(DEPTH)],
        "sg_b": [jnp.repeat(sg_b[l].T.astype(F32), BRANCH_W // SG_GROUPS, axis=1) for l in range(DEPTH)],
        "w_ffn_gate": _bf(w_ffn_gate), "w_ffn_up": _bf(w_ffn_up), "w_ffn_down": _bf(w_ffn_down),
        "g_final": g_final.reshape(1, D_MODEL).astype(F32),
        "bd": bd, "ones_bd": _bf(bd),
    }
    return (_trunk(x_prompt, mods_p, p), _trunk(x_sample, mods_s, p))
```
